```python
import math
import jax, jax.numpy as jnp
from jax import lax
import numpy as np

D_MODEL = 1024
BATCH = 8
SEQ = 2048
DEPTH = 4

ROPE_THETA = 10000.0
NORM_EPS = 1e-6

BRANCH_W = D_MODEL // 2
N_BRANCH = 3

GLA_HEADS = 4
GLA_DV = BRANCH_W // GLA_HEADS
GLA_DK = GLA_DV // 2
GLA_RANK = 16
GLA_TAU = 16.0
GLA_CHUNK = 64
GLA_QK_W = GLA_HEADS * GLA_DK

DIFF_HEADS = 4
DIFF_D = BRANCH_W // (2 * DIFF_HEADS)
DIFF_QK_W = DIFF_HEADS * 2 * DIFF_D
DIFF_QBLOCK = 128

DIL_HEADS = 8
DIL_HD = BRANCH_W // DIL_HEADS
DIL_W = DIL_HEADS * DIL_HD
DIL_PATTERNS = ((128, 1), (512, 4), (2048, 16))
DIL_QBLOCK = 64

IN_SPLITS = (
    GLA_QK_W, GLA_QK_W, BRANCH_W, BRANCH_W, GLA_RANK, GLA_RANK,
    DIFF_QK_W, DIFF_QK_W, BRANCH_W, BRANCH_W,
    DIL_W, DIL_W, BRANCH_W, BRANCH_W,
    N_BRANCH * D_MODEL,
)
IN_COLS = sum(IN_SPLITS)

kernel_name = "hybrid_gla_diff_dilated_encoder"


def rms_norm(x, g):
    xf = x.astype(jnp.float32)
    y = xf * lax.rsqrt(jnp.mean(xf * xf, axis=-1, keepdims=True) + NORM_EPS)
    return (y * g.astype(jnp.float32)).astype(x.dtype)


def rope_tables(seq, dim):
    inv = 1.0 / (ROPE_THETA ** (jnp.arange(0, dim, 2, dtype=jnp.float32) / dim))
    ang = jnp.arange(seq, dtype=jnp.float32)[:, None] * inv[None, :]
    return jnp.cos(ang), jnp.sin(ang)


def apply_rope(x, cos, sin):
    x1, x2 = jnp.split(x.astype(jnp.float32), 2, axis=-1)
    return jnp.concatenate([x1 * cos - x2 * sin, x2 * cos + x1 * sin], axis=-1).astype(x.dtype)


def split_points():
    pts, acc = [], 0
    for s in IN_SPLITS[:-1]:
        acc += s
        pts.append(acc)
    return pts


def gla_one_direction(q, k, v, log_a):
    B, H, S, DK = q.shape
    DV = v.shape[-1]
    C = GLA_CHUNK
    N = S // C
    q = q.reshape(B, H, N, C, DK)
    k = k.reshape(B, H, N, C, DK)
    v = v.reshape(B, H, N, C, DV)
    b = jnp.cumsum(log_a.reshape(B, H, N, C, DK), axis=3)
    b_last = b[:, :, :, -1:, :]
    q_hat = q * jnp.exp(b)
    k_hat = k * jnp.exp(-b)
    tri = jnp.tril(jnp.ones((C, C), dtype=bool))
    att = jnp.where(tri, jnp.einsum('bhncd,bhnjd->bhncj', q_hat, k_hat), 0.0)
    o_intra = jnp.einsum('bhncj,bhnje->bhnce', att, v)
    k_end = k * jnp.exp(b_last - b)
    upd = jnp.einsum('bhncd,bhnce->nbhde', k_end, v)
    decay = jnp.moveaxis(jnp.exp(b_last[:, :, :, 0, :]), 2, 0)[..., None]

    def step(state, inp):
        d, u = inp
        return d * state + u, state

    _, s_in = lax.scan(step, jnp.zeros((B, H, DK, DV), q.dtype), (decay, upd))
    o_inter = jnp.einsum('bhncd,nbhde->bhnce', q_hat, s_in)
    return (o_intra + o_inter).reshape(B, H, S, DV)


def diff_attention(q, k, v, lam):
    B, H, _, S, d = q.shape
    nb = S // DIFF_QBLOCK
    qb = jnp.moveaxis(q.reshape(B, H, 2, nb, DIFF_QBLOCK, d), 3, 0)
    scale = d ** -0.5

    def block(qblk):
        s = jnp.einsum('bhtqd,bhtkd->bhtqk', qblk, k, preferred_element_type=jnp.float32) * scale
        p = jax.nn.softmax(s, axis=-1)
        w = p[:, :, 0] - lam * p[:, :, 1]
        return jnp.einsum('bhqk,bhke->bhqe', w.astype(v.dtype), v)

    o = lax.map(block, qb)
    return jnp.moveaxis(o, 0, 2).reshape(B, H, S, 2 * d)


def dilated_window_attention(q, k, v, dilation, half):
    B, H, S, hd = q.shape
    Sr = S // dilation
    bq = math.gcd(DIL_QBLOCK, Sr)
    nb = Sr // bq
    L = bq + 2 * half

    def to_classes(t):
        return t.reshape(B, H, Sr, dilation, hd).transpose(0, 1, 3, 2, 4)

    qc, kc, vc = to_classes(q), to_classes(k), to_classes(v)
    pad = ((0, 0), (0, 0), (0, 0), (half, half), (0, 0))
    kp = jnp.pad(kc, pad)
    vp = jnp.pad(vc, pad)
    idx = (jnp.arange(nb) * bq)[:, None] + jnp.arange(L)[None, :]
    kb = kp[:, :, :, idx, :]
    vb = vp[:, :, :, idx, :]
    qb = qc.reshape(B, H, dilation, nb, bq, hd)
    s = jnp.einsum('bhrnqd,bhrnld->bhrnql', qb, kb, preferred_element_type=jnp.float32) * (hd ** -0.5)
    rel = jnp.arange(L)[None, :] - half - jnp.arange(bq)[:, None]
    src = idx[:, None, :] - half
    valid = (jnp.abs(rel) <= half)[None] & (src >= 0) & (src < Sr)
    s = jnp.where(valid, s, -1e30)
    lse = jax.nn.logsumexp(s, axis=-1)
    p = jnp.exp(s - lse[..., None])
    o = jnp.einsum('bhrnql,bhrnld->bhrnqd', p.astype(v.dtype), vb)
    o = o.reshape(B, H, dilation, Sr, hd).transpose(0, 1, 3, 2, 4).reshape(B, H, S, hd)
    lse = lse.reshape(B, H, dilation, Sr).transpose(0, 1, 3, 2).reshape(B, H, S)
    return o, lse


def hybrid_layer(x, layer, cos_b, sin_b, cos_c, sin_c, norm_g, w_in, gla_gate_up, gla_gate_b,
                 gla_norm_g, diff_lambda, diff_norm_g, w_branch, w_out):
    B, S, _ = x.shape
    f32 = jnp.float32
    h = rms_norm(x, norm_g)
    proj = jnp.einsum('bsd,dc->bsc', h, w_in)
    (a_q, a_k, a_v, a_g, a_lf, a_lb,
     b_q, b_k, b_v, b_g,
     c_q, c_k, c_v, c_g, merge) = jnp.split(proj, split_points(), axis=-1)

    def heads(t, n):
        return t.reshape(B, S, n, -1).transpose(0, 2, 1, 3)

    qa = heads(a_q, GLA_HEADS).astype(f32) * (GLA_DK ** -0.5)
    ka = heads(a_k, GLA_HEADS).astype(f32)
    va = heads(a_v, GLA_HEADS).astype(f32)
    la_f = heads(jax.nn.log_sigmoid((a_lf @ gla_gate_up[0] + gla_gate_b[0]).astype(f32)) / GLA_TAU, GLA_HEADS)
    la_b = heads(jax.nn.log_sigmoid((a_lb @ gla_gate_up[1] + gla_gate_b[1]).astype(f32)) / GLA_TAU, GLA_HEADS)
    flip = lambda t: jnp.flip(t, axis=2)
    o_a = gla_one_direction(qa, ka, va, la_f) + flip(gla_one_direction(flip(qa), flip(ka), flip(va), flip(la_b)))
    o_a = rms_norm(o_a.transpose(0, 2, 1, 3), gla_norm_g).reshape(B, S, BRANCH_W).astype(x.dtype)
    y_a = o_a * jax.nn.silu(a_g)

    qb = apply_rope(b_q.reshape(B, S, DIFF_HEADS, 2, DIFF_D).transpose(0, 2, 3, 1, 4), cos_b, sin_b)
    kb = apply_rope(b_k.reshape(B, S, DIFF_HEADS, 2, DIFF_D).transpose(0, 2, 3, 1, 4), cos_b, sin_b)
    vb = heads(b_v, DIFF_HEADS)
    lam_init = 0.8 - 0.6 * math.exp(-0.3 * layer)
    dl = diff_lambda.astype(f32)
    lam = jnp.exp(jnp.sum(dl[0] * dl[1])) - jnp.exp(jnp.sum(dl[2] * dl[3])) + lam_init
    o_b = diff_attention(qb, kb, vb, lam)
    o_b = rms_norm(o_b.transpose(0, 2, 1, 3), diff_norm_g) * (1.0 - lam_init)
    y_b = o_b.reshape(B, S, BRANCH_W).astype(x.dtype) * jax.nn.silu(b_g)

    qc = apply_rope(heads(c_q, DIL_HEADS), cos_c, sin_c)
    kc = apply_rope(heads(c_k, DIL_HEADS), cos_c, sin_c)
    vc = heads(c_v, DIL_HEADS)
    outs, lses = [], []
    for window, dil in DIL_PATTERNS:
        o_g, lse_g = dilated_window_attention(qc, kc, vc, dil, window // (2 * dil))
        outs.append(o_g)
        lses.append(lse_g)
    wts = jax.nn.softmax(jnp.stack(lses, axis=0), axis=0)
    o_c = jnp.einsum('gbhs,gbhsd->bhsd', wts, jnp.stack(outs, axis=0).astype(f32))
    y_c = o_c.transpose(0, 2, 1, 3).reshape(B, S, BRANCH_W).astype(x.dtype) * jax.nn.silu(c_g)

    y = jnp.stack([y_a, y_b, y_c], axis=2)
    z = jnp.einsum('bsgw,gwd->bsgd', y, w_branch)
    gates = jax.nn.sigmoid(merge.reshape(B, S, N_BRANCH, D_MODEL))
    mixed = jnp.sum(gates * z, axis=2)
    return x + jnp.einsum('bsd,de->bse', mixed, w_out)


def setup_inputs(seed: int = 0) -> dict:
    key = jax.random.key(seed)
    ks = jax.random.split(key, 12)
    nrm = jax.random.normal
    return {
        "x": nrm(ks[0], (BATCH, SEQ, D_MODEL), jnp.float32),
        "norm_g": 1.0 + 0.02 * nrm(ks[1], (DEPTH, D_MODEL), jnp.float32),
        "w_in": nrm(ks[2], (DEPTH, D_MODEL, IN_COLS), jnp.float32) * D_MODEL ** -0.5,
        "gla_gate_up": nrm(ks[3], (DEPTH, 2, GLA_RANK, GLA_QK_W), jnp.float32) * GLA_RANK ** -0.5,
        "gla_gate_b": 0.02 * nrm(ks[4], (DEPTH, 2, GLA_QK_W), jnp.float32),
        "gla_norm_g": 1.0 + 0.02 * nrm(ks[5], (DEPTH, GLA_DV), jnp.float32),
        "diff_lambda": 0.1 * nrm(ks[6], (DEPTH, 4, DIFF_D), jnp.float32),
        "diff_norm_g": 1.0 + 0.02 * nrm(ks[7], (DEPTH, 2 * DIFF_D), jnp.float32),
        "w_branch": nrm(ks[8], (DEPTH, N_BRANCH, BRANCH_W, D_MODEL), jnp.float32) * BRANCH_W ** -0.5,
        "w_out": nrm(ks[9], (DEPTH, D_MODEL, D_MODEL), jnp.float32) * D_MODEL ** -0.5,
        "final_norm_g": 1.0 + 0.02 * nrm(ks[10], (D_MODEL,), jnp.float32),
    }


def reference(x, norm_g, w_in, gla_gate_up, gla_gate_b, gla_norm_g, diff_lambda, diff_norm_g,
              w_branch, w_out, final_norm_g):
    S = x.shape[1]
    cos_b, sin_b = rope_tables(S, DIFF_D)
    cos_c, sin_c = rope_tables(S, DIL_HD)
    for layer in range(DEPTH):
        x = hybrid_layer(x, layer, cos_b, sin_b, cos_c, sin_c, norm_g[layer], w_in[layer],
                         gla_gate_up[layer], gla_gate_b[layer], gla_norm_g[layer],
                         diff_lambda[layer], diff_norm_g[layer], w_branch[layer], w_out[layer])
    return rms_norm(x, final_norm_g)
```

```python
import functools
import math

import jax
import jax.numpy as jnp
from jax import lax
from jax.experimental import pallas as pl
from jax.experimental.pallas import tpu as pltpu

F32 = jnp.float32
BF16 = jnp.bfloat16
HIGHEST = lax.Precision.HIGHEST

LANES = 128

D_MODEL = 1024
ROPE_THETA = 10000.0
NORM_EPS = 1e-6
BRANCH_W = D_MODEL // 2
N_BRANCH = 3

GLA_HEADS = 4
GLA_DV = BRANCH_W // GLA_HEADS
GLA_DK = GLA_DV // 2
GLA_RANK = 16
GLA_TAU = 16.0
GLA_CHUNK = 64
GLA_QK_W = GLA_HEADS * GLA_DK

DIFF_HEADS = 4
DIFF_D = BRANCH_W // (2 * DIFF_HEADS)
DIFF_QK_W = DIFF_HEADS * 2 * DIFF_D

DIL_HEADS = 8
DIL_HD = BRANCH_W // DIL_HEADS
DIL_W = DIL_HEADS * DIL_HD
DIL_PATTERNS = ((128, 1), (512, 4), (2048, 16))
DIL_HALF = 64
NEG_INF = -1e30

MERGE_W = N_BRANCH * D_MODEL
LOWRANK_OFF = GLA_QK_W * 2 + BRANCH_W * 2
LOWRANK_W = 2 * GLA_RANK
IN_COLS = LOWRANK_OFF + LOWRANK_W + 2 * (DIFF_QK_W * 2 + BRANCH_W * 2) + MERGE_W
MAIN_W = IN_COLS - LOWRANK_W

OFF_AQ = MERGE_W
OFF_AK = OFF_AQ + GLA_QK_W
OFF_AV = OFF_AK + GLA_QK_W
OFF_AG = OFF_AV + BRANCH_W
OFF_BQ = OFF_AG + BRANCH_W
OFF_BK = OFF_BQ + DIFF_QK_W
OFF_BV = OFF_BK + DIFF_QK_W
OFF_BG = OFF_BV + BRANCH_W
OFF_CQ = OFF_BG + BRANCH_W
OFF_CK = OFF_CQ + DIL_W
OFF_CV = OFF_CK + DIL_W
OFF_CG = OFF_CV + BRANCH_W

VMEM_LIMIT = 48 * 1024 * 1024


def _dot(a, b, **kw):
    return jnp.dot(a, b, preferred_element_type=F32, **kw)


def _dot_nt(a, b):
    return lax.dot_general(a, b, (((1,), (1,)), ((), ())), preferred_element_type=F32)


def _sigmoid(x):
    return 1.0 / (1.0 + jnp.exp(-x))


def _silu(x):
    return x * _sigmoid(x)


def _lane_iota(shape):
    return lax.broadcasted_iota(jnp.int32, shape, len(shape) - 1)


def _rope(x, cos, sin_signed):
    lane = _lane_iota(x.shape)
    first_half = (lane % 64) < 32
    partner = jnp.where(first_half, pltpu.roll(x, 96, 1), pltpu.roll(x, 32, 1))
    return x * cos + partner * sin_signed


def _in_proj_kernel(x_ref, g_ref, w_ref, wl_ref, gu_ref, gb_ref, out_ref, la_ref, h_ref):
    @pl.when(pl.program_id(1) == 0)
    def _():
        x = x_ref[...]
        ms = jnp.mean(x * x, axis=-1, keepdims=True)
        hb = (x * lax.rsqrt(ms + NORM_EPS) * g_ref[...]).astype(BF16)
        h_ref[...] = hb
        low = _dot(hb, wl_ref[...])
        z = _dot(low, gu_ref[...], precision=HIGHEST) + gb_ref[...]
        log_sig = jnp.minimum(z, 0.0) - jnp.log1p(jnp.exp(-jnp.abs(z)))
        la_ref[...] = log_sig * (1.0 / GLA_TAU)

    out_ref[...] = _dot(h_ref[...], w_ref[...]).astype(BF16)


def _in_proj(x2, g, w_main, w_low, gate_up, gate_b, tm=512, tn=2176):
    rows = x2.shape[0]
    grid = (rows // tm, MAIN_W // tn)
    return pl.pallas_call(
        _in_proj_kernel,
        grid=grid,
        in_specs=[
            pl.BlockSpec((tm, D_MODEL), lambda i, j: (i, 0)),
            pl.BlockSpec((1, D_MODEL), lambda i, j: (0, 0)),
            pl.BlockSpec((D_MODEL, tn), lambda i, j: (0, j)),
            pl.BlockSpec((D_MODEL, LANES), lambda i, j: (0, 0)),
            pl.BlockSpec((LANES, 4 * LANES), lambda i, j: (0, 0)),
            pl.BlockSpec((1, 4 * LANES), lambda i, j: (0, 0)),
        ],
        out_specs=[
            pl.BlockSpec((tm, tn), lambda i, j: (i, j)),
            pl.BlockSpec((tm, 4 * LANES), lambda i, j: (i, 0)),
        ],
        out_shape=[
            jax.ShapeDtypeStruct((rows, MAIN_W), BF16),
            jax.ShapeDtypeStruct((rows, 4 * LANES), F32),
        ],
        scratch_shapes=[pltpu.VMEM((tm, D_MODEL), BF16)],
        compiler_params=pltpu.CompilerParams(
            dimension_semantics=("arbitrary", "arbitrary"), vmem_limit_bytes=VMEM_LIMIT),
        name="in_proj",
    )(x2, g, w_main, w_low, gate_up, gate_b)


def _gla_kernel(q_ref, k_ref, v_ref, g_ref, la_ref, gn_ref, out_ref,
                u_sc, d_sc, st_sc, qh_sc, kh_sc, *, seq):
    C = GLA_CHUNK
    n_chunks = seq // C
    lane_c = _lane_iota((C, LANES))
    row_c = lax.broadcasted_iota(jnp.int32, (C, LANES), 0)
    fwd_c = lane_c < GLA_DK
    tri = jnp.where(fwd_c, row_c - lane_c, lane_c - GLA_DK - row_c) >= 0
    tri_f32 = tri.astype(F32)
    fwd_sq = _lane_iota((LANES, LANES)) < GLA_DK
    scale = GLA_DK ** -0.5

    for hh in range(2):
        cols = slice(hh * LANES, (hh + 1) * LANES)

        def dup(ref, rows):
            xx = ref[rows, :].astype(F32)
            rolled = pltpu.roll(xx, GLA_DK, 1)
            return jnp.where(fwd_c, xx, rolled) if hh == 0 else jnp.where(fwd_c, rolled, xx)

        def prep(n, carry):
            rows = pl.ds(pl.multiple_of(n * C, C), C)
            la = la_ref[rows, cols]
            la_bd = jnp.concatenate([jnp.where(fwd_c, la, 0.0), jnp.where(fwd_c, 0.0, la)], axis=0)
            b = _dot(tri_f32, la_bd, precision=HIGHEST)
            tot = jnp.sum(la, axis=0, keepdims=True)
            kc = dup(k_ref, rows)
            qc = dup(q_ref, rows)
            qh_sc[rows, :] = (qc * jnp.exp(b) * scale).astype(BF16)
            k_hat = kc * jnp.exp(-b)
            kh_sc[n] = jnp.concatenate(
                [jnp.where(fwd_c, k_hat, 0.0), jnp.where(fwd_c, 0.0, k_hat)], axis=0).astype(BF16)
            k_end = (kc * jnp.exp(tot - b)).astype(BF16)
            v_t = v_ref[rows, cols].astype(F32).T.astype(BF16)
            u_sc[n] = _dot(v_t, k_end)
            d_sc[n] = jnp.broadcast_to(jnp.exp(tot), (8, LANES))
            return carry

        lax.fori_loop(0, n_chunks, prep, 0)

        def scan(i, state):
            st_sc[i] = state
            j = n_chunks - 1 - i
            upd = jnp.where(fwd_sq, u_sc[i], u_sc[j])
            dec = jnp.where(fwd_sq[:8], d_sc[i], d_sc[j])[0:1]
            return state * dec + upd

        lax.fori_loop(0, n_chunks, scan, jnp.zeros((LANES, LANES), F32))

        def emit(n, carry):
            rows = pl.ds(pl.multiple_of(n * C, C), C)
            qh = qh_sc[rows, :]
            att = _dot_nt(qh, kh_sc[n])
            att = jnp.where(tri, att, 0.0).astype(BF16)
            v = v_ref[rows, cols]
            o = _dot(att, jnp.concatenate([v, v], axis=0))
            state = jnp.where(fwd_sq, st_sc[n], st_sc[n_chunks - 1 - n]).astype(BF16)
            o = o + _dot_nt(qh, state)
            ms = jnp.mean(o * o, axis=-1, keepdims=True)
            y = o * lax.rsqrt(ms + NORM_EPS) * gn_ref[...]
            out_ref[rows, cols] = (y * _silu(g_ref[rows, cols].astype(F32))).astype(BF16)
            return carry

        lax.fori_loop(0, n_chunks, emit, 0)


def _gla(proj, la, gn, batch, seq):
    n_chunks = seq // GLA_CHUNK
    pair = 2 * LANES
    return pl.pallas_call(
        functools.partial(_gla_kernel, seq=seq),
        grid=(batch, GLA_HEADS // 2),
        in_specs=[
            pl.BlockSpec((seq, LANES), lambda b, p: (b, OFF_AQ // LANES + p)),
            pl.BlockSpec((seq, LANES), lambda b, p: (b, OFF_AK // LANES + p)),
            pl.BlockSpec((seq, pair), lambda b, p: (b, OFF_AV // pair + p)),
            pl.BlockSpec((seq, pair), lambda b, p: (b, OFF_AG // pair + p)),
            pl.BlockSpec((seq, pair), lambda b, p: (b, p)),
            pl.BlockSpec((1, LANES), lambda b, p: (0, 0)),
        ],
        out_specs=pl.BlockSpec((seq, pair), lambda b, p: (b, p)),
        out_shape=jax.ShapeDtypeStruct((batch * seq, BRANCH_W), BF16),
        scratch_shapes=[
            pltpu.VMEM((n_chunks, LANES, LANES), F32),
            pltpu.VMEM((n_chunks, 8, LANES), F32),
            pltpu.VMEM((n_chunks, LANES, LANES), F32),
            pltpu.VMEM((seq, LANES), BF16),
            pltpu.VMEM((n_chunks, LANES, LANES), BF16),
        ],
        compiler_params=pltpu.CompilerParams(
            dimension_semantics=("arbitrary", "arbitrary"), vmem_limit_bytes=VMEM_LIMIT),
        name="gla",
    )(proj, proj, proj, proj, la, gn)


def _diff_kernel(q_ref, k_ref, v_ref, g_ref, cos_ref, sin_ref, dl_ref, li_ref, gn_ref, out_ref,
                 k_sc, *, seq, tq):
    half0 = _lane_iota((tq, LANES)) < DIFF_D
    scale = DIFF_D ** -0.5

    def rope_k(i, carry):
        rows = pl.ds(pl.multiple_of(i * tq, tq), tq)
        k_sc[rows, :] = _rope(k_ref[rows, :].astype(F32), cos_ref[rows, :], sin_ref[rows, :]).astype(BF16)
        return carry

    lax.fori_loop(0, seq // tq, rope_k, 0)

    dl = dl_ref[...]
    lam_init = li_ref[...]
    lam = (jnp.exp(jnp.sum(dl[0:1] * dl[1:2], axis=-1, keepdims=True))
           - jnp.exp(jnp.sum(dl[2:3] * dl[3:4], axis=-1, keepdims=True)) + lam_init)

    def block(i, carry):
        rows = pl.ds(pl.multiple_of(i * tq, tq), tq)
        q = _rope(q_ref[rows, :].astype(F32), cos_ref[rows, :], sin_ref[rows, :]) * scale
        kk = k_sc[...]
        vv = v_ref[...]

        def softmax_pv(qm):
            s = _dot_nt(qm.astype(BF16), kk)
            m = jnp.max(s, axis=-1, keepdims=True)
            e = jnp.exp(s - m)
            l = jnp.sum(e, axis=-1, keepdims=True)
            return _dot(e.astype(BF16), vv) * (1.0 / l)

        o = softmax_pv(jnp.where(half0, q, 0.0)) - lam * softmax_pv(jnp.where(half0, 0.0, q))
        ms = jnp.mean(o * o, axis=-1, keepdims=True)
        y = o * lax.rsqrt(ms + NORM_EPS) * gn_ref[...] * (1.0 - lam_init)
        out_ref[rows, :] = (y * _silu(g_ref[rows, :].astype(F32))).astype(BF16)
        return carry

    lax.fori_loop(0, seq // tq, block, 0)


def _diff(proj, cos, sin, dl, lam_init, gn, batch, seq, tq=256):
    blk = lambda off: pl.BlockSpec((seq, LANES), lambda b, h: (b, off // LANES + h))
    const = lambda shape: pl.BlockSpec(shape, lambda b, h: (0, 0))
    return pl.pallas_call(
        functools.partial(_diff_kernel, seq=seq, tq=tq),
        grid=(batch, DIFF_HEADS),
        in_specs=[blk(OFF_BQ), blk(OFF_BK), blk(OFF_BV), blk(OFF_BG),
                  const((seq, LANES)), const((seq, LANES)),
                  const((4, DIFF_D)), const((1, 1)), const((1, LANES))],
        out_specs=pl.BlockSpec((seq, LANES), lambda b, h: (b, h)),
        out_shape=jax.ShapeDtypeStruct((batch * seq, BRANCH_W), BF16),
        scratch_shapes=[pltpu.VMEM((seq, LANES), BF16)],
        compiler_params=pltpu.CompilerParams(
            dimension_semantics=("arbitrary", "arbitrary"), vmem_limit_bytes=VMEM_LIMIT),
        name="diff",
    )(proj, proj, proj, proj, cos, sin, dl, lam_init, gn)


def _band_attend(q, k, v, valid):
    head0 = _lane_iota(q.shape) < DIL_HD
    kb = k.astype(BF16)
    vb = v.astype(BF16)

    def one(qm):
        s = jnp.where(valid, _dot_nt(qm.astype(BF16), kb), NEG_INF)
        m = jnp.max(s, axis=-1, keepdims=True)
        e = jnp.exp(s - m)
        l = jnp.sum(e, axis=-1, keepdims=True)
        return _dot(e.astype(BF16), vb) * (1.0 / l), m + jnp.log(l)

    o0, lse0 = one(jnp.where(head0, q, 0.0))
    o1, lse1 = one(jnp.where(head0, 0.0, q))
    return jnp.where(head0, o0, o1), jnp.where(head0, lse0, lse1)


def _dilated_kernel(q_ref, k_ref, v_ref, g_ref, cos_ref, sin_ref, out_ref,
                    q_sc, k_sc, v_sc, o_sc, l_sc, *, seq, tq):
    scale = DIL_HD ** -0.5
    n_blk = seq // tq

    def stage(i, carry):
        rows = pl.ds(pl.multiple_of(i * tq, tq), tq)
        cos = cos_ref[rows, :]
        sin = sin_ref[rows, :]
        q_sc[rows, :] = _rope(q_ref[rows, :].astype(F32), cos, sin) * scale
        k_sc[rows, :] = _rope(k_ref[rows, :].astype(F32), cos, sin)
        v_sc[rows, :] = v_ref[rows, :].astype(F32)
        return carry

    lax.fori_loop(0, n_blk, stage, 0)

    win = 2 * tq
    delta = (lax.broadcasted_iota(jnp.int32, (tq, win), 1)
             - lax.broadcasted_iota(jnp.int32, (tq, win), 0))

    def dil1(i, carry):
        r0 = pl.multiple_of(i * tq, tq)
        ks = pl.multiple_of(jnp.clip(r0 - tq // 2, 0, seq - win), tq // 2)
        valid = jnp.abs(delta - (r0 - ks)) <= DIL_HALF
        o, lse = _band_attend(q_sc[pl.ds(r0, tq), :], k_sc[pl.ds(ks, win), :], v_sc[pl.ds(ks, win), :], valid)
        o_sc[0, pl.ds(r0, tq), :] = o
        l_sc[0, pl.ds(r0, tq), :] = lse
        return carry

    lax.fori_loop(0, n_blk, dil1, 0)

    len4 = seq // 4
    delta4 = (lax.broadcasted_iota(jnp.int32, (tq, len4), 1)
              - lax.broadcasted_iota(jnp.int32, (tq, len4), 0))

    def dil4(r, carry):
        kc = k_sc[pl.ds(r, len4, stride=4), :]
        vc = v_sc[pl.ds(r, len4, stride=4), :]
        for blk in range(len4 // tq):
            qc = q_sc[pl.ds(r + 4 * blk * tq, tq, stride=4), :]
            valid = jnp.abs(delta4 - blk * tq) <= DIL_HALF
            o, lse = _band_attend(qc, kc, vc, valid)
            o_sc[1, pl.ds(r + 4 * blk * tq, tq, stride=4), :] = o
            l_sc[1, pl.ds(r + 4 * blk * tq, tq, stride=4), :] = lse
        return carry

    lax.fori_loop(0, 4, dil4, 0)

    len16 = seq // 16
    valid16 = jnp.abs(lax.broadcasted_iota(jnp.int32, (len16, len16), 1)
                      - lax.broadcasted_iota(jnp.int32, (len16, len16), 0)) <= DIL_HALF

    def dil16(r, carry):
        sl = pl.ds(r, len16, stride=16)
        o, lse = _band_attend(q_sc[sl, :], k_sc[sl, :], v_sc[sl, :], valid16)
        o_sc[2, sl, :] = o
        l_sc[2, sl, :] = lse
        return carry

    lax.fori_loop(0, 16, dil16, 0)

    def combine(i, carry):
        rows = pl.ds(pl.multiple_of(i * tq, tq), tq)
        l0, l1, l2 = l_sc[0, rows, :], l_sc[1, rows, :], l_sc[2, rows, :]
        m = jnp.maximum(jnp.maximum(l0, l1), l2)
        w0, w1, w2 = jnp.exp(l0 - m), jnp.exp(l1 - m), jnp.exp(l2 - m)
        o = (w0 * o_sc[0, rows, :] + w1 * o_sc[1, rows, :] + w2 * o_sc[2, rows, :]) * (1.0 / (w0 + w1 + w2))
        out_ref[rows, :] = (o * _silu(g_ref[rows, :].astype(F32))).astype(BF16)
        return carry

    lax.fori_loop(0, n_blk, combine, 0)


def _dilated(proj, cos, sin, batch, seq, tq=256):
    assert seq // 4 == 2 * tq and seq // 16 == LANES
    blk = lambda off: pl.BlockSpec((seq, LANES), lambda b, p: (b, off // LANES + p))
    const = lambda shape: pl.BlockSpec(shape, lambda b, p: (0, 0))
    return pl.pallas_call(
        functools.partial(_dilated_kernel, seq=seq, tq=tq),
        grid=(batch, DIL_HEADS // 2),
        in_specs=[blk(OFF_CQ), blk(OFF_CK), blk(OFF_CV), blk(OFF_CG),
                  const((seq, LANES)), const((seq, LANES))],
        out_specs=pl.BlockSpec((seq, LANES), lambda b, p: (b, p)),
        out_shape=jax.ShapeDtypeStruct((batch * seq, BRANCH_W), BF16),
        scratch_shapes=[
            pltpu.VMEM((seq, LANES), F32), pltpu.VMEM((seq, LANES), F32), pltpu.VMEM((seq, LANES), F32),
            pltpu.VMEM((3, seq, LANES), F32), pltpu.VMEM((3, seq, LANES), F32),
        ],
        compiler_params=pltpu.CompilerParams(
            dimension_semantics=("arbitrary", "arbitrary"), vmem_limit_bytes=VMEM_LIMIT),
        name="dilated",
    )(proj, proj, proj, proj, cos, sin)


def _merge_kernel(x_ref, ya_ref, yb_ref, yc_ref, m_ref, wb_ref, wo_ref, out_ref):
    mixed = None
    for gi, y_ref in enumerate((ya_ref, yb_ref, yc_ref)):
        z = _dot(y_ref[...], wb_ref[gi])
        gate = _sigmoid(m_ref[:, gi * D_MODEL:(gi + 1) * D_MODEL].astype(F32))
        mixed = gate * z if mixed is None else mixed + gate * z
    out_ref[...] = x_ref[...] + _dot(mixed.astype(BF16), wo_ref[...])


def _merge(x2, ya, yb, yc, proj, wb, wo, tm=512):
    rows = x2.shape[0]
    yspec = pl.BlockSpec((tm, BRANCH_W), lambda i: (i, 0))
    return pl.pallas_call(
        _merge_kernel,
        grid=(rows // tm,),
        in_specs=[
            pl.BlockSpec((tm, D_MODEL), lambda i: (i, 0)),
            yspec, yspec, yspec,
            pl.BlockSpec((tm, MERGE_W), lambda i: (i, 0)),
            pl.BlockSpec((N_BRANCH, BRANCH_W, D_MODEL), lambda i: (0, 0, 0)),
            pl.BlockSpec((D_MODEL, D_MODEL), lambda i: (0, 0)),
        ],
        out_specs=pl.BlockSpec((tm, D_MODEL), lambda i: (i, 0)),
        out_shape=jax.ShapeDtypeStruct((rows, D_MODEL), F32),
        compiler_params=pltpu.CompilerParams(
            dimension_semantics=("arbitrary",), vmem_limit_bytes=VMEM_LIMIT),
        name="merge",
    )(x2, ya, yb, yc, proj, wb, wo)


def _final_norm_kernel(x_ref, g_ref, out_ref):
    x = x_ref[...]
    ms = jnp.mean(x * x, axis=-1, keepdims=True)
    out_ref[...] = x * lax.rsqrt(ms + NORM_EPS) * g_ref[...]


def _final_norm(x2, g, tm=1024):
    rows = x2.shape[0]
    return pl.pallas_call(
        _final_norm_kernel,
        grid=(rows // tm,),
        in_specs=[pl.BlockSpec((tm, D_MODEL), lambda i: (i, 0)),
                  pl.BlockSpec((1, D_MODEL), lambda i: (0, 0))],
        out_specs=pl.BlockSpec((tm, D_MODEL), lambda i: (i, 0)),
        out_shape=jax.ShapeDtypeStruct((rows, D_MODEL), F32),
        compiler_params=pltpu.CompilerParams(dimension_semantics=("arbitrary",)),
        name="final_norm",
    )(x2, g)


def _rope_tables(seq, dim):
    inv = 1.0 / (ROPE_THETA ** (jnp.arange(0, dim, 2, dtype=F32) / dim))
    ang = jnp.arange(seq, dtype=F32)[:, None] * inv[None, :]
    cos, sin = jnp.cos(ang), jnp.sin(ang)
    reps = LANES // dim
    cos_t = jnp.tile(jnp.concatenate([cos, cos], axis=-1), (1, reps))
    sin_t = jnp.tile(jnp.concatenate([-sin, sin], axis=-1), (1, reps))
    return cos_t, sin_t


def kernel(x, norm_g, w_in, gla_gate_up, gla_gate_b, gla_norm_g, diff_lambda, diff_norm_g, w_branch, w_out, final_norm_g):
    batch, seq, _ = x.shape
    depth = w_in.shape[0]
    cos_b, sin_b = _rope_tables(seq, DIFF_D)
    cos_c, sin_c = _rope_tables(seq, DIL_HD)

    lo, hi = LOWRANK_OFF, LOWRANK_OFF + LOWRANK_W
    merge_off = IN_COLS - MERGE_W
    w_main = jnp.concatenate([w_in[:, :, merge_off:], w_in[:, :, :lo], w_in[:, :, hi:merge_off]],
                             axis=-1).astype(BF16)
    w_low = jnp.pad(w_in[:, :, lo:hi], ((0, 0), (0, 0), (0, LANES - LOWRANK_W))).astype(BF16)
    gu = gla_gate_up.reshape(depth, 2, GLA_RANK, GLA_HEADS, GLA_DK)
    zeros = jnp.zeros_like(gu[:, 0])
    gu_f = jnp.concatenate([gu[:, 0], zeros], axis=-1).reshape(depth, GLA_RANK, GLA_HEADS * LANES)
    gu_b = jnp.concatenate([zeros, gu[:, 1]], axis=-1).reshape(depth, GLA_RANK, GLA_HEADS * LANES)
    gate_up = jnp.pad(jnp.concatenate([gu_f, gu_b], axis=1), ((0, 0), (0, LANES - LOWRANK_W), (0, 0)))
    gb = gla_gate_b.reshape(depth, 2, GLA_HEADS, GLA_DK)
    gate_b = jnp.concatenate([gb[:, 0], gb[:, 1]], axis=-1).reshape(depth, 1, GLA_HEADS * LANES)
    wb = w_branch.astype(BF16)
    wo = w_out.astype(BF16)

    x2 = x.reshape(batch * seq, D_MODEL)
    for layer in range(depth):
        lam_init = jnp.full((1, 1), 0.8 - 0.6 * math.exp(-0.3 * layer), F32)
        proj, la = _in_proj(x2, norm_g[layer][None], w_main[layer], w_low[layer], gate_up[layer], gate_b[layer])
        ya = _gla(proj, la, gla_norm_g[layer][None], batch, seq)
        yb = _diff(proj, cos_b, sin_b, diff_lambda[layer], lam_init, diff_norm_g[layer][None], batch, seq)
        yc = _dilated(proj, cos_c, sin_c, batch, seq)
        x2 = _merge(x2, ya, yb, yc, proj, wb[layer], wo[layer])
    return _final_norm(x2, final_norm_g[None]).reshape(batch, seq, D_MODEL)
```

```python
import functools
import math

import jax
import jax.numpy as jnp
from jax import lax
from jax.experimental import pallas as pl
from jax.experimental.pallas import tpu as pltpu

F32 = jnp.float32
BF16 = jnp.bfloat16

LANES = 128

D_MODEL = 1024
ROPE_THETA = 10000.0
NORM_EPS = 1e-6
BRANCH_W = D_MODEL // 2
N_BRANCH = 3

GLA_HEADS = 4
GLA_DV = BRANCH_W // GLA_HEADS
GLA_DK = GLA_DV // 2
GLA_RANK = 16
GLA_TAU = 16.0
GLA_CHUNK = 64
GLA_QK_W = GLA_HEADS * GLA_DK

DIFF_HEADS = 4
DIFF_D = BRANCH_W // (2 * DIFF_HEADS)
DIFF_QK_W = DIFF_HEADS * 2 * DIFF_D

DIL_HEADS = 8
DIL_HD = BRANCH_W // DIL_HEADS
DIL_W = DIL_HEADS * DIL_HD
DIL_PATTERNS = ((128, 1), (512, 4), (2048, 16))
DIL_HALF = 64
DIL_TQ = 128
NEG_INF = -1e30

MERGE_W = N_BRANCH * D_MODEL
LOWRANK_OFF = GLA_QK_W * 2 + BRANCH_W * 2
LOWRANK_W = 2 * GLA_RANK
IN_COLS = LOWRANK_OFF + LOWRANK_W + 2 * (DIFF_QK_W * 2 + BRANCH_W * 2) + MERGE_W
MAIN_W = IN_COLS - LOWRANK_W

OFF_AQ = MERGE_W
OFF_AK = OFF_AQ + GLA_QK_W
OFF_AV = OFF_AK + GLA_QK_W
OFF_AG = OFF_AV + BRANCH_W
OFF_BQ = OFF_AG + BRANCH_W
OFF_BK = OFF_BQ + DIFF_QK_W
OFF_BV = OFF_BK + DIFF_QK_W
OFF_BG = OFF_BV + BRANCH_W
OFF_CQ = OFF_BG + BRANCH_W
OFF_CK = OFF_CQ + DIL_W
OFF_CV = OFF_CK + DIL_W
OFF_CG = OFF_CV + BRANCH_W

VMEM_LIMIT = 56 * 1024 * 1024


def _dot(a, b):
    return jnp.dot(a, b, preferred_element_type=F32)


def _dot_nt(a, b):
    return lax.dot_general(a, b, (((1,), (1,)), ((), ())), preferred_element_type=F32)


def _sigmoid(x):
    return 1.0 / (1.0 + jnp.exp(-x))


def _silu(x):
    return x * _sigmoid(x)


def _lane_iota(shape):
    return lax.broadcasted_iota(jnp.int32, shape, len(shape) - 1)


def _split_bf16(x):
    hi = x.astype(BF16).astype(F32)
    return hi, x - hi


def _rope(x, cos, sin_signed):
    lane = _lane_iota(x.shape)
    first_half = (lane % 64) < 32
    partner = jnp.where(first_half, pltpu.roll(x, 96, 1), pltpu.roll(x, 32, 1))
    return x * cos + partner * sin_signed


def _in_proj_kernel(x_ref, g_ref, w_ref, wl_ref, gu_ref, gb_ref, out_ref, la_ref, *, n_split):
    x = x_ref[...]
    ms = jnp.mean(x * x, axis=-1, keepdims=True)
    hb = (x * lax.rsqrt(ms + NORM_EPS) * g_ref[...]).astype(BF16)
    low = _dot(hb, wl_ref[...])
    low_hi, low_lo = _split_bf16(low)
    lhs = jnp.where(_lane_iota(low.shape) < 2 * LOWRANK_W, low_hi, low_lo).astype(BF16)
    z = _dot(lhs, gu_ref[...]) + gb_ref[...]
    log_sig = jnp.minimum(z, 0.0) - jnp.log1p(jnp.exp(-jnp.abs(z)))
    la_ref[...] = log_sig * (1.0 / GLA_TAU)
    tn = MAIN_W // n_split
    for c in range(n_split):
        out_ref[:, c * tn:(c + 1) * tn] = _dot(hb, w_ref[:, c * tn:(c + 1) * tn]).astype(BF16)


def _in_proj(x2, g, w_main, w_low, gate_up, gate_b, layer, tm=512, n_split=4):
    rows = x2.shape[0]
    resident = dict(pipeline_mode=pl.Buffered(1))
    return pl.pallas_call(
        functools.partial(_in_proj_kernel, n_split=n_split),
        grid=(rows // tm,),
        in_specs=[
            pl.BlockSpec((tm, D_MODEL), lambda i: (i, 0)),
            pl.BlockSpec((None, 1, D_MODEL), lambda i: (layer, 0, 0)),
            pl.BlockSpec((None, D_MODEL, MAIN_W), lambda i: (layer, 0, 0), **resident),
            pl.BlockSpec((None, D_MODEL, LANES), lambda i: (layer, 0, 0), **resident),
            pl.BlockSpec((None, LANES, 4 * LANES), lambda i: (layer, 0, 0), **resident),
            pl.BlockSpec((None, 1, 4 * LANES), lambda i: (layer, 0, 0)),
        ],
        out_specs=[
            pl.BlockSpec((tm, MAIN_W), lambda i: (i, 0)),
            pl.BlockSpec((tm, 4 * LANES), lambda i: (i, 0)),
        ],
        out_shape=[
            jax.ShapeDtypeStruct((rows, MAIN_W), BF16),
            jax.ShapeDtypeStruct((rows, 4 * LANES), F32),
        ],
        compiler_params=pltpu.CompilerParams(
            dimension_semantics=("arbitrary",), vmem_limit_bytes=VMEM_LIMIT),
        name="in_proj",
    )(x2, g, w_main, w_low, gate_up, gate_b)


def _gla_kernel(q_ref, k_ref, v_ref, g_ref, la_ref, gn_ref, out_ref,
                u_sc, d_sc, st_sc, qh_sc, kh_sc, att_sc, *, seq, unroll):
    C = GLA_CHUNK
    n_chunks = seq // C
    lane_c = _lane_iota((C, LANES))
    row_c = lax.broadcasted_iota(jnp.int32, (C, LANES), 0)
    fwd_c = lane_c < GLA_DK
    tri = jnp.where(fwd_c, row_c - lane_c, lane_c - GLA_DK - row_c) >= 0
    tri_bf = tri.astype(F32).astype(BF16)
    tri2 = jnp.concatenate([tri_bf, tri_bf], axis=1)
    fwd_sq = _lane_iota((LANES, LANES)) < GLA_DK
    scale = GLA_DK ** -0.5

    def head_cols(hh):
        return slice(hh * LANES, (hh + 1) * LANES)

    def dup_heads(ref, rows):
        xx = ref[rows, :].astype(F32)
        rolled = pltpu.roll(xx, GLA_DK, 1)
        return jnp.where(fwd_c, xx, rolled), jnp.where(fwd_c, rolled, xx)

    def prep(n, carry):
        rows = pl.ds(pl.multiple_of(n * C, C), C)
        q2 = dup_heads(q_ref, rows)
        k2 = dup_heads(k_ref, rows)
        for hh in range(2):
            cols = head_cols(hh)
            la = la_ref[rows, cols]
            la_bd = jnp.concatenate([jnp.where(fwd_c, la, 0.0), jnp.where(fwd_c, 0.0, la)], axis=0)
            la_hi, la_lo = _split_bf16(la_bd)
            b = _dot(tri2, jnp.concatenate([la_hi, la_lo], axis=0).astype(BF16))
            tot = jnp.sum(la, axis=0, keepdims=True)
            qh_sc[hh, rows, :] = (q2[hh] * jnp.exp(b) * scale).astype(BF16)
            k_hat = k2[hh] * jnp.exp(-b)
            kh_sc[hh, n] = jnp.concatenate(
                [jnp.where(fwd_c, k_hat, 0.0), jnp.where(fwd_c, 0.0, k_hat)], axis=0).astype(BF16)
            k_end = (k2[hh] * jnp.exp(tot - b)).astype(BF16)
            v_t = v_ref[rows, cols].astype(F32).T.astype(BF16)
            u_sc[hh, n] = _dot(v_t, k_end)
            d_sc[hh, n] = jnp.broadcast_to(jnp.exp(tot), (8, LANES))
        return carry

    lax.fori_loop(0, n_chunks, prep, 0, unroll=unroll)

    def scan(i, states):
        j = n_chunks - 1 - i
        new = []
        for hh in range(2):
            st_sc[hh, i] = states[hh]
            upd = jnp.where(fwd_sq, u_sc[hh, i], u_sc[hh, j])
            dec = jnp.where(fwd_sq[:8], d_sc[hh, i], d_sc[hh, j])[0:1]
            new.append(states[hh] * dec + upd)
        return tuple(new)

    zero = jnp.zeros((LANES, LANES), F32)
    lax.fori_loop(0, n_chunks, scan, (zero, zero))

    def attend(n, carry):
        rows = pl.ds(pl.multiple_of(n * C, C), C)
        for hh in range(2):
            att = _dot_nt(qh_sc[hh, rows, :], kh_sc[hh, n])
            att_sc[hh, rows, :] = jnp.where(tri, att, 0.0).astype(BF16)
        return carry

    lax.fori_loop(0, n_chunks, attend, 0, unroll=2 * unroll)

    def emit(n, carry):
        rows = pl.ds(pl.multiple_of(n * C, C), C)
        for hh in range(2):
            cols = head_cols(hh)
            v = v_ref[rows, cols]
            o = _dot(att_sc[hh, rows, :], jnp.concatenate([v, v], axis=0))
            state = jnp.where(fwd_sq, st_sc[hh, n], st_sc[hh, n_chunks - 1 - n]).astype(BF16)
            o = o + _dot_nt(qh_sc[hh, rows, :], state)
            ms = jnp.mean(o * o, axis=-1, keepdims=True)
            y = o * lax.rsqrt(ms + NORM_EPS) * gn_ref[...]
            out_ref[rows, cols] = (y * _silu(g_ref[rows, cols].astype(F32))).astype(BF16)
        return carry

    lax.fori_loop(0, n_chunks, emit, 0, unroll=unroll)


def _gla(proj, la, gn, layer, batch, seq, unroll=4):
    n_chunks = seq // GLA_CHUNK
    pair = 2 * LANES
    return pl.pallas_call(
        functools.partial(_gla_kernel, seq=seq, unroll=unroll),
        grid=(batch, GLA_HEADS // 2),
        in_specs=[
            pl.BlockSpec((seq, LANES), lambda b, p: (b, OFF_AQ // LANES + p)),
            pl.BlockSpec((seq, LANES), lambda b, p: (b, OFF_AK // LANES + p)),
            pl.BlockSpec((seq, pair), lambda b, p: (b, OFF_AV // pair + p)),
            pl.BlockSpec((seq, pair), lambda b, p: (b, OFF_AG // pair + p)),
            pl.BlockSpec((seq, pair), lambda b, p: (b, p)),
            pl.BlockSpec((None, 1, LANES), lambda b, p: (layer, 0, 0)),
        ],
        out_specs=pl.BlockSpec((seq, pair), lambda b, p: (b, p)),
        out_shape=jax.ShapeDtypeStruct((batch * seq, BRANCH_W), BF16),
        scratch_shapes=[
            pltpu.VMEM((2, n_chunks, LANES, LANES), F32),
            pltpu.VMEM((2, n_chunks, 8, LANES), F32),
            pltpu.VMEM((2, n_chunks, LANES, LANES), F32),
            pltpu.VMEM((2, seq, LANES), BF16),
            pltpu.VMEM((2, n_chunks, LANES, LANES), BF16),
            pltpu.VMEM((2, seq, LANES), BF16),
        ],
        compiler_params=pltpu.CompilerParams(
            dimension_semantics=("arbitrary", "arbitrary"), vmem_limit_bytes=VMEM_LIMIT),
        name="gla",
    )(proj, proj, proj, proj, la, gn)


def _diff_kernel(q_ref, k_ref, v_ref, g_ref, cos_ref, sin_ref, dl_ref, li_ref, gn_ref, out_ref,
                 k_sc, *, seq, tq):
    half0 = _lane_iota((tq, LANES)) < DIFF_D
    scale = DIFF_D ** -0.5

    def rope_k(i, carry):
        rows = pl.ds(pl.multiple_of(i * tq, tq), tq)
        k_sc[rows, :] = _rope(k_ref[rows, :].astype(F32), cos_ref[rows, :], sin_ref[rows, :]).astype(BF16)
        return carry

    lax.fori_loop(0, seq // tq, rope_k, 0)

    dl = dl_ref[...]
    lam_init = li_ref[...]
    lam = (jnp.exp(jnp.sum(dl[0:1] * dl[1:2], axis=-1, keepdims=True))
           - jnp.exp(jnp.sum(dl[2:3] * dl[3:4], axis=-1, keepdims=True)) + lam_init)

    def block(i, carry):
        rows = pl.ds(pl.multiple_of(i * tq, tq), tq)
        q = _rope(q_ref[rows, :].astype(F32), cos_ref[rows, :], sin_ref[rows, :]) * scale
        kk = k_sc[...]
        vv = v_ref[...]

        def softmax_pv(qm):
            s = _dot_nt(qm.astype(BF16), kk)
            m = jnp.max(s, axis=-1, keepdims=True)
            e = jnp.exp(s - m)
            l = jnp.sum(e, axis=-1, keepdims=True)
            return _dot(e.astype(BF16), vv) * (1.0 / l)

        o = softmax_pv(jnp.where(half0, q, 0.0)) - lam * softmax_pv(jnp.where(half0, 0.0, q))
        ms = jnp.mean(o * o, axis=-1, keepdims=True)
        y = o * lax.rsqrt(ms + NORM_EPS) * gn_ref[...] * (1.0 - lam_init)
        out_ref[rows, :] = (y * _silu(g_ref[rows, :].astype(F32))).astype(BF16)
        return carry

    lax.fori_loop(0, seq // tq, block, 0)


def _diff(proj, cos, sin, dl, lam_init, gn, layer, batch, seq, tq=256):
    blk = lambda off: pl.BlockSpec((seq, LANES), lambda b, h: (b, off // LANES + h))
    const = lambda shape: pl.BlockSpec(shape, lambda b, h: (0, 0))
    per_layer = lambda shape: pl.BlockSpec((None,) + shape, lambda b, h: (layer, 0, 0))
    return pl.pallas_call(
        functools.partial(_diff_kernel, seq=seq, tq=tq),
        grid=(batch, DIFF_HEADS),
        in_specs=[blk(OFF_BQ), blk(OFF_BK), blk(OFF_BV), blk(OFF_BG),
                  const((seq, LANES)), const((seq, LANES)),
                  per_layer((4, DIFF_D)), per_layer((1, 1)), per_layer((1, LANES))],
        out_specs=pl.BlockSpec((seq, LANES), lambda b, h: (b, h)),
        out_shape=jax.ShapeDtypeStruct((batch * seq, BRANCH_W), BF16),
        scratch_shapes=[pltpu.VMEM((seq, LANES), BF16)],
        compiler_params=pltpu.CompilerParams(
            dimension_semantics=("arbitrary", "arbitrary"), vmem_limit_bytes=VMEM_LIMIT),
        name="diff",
    )(proj, proj, proj, proj, cos, sin, dl, lam_init, gn)


def _dilated_kernel(q_ref, k_ref, v_ref, g_ref, cos_ref, sin_ref, out_ref,
                    nat_sc, kv1_sc, q4_sc, kv4_sc, q16_sc, kv16_sc, s_sc, e_sc, m_sc, o_sc, l_sc, bias_sc,
                    *, seq, unroll):
    scale = DIL_HD ** -0.5
    tq, win = DIL_TQ, 2 * DIL_TQ
    stage_rows = 2 * tq
    len4, len16 = seq // 4, seq // 16

    dcol = (lax.broadcasted_iota(jnp.int32, (tq, win), 1) - lax.broadcasted_iota(jnp.int32, (tq, win), 0))
    for idx in range(3):
        bias_sc[idx] = jnp.where(jnp.abs(dcol - idx * DIL_HALF) <= DIL_HALF, 0.0, NEG_INF)

    def stage(i, carry):
        rows = pl.ds(pl.multiple_of(i * stage_rows, stage_rows), stage_rows)
        cos = cos_ref[rows, :]
        sin = sin_ref[rows, :]
        nat_sc[0, rows, :] = _rope(q_ref[rows, :].astype(F32), cos, sin) * scale
        k = _rope(k_ref[rows, :].astype(F32), cos, sin)
        nat_sc[1, rows, :] = k
        kv1_sc[0, rows, :] = k.astype(BF16)
        nat_sc[2, rows, :] = v_ref[rows, :].astype(F32)
        kv1_sc[1, rows, :] = v_ref[rows, :]
        return carry

    lax.fori_loop(0, seq // stage_rows, stage, 0)

    for r in range(4):
        dst = slice(r * len4, (r + 1) * len4)
        q4_sc[dst, :] = nat_sc[0, pl.ds(r, len4, stride=4), :]
    for t in (1, 2):
        for r in range(4):
            dst = slice(r * len4, (r + 1) * len4)
            kv4_sc[t - 1, dst, :] = nat_sc[t, pl.ds(r, len4, stride=4), :].astype(BF16)
    for r4 in range(4):
        for j in range(4):
            dst = slice((r4 + 4 * j) * len16, (r4 + 4 * j + 1) * len16)
            q16_sc[dst, :] = q4_sc[pl.ds(r4 * len4 + j, len16, stride=4), :]
            for t in (1, 2):
                kv16_sc[t - 1, dst, :] = nat_sc[t, pl.ds(r4 + 4 * j, len16, stride=16), :].astype(BF16)

    head0 = _lane_iota((tq, LANES)) < DIL_HD
    n_blocks = seq // tq

    def run_pattern(g, q_src, kv_src, seq_len, w, out_rows):
        per_seq = seq_len // tq

        def where(t):
            r, i = t // per_seq, t % per_seq
            ks = jnp.clip(i * tq - tq // 2, 0, seq_len - w)
            which = jnp.where(i == 0, 0, jnp.where(i == per_seq - 1, 2, 1)) if per_seq > 1 else 0
            qrows = pl.ds(pl.multiple_of(r * seq_len + i * tq, tq), tq)
            krows = pl.ds(pl.multiple_of(r * seq_len + ks, tq // 2), w)
            return qrows, krows, which

        def scores(t, carry):
            qrows, krows, which = where(t)
            q = q_src[qrows, :]
            q2 = jnp.concatenate([jnp.where(head0, q, 0.0), jnp.where(head0, 0.0, q)], axis=0)
            bias = bias_sc[which, :, :w]
            s = _dot_nt(q2.astype(BF16), kv_src[0, krows, :])
            s_sc[t, :, :w] = s + jnp.concatenate([bias, bias], axis=0)
            return carry

        lax.fori_loop(0, n_blocks, scores, 0, unroll=unroll)

        def softmax(t, carry):
            s = s_sc[t, :, :w]
            m = jnp.max(s, axis=-1, keepdims=True)
            e_sc[t, :, :w] = jnp.exp(s - m).astype(BF16)
            m_sc[t] = jnp.where(head0, m[:tq], m[tq:])
            return carry

        lax.fori_loop(0, n_blocks, softmax, 0, unroll=unroll)

        ones = jnp.ones((w, LANES), BF16)

        def values(t, carry):
            _, krows, _ = where(t)
            e = e_sc[t, :, :w]
            o2 = _dot(e, kv_src[1, krows, :])
            l2 = _dot(e, ones)
            o2 = o2 * (1.0 / l2)
            rows = out_rows(t)
            o_sc[g, rows, :] = jnp.where(head0, o2[:tq], o2[tq:])
            l_sc[g, rows, :] = m_sc[t] + jnp.log(jnp.where(head0, l2[:tq], l2[tq:]))
            return carry

        lax.fori_loop(0, n_blocks, values, 0, unroll=unroll)

    blk4 = len4 // tq
    run_pattern(0, nat_sc.at[0], kv1_sc, seq, win,
                lambda t: pl.ds(pl.multiple_of(t * tq, tq), tq))
    run_pattern(1, q4_sc, kv4_sc, len4, win,
                lambda t: pl.ds(t // blk4 + 4 * (t % blk4) * tq, tq, stride=4))
    run_pattern(2, q16_sc, kv16_sc, len16, len16,
                lambda t: pl.ds(t, len16, stride=16))

    def combine(i, carry):
        rows = pl.ds(pl.multiple_of(i * stage_rows, stage_rows), stage_rows)
        l0, l1, l2 = l_sc[0, rows, :], l_sc[1, rows, :], l_sc[2, rows, :]
        m = jnp.maximum(jnp.maximum(l0, l1), l2)
        w0, w1, w2 = jnp.exp(l0 - m), jnp.exp(l1 - m), jnp.exp(l2 - m)
        o = (w0 * o_sc[0, rows, :] + w1 * o_sc[1, rows, :] + w2 * o_sc[2, rows, :]) * (1.0 / (w0 + w1 + w2))
        out_ref[rows, :] = (o * _silu(g_ref[rows, :].astype(F32))).astype(BF16)
        return carry

    lax.fori_loop(0, seq // stage_rows, combine, 0)


def _dilated(proj, cos, sin, batch, seq, unroll=4):
    assert seq // 16 == DIL_TQ and all(w // (2 * d) == DIL_HALF for w, d in DIL_PATTERNS)
    blk = lambda off: pl.BlockSpec((seq, LANES), lambda b, p: (b, off // LANES + p))
    const = lambda shape: pl.BlockSpec(shape, lambda b, p: (0, 0))
    return pl.pallas_call(
        functools.partial(_dilated_kernel, seq=seq, unroll=unroll),
        grid=(batch, DIL_HEADS // 2),
        in_specs=[blk(OFF_CQ), blk(OFF_CK), blk(OFF_CV), blk(OFF_CG),
                  const((seq, LANES)), const((seq, LANES))],
        out_specs=pl.BlockSpec((seq, LANES), lambda b, p: (b, p)),
        out_shape=jax.ShapeDtypeStruct((batch * seq, BRANCH_W), BF16),
        scratch_shapes=[
            pltpu.VMEM((3, seq, LANES), F32), pltpu.VMEM((2, seq, LANES), BF16),
            pltpu.VMEM((seq, LANES), F32), pltpu.VMEM((2, seq, LANES), BF16),
            pltpu.VMEM((seq, LANES), F32), pltpu.VMEM((2, seq, LANES), BF16),
            pltpu.VMEM((seq // DIL_TQ, 2 * DIL_TQ, 2 * DIL_TQ), F32),
            pltpu.VMEM((seq // DIL_TQ, 2 * DIL_TQ, 2 * DIL_TQ), BF16),
            pltpu.VMEM((seq // DIL_TQ, DIL_TQ, LANES), F32),
            pltpu.VMEM((3, seq, LANES), F32), pltpu.VMEM((3, seq, LANES), F32),
            pltpu.VMEM((3, DIL_TQ, 2 * DIL_TQ), F32),
        ],
        compiler_params=pltpu.CompilerParams(
            dimension_semantics=("arbitrary", "arbitrary"), vmem_limit_bytes=VMEM_LIMIT),
        name="dilated",
    )(proj, proj, proj, proj, cos, sin)


def _merge_kernel(x_ref, ya_ref, yb_ref, yc_ref, m_ref, wb_ref, wo_ref, out_ref):
    mixed = None
    for gi, y_ref in enumerate((ya_ref, yb_ref, yc_ref)):
        z = _dot(y_ref[...], wb_ref[gi])
        gate = _sigmoid(m_ref[:, gi * D_MODEL:(gi + 1) * D_MODEL].astype(F32))
        mixed = gate * z if mixed is None else mixed + gate * z
    out_ref[...] = x_ref[...] + _dot(mixed.astype(BF16), wo_ref[...])


def _merge(x2, ya, yb, yc, proj, wb, wo, layer, tm=512):
    rows = x2.shape[0]
    yspec = pl.BlockSpec((tm, BRANCH_W), lambda i: (i, 0))
    return pl.pallas_call(
        _merge_kernel,
        grid=(rows // tm,),
        in_specs=[
            pl.BlockSpec((tm, D_MODEL), lambda i: (i, 0)),
            yspec, yspec, yspec,
            pl.BlockSpec((tm, MERGE_W), lambda i: (i, 0)),
            pl.BlockSpec((None, N_BRANCH, BRANCH_W, D_MODEL), lambda i: (layer, 0, 0, 0)),
            pl.BlockSpec((None, D_MODEL, D_MODEL), lambda i: (layer, 0, 0)),
        ],
        out_specs=pl.BlockSpec((tm, D_MODEL), lambda i: (i, 0)),
        out_shape=jax.ShapeDtypeStruct((rows, D_MODEL), F32),
        compiler_params=pltpu.CompilerParams(
            dimension_semantics=("arbitrary",), vmem_limit_bytes=VMEM_LIMIT),
        name="merge",
    )(x2, ya, yb, yc, proj, wb, wo)


def _final_norm_kernel(x_ref, g_ref, out_ref):
    x = x_ref[...]
    ms = jnp.mean(x * x, axis=-1, keepdims=True)
    out_ref[...] = x * lax.rsqrt(ms + NORM_EPS) * g_ref[...]


def _final_norm(x2, g, tm=1024):
    rows = x2.shape[0]
    return pl.pallas_call(
        _final_norm_kernel,
        grid=(rows // tm,),
        in_specs=[pl.BlockSpec((tm, D_MODEL), lambda i: (i, 0)),
                  pl.BlockSpec((1, D_MODEL), lambda i: (0, 0))],
        out_specs=pl.BlockSpec((tm, D_MODEL), lambda i: (i, 0)),
        out_shape=jax.ShapeDtypeStruct((rows, D_MODEL), F32),
        compiler_params=pltpu.CompilerParams(dimension_semantics=("arbitrary",)),
        name="final_norm",
    )(x2, g)


def _rope_tables(seq, dim):
    inv = 1.0 / (ROPE_THETA ** (jnp.arange(0, dim, 2, dtype=F32) / dim))
    ang = jnp.arange(seq, dtype=F32)[:, None] * inv[None, :]
    cos, sin = jnp.cos(ang), jnp.sin(ang)
    reps = LANES // dim
    cos_t = jnp.tile(jnp.concatenate([cos, cos], axis=-1), (1, reps))
    sin_t = jnp.tile(jnp.concatenate([-sin, sin], axis=-1), (1, reps))
    return cos_t, sin_t


def kernel(x, norm_g, w_in, gla_gate_up, gla_gate_b, gla_norm_g, diff_lambda, diff_norm_g, w_branch, w_out, final_norm_g):
    batch, seq, _ = x.shape
    depth = w_in.shape[0]
    cos_b, sin_b = _rope_tables(seq, DIFF_D)
    cos_c, sin_c = _rope_tables(seq, DIL_HD)

    lo, hi = LOWRANK_OFF, LOWRANK_OFF + LOWRANK_W
    merge_off = IN_COLS - MERGE_W
    w_main = jnp.concatenate([w_in[:, :, merge_off:], w_in[:, :, :lo], w_in[:, :, hi:merge_off]],
                             axis=-1).astype(BF16)
    w_lr = w_in[:, :, lo:hi]
    w_low = jnp.concatenate([w_lr, w_lr, w_lr, jnp.zeros_like(w_lr)], axis=-1).astype(BF16)
    gu = gla_gate_up.reshape(depth, 2, GLA_RANK, GLA_HEADS, GLA_DK)
    zeros = jnp.zeros_like(gu[:, 0])
    gu_f = jnp.concatenate([gu[:, 0], zeros], axis=-1).reshape(depth, GLA_RANK, GLA_HEADS * LANES)
    gu_b = jnp.concatenate([zeros, gu[:, 1]], axis=-1).reshape(depth, GLA_RANK, GLA_HEADS * LANES)
    gu_cat = jnp.concatenate([gu_f, gu_b], axis=1)
    gu_hi = gu_cat.astype(BF16)
    gu_lo = (gu_cat - gu_hi.astype(F32)).astype(BF16)
    gate_up = jnp.concatenate([gu_hi, gu_lo, gu_hi, jnp.zeros_like(gu_hi)], axis=1)
    gb = gla_gate_b.reshape(depth, 2, GLA_HEADS, GLA_DK)
    gate_b = jnp.concatenate([gb[:, 0], gb[:, 1]], axis=-1).reshape(depth, 1, GLA_HEADS * LANES)
    wb = w_branch.astype(BF16)
    wo = w_out.astype(BF16)
    lam_init = jnp.asarray([0.8 - 0.6 * math.exp(-0.3 * layer) for layer in range(depth)],
                           F32).reshape(depth, 1, 1)
    norm_g3 = norm_g[:, None, :]
    gla_norm_g3 = gla_norm_g[:, None, :]
    diff_norm_g3 = diff_norm_g[:, None, :]

    x2 = x.reshape(batch * seq, D_MODEL)
    for layer in range(depth):
        proj, la = _in_proj(x2, norm_g3, w_main, w_low, gate_up, gate_b, layer)
        ya = _gla(proj, la, gla_norm_g3, layer, batch, seq)
        yb = _diff(proj, cos_b, sin_b, diff_lambda, lam_init, diff_norm_g3, layer, batch, seq)
        yc = _dilated(proj, cos_c, sin_c, batch, seq)
        x2 = _merge(x2, ya, yb, yc, proj, wb, wo, layer)
    return _final_norm(x2, final_norm_g[None]).reshape(batch, seq, D_MODEL)
```

```python
import functools
import math

import jax
import jax.numpy as jnp
from jax import lax
from jax.experimental import pallas as pl
from jax.experimental.pallas import tpu as pltpu

F32 = jnp.float32
BF16 = jnp.bfloat16

LANES = 128

D_MODEL = 1024
ROPE_THETA = 10000.0
NORM_EPS = 1e-6
BRANCH_W = D_MODEL // 2
N_BRANCH = 3

GLA_HEADS = 4
GLA_DV = BRANCH_W // GLA_HEADS
GLA_DK = GLA_DV // 2
GLA_RANK = 16
GLA_TAU = 16.0
GLA_CHUNK = 64
GLA_QK_W = GLA_HEADS * GLA_DK

DIFF_HEADS = 4
DIFF_D = BRANCH_W // (2 * DIFF_HEADS)
DIFF_QK_W = DIFF_HEADS * 2 * DIFF_D

DIL_HEADS = 8
DIL_HD = BRANCH_W // DIL_HEADS
DIL_W = DIL_HEADS * DIL_HD
DIL_PATTERNS = ((128, 1), (512, 4), (2048, 16))
DIL_HALF = 64
DIL_TQ = 128
NEG_INF = -1e30

MERGE_W = N_BRANCH * D_MODEL
LOWRANK_OFF = GLA_QK_W * 2 + BRANCH_W * 2
LOWRANK_W = 2 * GLA_RANK
IN_COLS = LOWRANK_OFF + LOWRANK_W + 2 * (DIFF_QK_W * 2 + BRANCH_W * 2) + MERGE_W
MAIN_W = IN_COLS - LOWRANK_W

OFF_AQ = MERGE_W
OFF_AK = OFF_AQ + GLA_QK_W
OFF_AV = OFF_AK + GLA_QK_W
OFF_AG = OFF_AV + BRANCH_W
OFF_BQ = OFF_AG + BRANCH_W
OFF_BK = OFF_BQ + DIFF_QK_W
OFF_BV = OFF_BK + DIFF_QK_W
OFF_BG = OFF_BV + BRANCH_W
OFF_CQ = OFF_BG + BRANCH_W
OFF_CK = OFF_CQ + DIL_W
OFF_CV = OFF_CK + DIL_W
OFF_CG = OFF_CV + BRANCH_W

VMEM_LIMIT = 56 * 1024 * 1024


def _dot(a, b):
    return jnp.dot(a, b, preferred_element_type=F32)


def _dot_nt(a, b):
    return lax.dot_general(a, b, (((1,), (1,)), ((), ())), preferred_element_type=F32)


def _sigmoid(x):
    return 1.0 / (1.0 + jnp.exp(-x))


def _silu(x):
    return x * _sigmoid(x)


def _lane_iota(shape):
    return lax.broadcasted_iota(jnp.int32, shape, len(shape) - 1)


def _split_bf16(x):
    hi = x.astype(BF16).astype(F32)
    return hi, x - hi


def _rope(x, cos, sin_signed):
    lane = _lane_iota(x.shape)
    first_half = (lane % 64) < 32
    partner = jnp.where(first_half, pltpu.roll(x, 96, 1), pltpu.roll(x, 32, 1))
    return x * cos + partner * sin_signed


_PLAIN, _ROPE_Q, _ROPE_K = 0, 1, 2
_PROJ_SEGMENTS = (
    (0, OFF_BQ // 2, _PLAIN), (OFF_BQ // 2, OFF_BQ // 2, _PLAIN),
    (OFF_BQ, DIFF_QK_W, _ROPE_Q), (OFF_BK, DIFF_QK_W, _ROPE_K),
    (OFF_BV, 2 * BRANCH_W, _PLAIN),
    (OFF_CQ, DIL_W, _ROPE_Q), (OFF_CK, DIL_W, _ROPE_K),
    (OFF_CV, 2 * BRANCH_W, _PLAIN),
)
QK_SCALE = DIFF_D ** -0.5


def _in_proj_kernel(x_ref, g_ref, w_ref, wl_ref, gu_ref, gb_ref, cos_ref, sin_ref, out_ref, la_ref):
    x = x_ref[...]
    ms = jnp.mean(x * x, axis=-1, keepdims=True)
    hb = (x * lax.rsqrt(ms + NORM_EPS) * g_ref[...]).astype(BF16)
    low = _dot(hb, wl_ref[...])
    low_hi, low_lo = _split_bf16(low)
    lhs = jnp.where(_lane_iota(low.shape) < 2 * LOWRANK_W, low_hi, low_lo).astype(BF16)
    z = _dot(lhs, gu_ref[...]) + gb_ref[...]
    log_sig = jnp.minimum(z, 0.0) - jnp.log1p(jnp.exp(-jnp.abs(z)))
    la_ref[...] = log_sig * (1.0 / GLA_TAU)
    for start, width, kind in _PROJ_SEGMENTS:
        res = _dot(hb, w_ref[:, start:start + width])
        if kind != _PLAIN:
            cos, sin = cos_ref[...], sin_ref[...]
            tiles = [_rope(res[:, c:c + LANES], cos, sin) for c in range(0, width, LANES)]
            res = jnp.concatenate(tiles, axis=1)
            if kind == _ROPE_Q:
                res = res * QK_SCALE
        out_ref[:, start:start + width] = res.astype(BF16)


def _in_proj(x2, g, w_main, w_low, gate_up, gate_b, cos, sin, layer, seq, tm=512):
    assert DIFF_D == DIL_HD and seq % tm == 0
    rows = x2.shape[0]
    resident = dict(pipeline_mode=pl.Buffered(1))
    pos_blocks = seq // tm
    return pl.pallas_call(
        _in_proj_kernel,
        grid=(rows // tm,),
        in_specs=[
            pl.BlockSpec((tm, D_MODEL), lambda i: (i, 0)),
            pl.BlockSpec((None, 1, D_MODEL), lambda i: (layer, 0, 0)),
            pl.BlockSpec((None, D_MODEL, MAIN_W), lambda i: (layer, 0, 0), **resident),
            pl.BlockSpec((None, D_MODEL, LANES), lambda i: (layer, 0, 0), **resident),
            pl.BlockSpec((None, LANES, 4 * LANES), lambda i: (layer, 0, 0), **resident),
            pl.BlockSpec((None, 1, 4 * LANES), lambda i: (layer, 0, 0)),
            pl.BlockSpec((tm, LANES), lambda i: (i % pos_blocks, 0)),
            pl.BlockSpec((tm, LANES), lambda i: (i % pos_blocks, 0)),
        ],
        out_specs=[
            pl.BlockSpec((tm, MAIN_W), lambda i: (i, 0)),
            pl.BlockSpec((tm, 4 * LANES), lambda i: (i, 0)),
        ],
        out_shape=[
            jax.ShapeDtypeStruct((rows, MAIN_W), BF16),
            jax.ShapeDtypeStruct((rows, 4 * LANES), F32),
        ],
        compiler_params=pltpu.CompilerParams(
            dimension_semantics=("arbitrary",), vmem_limit_bytes=VMEM_LIMIT),
        name="in_proj",
    )(x2, g, w_main, w_low, gate_up, gate_b, cos, sin)


def _gla_kernel(q_ref, k_ref, v_ref, g_ref, la_ref, gn_ref, out_ref,
                u_sc, d_sc, st_sc, qh_sc, kh_sc, att_sc, *, seq, unroll):
    C = GLA_CHUNK
    n_chunks = seq // C
    lane_c = _lane_iota((C, LANES))
    row_c = lax.broadcasted_iota(jnp.int32, (C, LANES), 0)
    fwd_c = lane_c < GLA_DK
    tri = jnp.where(fwd_c, row_c - lane_c, lane_c - GLA_DK - row_c) >= 0
    tri_bf = tri.astype(F32).astype(BF16)
    tri2 = jnp.concatenate([tri_bf, tri_bf], axis=1)
    fwd_sq = _lane_iota((LANES, LANES)) < GLA_DK
    scale = GLA_DK ** -0.5

    def head_cols(hh):
        return slice(hh * LANES, (hh + 1) * LANES)

    def dup_heads(ref, rows):
        xx = ref[rows, :].astype(F32)
        rolled = pltpu.roll(xx, GLA_DK, 1)
        return jnp.where(fwd_c, xx, rolled), jnp.where(fwd_c, rolled, xx)

    def prep(n, carry):
        rows = pl.ds(pl.multiple_of(n * C, C), C)
        q2 = dup_heads(q_ref, rows)
        k2 = dup_heads(k_ref, rows)
        for hh in range(2):
            cols = head_cols(hh)
            la = la_ref[rows, cols]
            la_bd = jnp.concatenate([jnp.where(fwd_c, la, 0.0), jnp.where(fwd_c, 0.0, la)], axis=0)
            la_hi, la_lo = _split_bf16(la_bd)
            b = _dot(tri2, jnp.concatenate([la_hi, la_lo], axis=0).astype(BF16))
            tot = jnp.sum(la, axis=0, keepdims=True)
            qh_sc[hh, rows, :] = (q2[hh] * jnp.exp(b) * scale).astype(BF16)
            k_hat = k2[hh] * jnp.exp(-b)
            kh_sc[hh, n] = jnp.concatenate(
                [jnp.where(fwd_c, k_hat, 0.0), jnp.where(fwd_c, 0.0, k_hat)], axis=0).astype(BF16)
            k_end = (k2[hh] * jnp.exp(tot - b)).astype(BF16)
            v_t = v_ref[rows, cols].astype(F32).T.astype(BF16)
            u_sc[hh, n] = _dot(v_t, k_end)
            d_sc[hh, n] = jnp.broadcast_to(jnp.exp(tot), (8, LANES))
        return carry

    lax.fori_loop(0, n_chunks, prep, 0, unroll=unroll)

    def scan(i, states):
        j = n_chunks - 1 - i
        new = []
        for hh in range(2):
            st_sc[hh, i] = states[hh]
            upd = jnp.where(fwd_sq, u_sc[hh, i], u_sc[hh, j])
            dec = jnp.where(fwd_sq[:8], d_sc[hh, i], d_sc[hh, j])[0:1]
            new.append(states[hh] * dec + upd)
        return tuple(new)

    zero = jnp.zeros((LANES, LANES), F32)
    lax.fori_loop(0, n_chunks, scan, (zero, zero))

    def attend(n, carry):
        rows = pl.ds(pl.multiple_of(n * C, C), C)
        for hh in range(2):
            att = _dot_nt(qh_sc[hh, rows, :], kh_sc[hh, n])
            att_sc[hh, rows, :] = jnp.where(tri, att, 0.0).astype(BF16)
        return carry

    lax.fori_loop(0, n_chunks, attend, 0, unroll=2 * unroll)

    def emit(n, carry):
        rows = pl.ds(pl.multiple_of(n * C, C), C)
        for hh in range(2):
            cols = head_cols(hh)
            v = v_ref[rows, cols]
            o = _dot(att_sc[hh, rows, :], jnp.concatenate([v, v], axis=0))
            state = jnp.where(fwd_sq, st_sc[hh, n], st_sc[hh, n_chunks - 1 - n]).astype(BF16)
            o = o + _dot_nt(qh_sc[hh, rows, :], state)
            ms = jnp.mean(o * o, axis=-1, keepdims=True)
            y = o * lax.rsqrt(ms + NORM_EPS) * gn_ref[...]
            out_ref[rows, cols] = (y * _silu(g_ref[rows, cols].astype(F32))).astype(BF16)
        return carry

    lax.fori_loop(0, n_chunks, emit, 0, unroll=unroll)


def _gla(proj, la, gn, layer, batch, seq, unroll=4):
    n_chunks = seq // GLA_CHUNK
    pair = 2 * LANES
    return pl.pallas_call(
        functools.partial(_gla_kernel, seq=seq, unroll=unroll),
        grid=(batch, GLA_HEADS // 2),
        in_specs=[
            pl.BlockSpec((seq, LANES), lambda b, p: (b, OFF_AQ // LANES + p)),
            pl.BlockSpec((seq, LANES), lambda b, p: (b, OFF_AK // LANES + p)),
            pl.BlockSpec((seq, pair), lambda b, p: (b, OFF_AV // pair + p)),
            pl.BlockSpec((seq, pair), lambda b, p: (b, OFF_AG // pair + p)),
            pl.BlockSpec((seq, pair), lambda b, p: (b, p)),
            pl.BlockSpec((None, 1, LANES), lambda b, p: (layer, 0, 0)),
        ],
        out_specs=pl.BlockSpec((seq, pair), lambda b, p: (b, p)),
        out_shape=jax.ShapeDtypeStruct((batch * seq, BRANCH_W), BF16),
        scratch_shapes=[
            pltpu.VMEM((2, n_chunks, LANES, LANES), F32),
            pltpu.VMEM((2, n_chunks, 8, LANES), F32),
            pltpu.VMEM((2, n_chunks, LANES, LANES), F32),
            pltpu.VMEM((2, seq, LANES), BF16),
            pltpu.VMEM((2, n_chunks, LANES, LANES), BF16),
            pltpu.VMEM((2, seq, LANES), BF16),
        ],
        compiler_params=pltpu.CompilerParams(
            dimension_semantics=("arbitrary", "arbitrary"), vmem_limit_bytes=VMEM_LIMIT),
        name="gla",
    )(proj, proj, proj, proj, la, gn)


def _diff_kernel(q_ref, k_ref, v_ref, g_ref, dl_ref, li_ref, gn_ref, out_ref,
                 q_sc, s0_sc, s1_sc, *, seq, tq):
    n_blk = seq // tq
    half0 = _lane_iota((tq, LANES)) < DIFF_D

    def stack_q(i, carry):
        q = q_ref[pl.ds(pl.multiple_of(i * tq, tq), tq), :].astype(F32)
        q_sc[i] = jnp.concatenate([jnp.where(half0, q, 0.0), jnp.where(half0, 0.0, q)], axis=0).astype(BF16)
        return carry

    lax.fori_loop(0, n_blk, stack_q, 0)

    dl = dl_ref[...]
    lam_init = li_ref[...]
    lam = (jnp.exp(jnp.sum(dl[0:1] * dl[1:2], axis=-1, keepdims=True))
           - jnp.exp(jnp.sum(dl[2:3] * dl[3:4], axis=-1, keepdims=True)) + lam_init)

    def scores(i, s_sc):
        s_sc[...] = _dot_nt(q_sc[jnp.minimum(i, n_blk - 1)], k_ref[...])

    def finish(i, s_sc):
        rows = pl.ds(pl.multiple_of(i * tq, tq), tq)
        s = s_sc[...]
        m = jnp.max(s, axis=-1, keepdims=True)
        e = jnp.exp(s - m)
        l = jnp.sum(e, axis=-1, keepdims=True)
        o2 = _dot(e.astype(BF16), v_ref[...]) * (1.0 / l)
        o = o2[:tq] - lam * o2[tq:]
        ms = jnp.mean(o * o, axis=-1, keepdims=True)
        y = o * lax.rsqrt(ms + NORM_EPS) * gn_ref[...] * (1.0 - lam_init)
        out_ref[rows, :] = (y * _silu(g_ref[rows, :].astype(F32))).astype(BF16)

    scores(0, s0_sc)

    def pair(jj, carry):
        j = 2 * jj
        scores(j + 1, s1_sc)
        finish(j, s0_sc)
        scores(j + 2, s0_sc)
        finish(j + 1, s1_sc)
        return carry

    lax.fori_loop(0, n_blk // 2, pair, 0)


def _diff(proj, dl, lam_init, gn, layer, batch, seq, tq=256):
    blk = lambda off: pl.BlockSpec((seq, LANES), lambda b, h: (b, off // LANES + h))
    per_layer = lambda shape: pl.BlockSpec((None,) + shape, lambda b, h: (layer, 0, 0))
    return pl.pallas_call(
        functools.partial(_diff_kernel, seq=seq, tq=tq),
        grid=(batch, DIFF_HEADS),
        in_specs=[blk(OFF_BQ), blk(OFF_BK), blk(OFF_BV), blk(OFF_BG),
                  per_layer((4, DIFF_D)), per_layer((1, 1)), per_layer((1, LANES))],
        out_specs=pl.BlockSpec((seq, LANES), lambda b, h: (b, h)),
        out_shape=jax.ShapeDtypeStruct((batch * seq, BRANCH_W), BF16),
        scratch_shapes=[pltpu.VMEM((seq // tq, 2 * tq, LANES), BF16),
                        pltpu.VMEM((2 * tq, seq), F32), pltpu.VMEM((2 * tq, seq), F32)],
        compiler_params=pltpu.CompilerParams(
            dimension_semantics=("arbitrary", "arbitrary"), vmem_limit_bytes=VMEM_LIMIT),
        name="diff",
    )(proj, proj, proj, proj, dl, lam_init, gn)


def _dilated_kernel(q_ref, k_ref, v_ref, g_ref, out_ref,
                    nat_sc, q4_sc, kv4_sc, q16_sc, kv16_sc, s_sc, e_sc, m_sc, o_sc, l_sc, bias_sc,
                    *, seq, unroll):
    tq, win = DIL_TQ, 2 * DIL_TQ
    stage_rows = 2 * tq
    len4, len16 = seq // 4, seq // 16

    dcol = (lax.broadcasted_iota(jnp.int32, (tq, win), 1) - lax.broadcasted_iota(jnp.int32, (tq, win), 0))
    for idx in range(3):
        bias_sc[idx] = jnp.where(jnp.abs(dcol - idx * DIL_HALF) <= DIL_HALF, 0.0, NEG_INF)

    def stage(i, carry):
        rows = pl.ds(pl.multiple_of(i * stage_rows, stage_rows), stage_rows)
        nat_sc[0, rows, :] = q_ref[rows, :].astype(F32)
        nat_sc[1, rows, :] = k_ref[rows, :].astype(F32)
        nat_sc[2, rows, :] = v_ref[rows, :].astype(F32)
        return carry

    lax.fori_loop(0, seq // stage_rows, stage, 0)

    for r in range(4):
        dst = slice(r * len4, (r + 1) * len4)
        q4_sc[dst, :] = nat_sc[0, pl.ds(r, len4, stride=4), :]
    for t in (1, 2):
        for r in range(4):
            dst = slice(r * len4, (r + 1) * len4)
            kv4_sc[t - 1, dst, :] = nat_sc[t, pl.ds(r, len4, stride=4), :].astype(BF16)
    for r4 in range(4):
        for j in range(4):
            dst = slice((r4 + 4 * j) * len16, (r4 + 4 * j + 1) * len16)
            q16_sc[dst, :] = q4_sc[pl.ds(r4 * len4 + j, len16, stride=4), :]
            for t in (1, 2):
                kv16_sc[t - 1, dst, :] = nat_sc[t, pl.ds(r4 + 4 * j, len16, stride=16), :].astype(BF16)

    head0 = _lane_iota((tq, LANES)) < DIL_HD
    n_blocks = seq // tq

    def run_pattern(g, q_src, k_src, v_src, seq_len, w, out_rows):
        per_seq = seq_len // tq

        def where(t):
            r, i = t // per_seq, t % per_seq
            ks = jnp.clip(i * tq - tq // 2, 0, seq_len - w)
            which = jnp.where(i == 0, 0, jnp.where(i == per_seq - 1, 2, 1)) if per_seq > 1 else 0
            qrows = pl.ds(pl.multiple_of(r * seq_len + i * tq, tq), tq)
            krows = pl.ds(pl.multiple_of(r * seq_len + ks, tq // 2), w)
            return qrows, krows, which

        def scores(t, carry):
            qrows, krows, which = where(t)
            q = q_src[qrows, :]
            q2 = jnp.concatenate([jnp.where(head0, q, 0.0), jnp.where(head0, 0.0, q)], axis=0)
            bias = bias_sc[which, :, :w]
            s = _dot_nt(q2.astype(BF16), k_src[krows, :])
            s_sc[t, :, :w] = s + jnp.concatenate([bias, bias], axis=0)
            return carry

        lax.fori_loop(0, n_blocks, scores, 0, unroll=unroll)

        def softmax(t, carry):
            s = s_sc[t, :, :w]
            m = jnp.max(s, axis=-1, keepdims=True)
            e_sc[t, :, :w] = jnp.exp(s - m).astype(BF16)
            m_sc[t] = jnp.where(head0, m[:tq], m[tq:])
            return carry

        lax.fori_loop(0, n_blocks, softmax, 0, unroll=unroll)

        ones = jnp.ones((w, LANES), BF16)

        def values(t, carry):
            _, krows, _ = where(t)
            e = e_sc[t, :, :w]
            o2 = _dot(e, v_src[krows, :])
            l2 = _dot(e, ones)
            o2 = o2 * (1.0 / l2)
            rows = out_rows(t)
            o_sc[g, rows, :] = jnp.where(head0, o2[:tq], o2[tq:])
            l_sc[g, rows, :] = m_sc[t] + jnp.log(jnp.where(head0, l2[:tq], l2[tq:]))
            return carry

        lax.fori_loop(0, n_blocks, values, 0, unroll=unroll)

    blk4 = len4 // tq
    run_pattern(0, nat_sc.at[0], k_ref, v_ref, seq, win,
                lambda t: pl.ds(pl.multiple_of(t * tq, tq), tq))
    run_pattern(1, q4_sc, kv4_sc.at[0], kv4_sc.at[1], len4, win,
                lambda t: pl.ds(t // blk4 + 4 * (t % blk4) * tq, tq, stride=4))
    run_pattern(2, q16_sc, kv16_sc.at[0], kv16_sc.at[1], len16, len16,
                lambda t: pl.ds(t, len16, stride=16))

    def combine(i, carry):
        rows = pl.ds(pl.multiple_of(i * stage_rows, stage_rows), stage_rows)
        l0, l1, l2 = l_sc[0, rows, :], l_sc[1, rows, :], l_sc[2, rows, :]
        m = jnp.maximum(jnp.maximum(l0, l1), l2)
        w0, w1, w2 = jnp.exp(l0 - m), jnp.exp(l1 - m), jnp.exp(l2 - m)
        o = (w0 * o_sc[0, rows, :] + w1 * o_sc[1, rows, :] + w2 * o_sc[2, rows, :]) * (1.0 / (w0 + w1 + w2))
        out_ref[rows, :] = (o * _silu(g_ref[rows, :].astype(F32))).astype(BF16)
        return carry

    lax.fori_loop(0, seq // stage_rows, combine, 0)


def _dilated(proj, batch, seq, unroll=4):
    assert seq // 16 == DIL_TQ and all(w // (2 * d) == DIL_HALF for w, d in DIL_PATTERNS)
    blk = lambda off: pl.BlockSpec((seq, LANES), lambda b, p: (b, off // LANES + p))
    return pl.pallas_call(
        functools.partial(_dilated_kernel, seq=seq, unroll=unroll),
        grid=(batch, DIL_HEADS // 2),
        in_specs=[blk(OFF_CQ), blk(OFF_CK), blk(OFF_CV), blk(OFF_CG)],
        out_specs=pl.BlockSpec((seq, LANES), lambda b, p: (b, p)),
        out_shape=jax.ShapeDtypeStruct((batch * seq, BRANCH_W), BF16),
        scratch_shapes=[
            pltpu.VMEM((3, seq, LANES), F32),
            pltpu.VMEM((seq, LANES), F32), pltpu.VMEM((2, seq, LANES), BF16),
            pltpu.VMEM((seq, LANES), F32), pltpu.VMEM((2, seq, LANES), BF16),
            pltpu.VMEM((seq // DIL_TQ, 2 * DIL_TQ, 2 * DIL_TQ), F32),
            pltpu.VMEM((seq // DIL_TQ, 2 * DIL_TQ, 2 * DIL_TQ), BF16),
            pltpu.VMEM((seq // DIL_TQ, DIL_TQ, LANES), F32),
            pltpu.VMEM((3, seq, LANES), F32), pltpu.VMEM((3, seq, LANES), F32),
            pltpu.VMEM((3, DIL_TQ, 2 * DIL_TQ), F32),
        ],
        compiler_params=pltpu.CompilerParams(
            dimension_semantics=("arbitrary", "arbitrary"), vmem_limit_bytes=VMEM_LIMIT),
        name="dilated",
    )(proj, proj, proj, proj)


def _merge_kernel(x_ref, ya_ref, yb_ref, yc_ref, m_ref, wb_ref, wo_ref, fg_ref, out_ref, *, final):
    mixed = None
    for gi, y_ref in enumerate((ya_ref, yb_ref, yc_ref)):
        z = _dot(y_ref[...], wb_ref[gi])
        gate = _sigmoid(m_ref[:, gi * D_MODEL:(gi + 1) * D_MODEL].astype(F32))
        mixed = gate * z if mixed is None else mixed + gate * z
    x = x_ref[...] + _dot(mixed.astype(BF16), wo_ref[...])
    if final:
        ms = jnp.mean(x * x, axis=-1, keepdims=True)
        x = x * lax.rsqrt(ms + NORM_EPS) * fg_ref[...]
    out_ref[...] = x


def _merge(x2, ya, yb, yc, proj, wb, wo, final_g, layer, final, tm=512):
    rows = x2.shape[0]
    yspec = pl.BlockSpec((tm, BRANCH_W), lambda i: (i, 0))
    return pl.pallas_call(
        functools.partial(_merge_kernel, final=final),
        grid=(rows // tm,),
        in_specs=[
            pl.BlockSpec((tm, D_MODEL), lambda i: (i, 0)),
            yspec, yspec, yspec,
            pl.BlockSpec((tm, MERGE_W), lambda i: (i, 0)),
            pl.BlockSpec((None, N_BRANCH, BRANCH_W, D_MODEL), lambda i: (layer, 0, 0, 0)),
            pl.BlockSpec((None, D_MODEL, D_MODEL), lambda i: (layer, 0, 0)),
            pl.BlockSpec((1, D_MODEL), lambda i: (0, 0)),
        ],
        out_specs=pl.BlockSpec((tm, D_MODEL), lambda i: (i, 0)),
        out_shape=jax.ShapeDtypeStruct((rows, D_MODEL), F32),
        compiler_params=pltpu.CompilerParams(
            dimension_semantics=("arbitrary",), vmem_limit_bytes=VMEM_LIMIT),
        name="merge",
    )(x2, ya, yb, yc, proj, wb, wo, final_g)


def _rope_tables(seq, dim):
    inv = 1.0 / (ROPE_THETA ** (jnp.arange(0, dim, 2, dtype=F32) / dim))
    ang = jnp.arange(seq, dtype=F32)[:, None] * inv[None, :]
    cos, sin = jnp.cos(ang), jnp.sin(ang)
    reps = LANES // dim
    cos_t = jnp.tile(jnp.concatenate([cos, cos], axis=-1), (1, reps))
    sin_t = jnp.tile(jnp.concatenate([-sin, sin], axis=-1), (1, reps))
    return cos_t, sin_t


def kernel(x, norm_g, w_in, gla_gate_up, gla_gate_b, gla_norm_g, diff_lambda, diff_norm_g, w_branch, w_out, final_norm_g):
    batch, seq, _ = x.shape
    depth = w_in.shape[0]
    cos, sin = _rope_tables(seq, DIFF_D)

    lo, hi = LOWRANK_OFF, LOWRANK_OFF + LOWRANK_W
    merge_off = IN_COLS - MERGE_W
    w_bf = w_in.astype(BF16)
    w_main = jnp.concatenate([w_bf[:, :, merge_off:], w_bf[:, :, :lo], w_bf[:, :, hi:merge_off]], axis=-1)
    w_lr = w_bf[:, :, lo:hi]
    w_low = jnp.concatenate([w_lr, w_lr, w_lr, jnp.zeros_like(w_lr)], axis=-1)
    gu = gla_gate_up.reshape(depth, 2, GLA_RANK, GLA_HEADS, GLA_DK)
    zeros = jnp.zeros_like(gu[:, 0])
    gu_f = jnp.concatenate([gu[:, 0], zeros], axis=-1).reshape(depth, GLA_RANK, GLA_HEADS * LANES)
    gu_b = jnp.concatenate([zeros, gu[:, 1]], axis=-1).reshape(depth, GLA_RANK, GLA_HEADS * LANES)
    gu_cat = jnp.concatenate([gu_f, gu_b], axis=1)
    gu_hi = gu_cat.astype(BF16)
    gu_lo = (gu_cat - gu_hi.astype(F32)).astype(BF16)
    gate_up = jnp.concatenate([gu_hi, gu_lo, gu_hi, jnp.zeros_like(gu_hi)], axis=1)
    gb = gla_gate_b.reshape(depth, 2, GLA_HEADS, GLA_DK)
    gate_b = jnp.concatenate([gb[:, 0], gb[:, 1]], axis=-1).reshape(depth, 1, GLA_HEADS * LANES)
    wb = w_branch.astype(BF16)
    wo = w_out.astype(BF16)
    lam_init = jnp.asarray([0.8 - 0.6 * math.exp(-0.3 * layer) for layer in range(depth)],
                           F32).reshape(depth, 1, 1)
    norm_g3 = norm_g[:, None, :]
    gla_norm_g3 = gla_norm_g[:, None, :]
    diff_norm_g3 = diff_norm_g[:, None, :]

    x2 = x.reshape(batch * seq, D_MODEL)
    for layer in range(depth):
        proj, la = _in_proj(x2, norm_g3, w_main, w_low, gate_up, gate_b, cos, sin, layer, seq)
        ya = _gla(proj, la, gla_norm_g3, layer, batch, seq)
        yb = _diff(proj, diff_lambda, lam_init, diff_norm_g3, layer, batch, seq)
        yc = _dilated(proj, batch, seq)
        x2 = _merge(x2, ya, yb, yc, proj, wb, wo, final_norm_g[None], layer, final=layer == depth - 1)
    return x2.reshape(batch, seq, D_MODEL)
```

```python
import functools
import math

import jax
import jax.numpy as jnp
from jax import lax
from jax.experimental import pallas as pl
from jax.experimental.pallas import tpu as pltpu

F32 = jnp.float32
BF16 = jnp.bfloat16

LANES = 128

D_MODEL = 1024
ROPE_THETA = 10000.0
NORM_EPS = 1e-6
BRANCH_W = D_MODEL // 2
N_BRANCH = 3

GLA_HEADS = 4
GLA_DV = BRANCH_W // GLA_HEADS
GLA_DK = GLA_DV // 2
GLA_RANK = 16
GLA_TAU = 16.0
GLA_CHUNK = 64
GLA_QK_W = GLA_HEADS * GLA_DK

DIFF_HEADS = 4
DIFF_D = BRANCH_W // (2 * DIFF_HEADS)
DIFF_QK_W = DIFF_HEADS * 2 * DIFF_D

DIL_HEADS = 8
DIL_HD = BRANCH_W // DIL_HEADS
DIL_W = DIL_HEADS * DIL_HD
DIL_PATTERNS = ((128, 1), (512, 4), (2048, 16))
DIL_HALF = 64
DIL_TQ = 128
NEG_INF = -1e30

MERGE_W = N_BRANCH * D_MODEL
LOWRANK_OFF = GLA_QK_W * 2 + BRANCH_W * 2
LOWRANK_W = 2 * GLA_RANK
IN_COLS = LOWRANK_OFF + LOWRANK_W + 2 * (DIFF_QK_W * 2 + BRANCH_W * 2) + MERGE_W
MAIN_W = IN_COLS - LOWRANK_W

OFF_AQ = MERGE_W
OFF_AK = OFF_AQ + GLA_QK_W
OFF_AV = OFF_AK + GLA_QK_W
OFF_AG = OFF_AV + BRANCH_W
OFF_BQ = OFF_AG + BRANCH_W
OFF_BK = OFF_BQ + DIFF_QK_W
OFF_BV = OFF_BK + DIFF_QK_W
OFF_BG = OFF_BV + BRANCH_W
OFF_CQ = OFF_BG + BRANCH_W
OFF_CK = OFF_CQ + DIL_W
OFF_CV = OFF_CK + DIL_W
OFF_CG = OFF_CV + BRANCH_W

VMEM_LIMIT = 56 * 1024 * 1024


def _dot(a, b):
    return jnp.dot(a, b, preferred_element_type=F32)


def _dot_nt(a, b):
    return lax.dot_general(a, b, (((1,), (1,)), ((), ())), preferred_element_type=F32)


def _sigmoid(x):
    return 1.0 / (1.0 + jnp.exp(-x))


def _silu(x):
    return x * _sigmoid(x)


def _lane_iota(shape):
    return lax.broadcasted_iota(jnp.int32, shape, len(shape) - 1)


def _split_bf16(x):
    hi = x.astype(BF16).astype(F32)
    return hi, x - hi


def _rope(x, cos, sin_signed):
    lane = _lane_iota(x.shape)
    first_half = (lane % 64) < 32
    partner = jnp.where(first_half, pltpu.roll(x, 96, 1), pltpu.roll(x, 32, 1))
    return x * cos + partner * sin_signed


_PLAIN, _ROPE_Q, _ROPE_K = 0, 1, 2
_PROJ_SEGMENTS = (
    (0, OFF_BQ // 2, _PLAIN), (OFF_BQ // 2, OFF_BQ // 2, _PLAIN),
    (OFF_BQ, DIFF_QK_W, _ROPE_Q), (OFF_BK, DIFF_QK_W, _ROPE_K),
    (OFF_BV, 2 * BRANCH_W, _PLAIN),
    (OFF_CQ, DIL_W, _ROPE_Q), (OFF_CK, DIL_W, _ROPE_K),
    (OFF_CV, 2 * BRANCH_W, _PLAIN),
)
QK_SCALE = DIFF_D ** -0.5


def _in_proj_kernel(x_ref, g_ref, w_ref, wl_ref, gu_ref, gb_ref, cos_ref, sin_ref, out_ref, la_ref):
    x = x_ref[...]
    ms = jnp.mean(x * x, axis=-1, keepdims=True)
    hb = (x * lax.rsqrt(ms + NORM_EPS) * g_ref[...]).astype(BF16)
    low = _dot(hb, wl_ref[...])
    low_hi, low_lo = _split_bf16(low)
    lhs = jnp.where(_lane_iota(low.shape) < 2 * LOWRANK_W, low_hi, low_lo).astype(BF16)
    z = _dot(lhs, gu_ref[...]) + gb_ref[...]
    log_sig = jnp.minimum(z, 0.0) - jnp.log1p(jnp.exp(-jnp.abs(z)))
    la_ref[...] = log_sig * (1.0 / GLA_TAU)
    for start, width, kind in _PROJ_SEGMENTS:
        res = _dot(hb, w_ref[:, start:start + width])
        if kind != _PLAIN:
            cos, sin = cos_ref[...], sin_ref[...]
            tiles = [_rope(res[:, c:c + LANES], cos, sin) for c in range(0, width, LANES)]
            res = jnp.concatenate(tiles, axis=1)
            if kind == _ROPE_Q:
                res = res * QK_SCALE
        out_ref[:, start:start + width] = res.astype(BF16)


def _in_proj(x2, g, w_main, w_low, gate_up, gate_b, cos, sin, layer, seq, tm=512):
    assert DIFF_D == DIL_HD and seq % tm == 0
    rows = x2.shape[0]
    resident = dict(pipeline_mode=pl.Buffered(1))
    pos_blocks = seq // tm
    return pl.pallas_call(
        _in_proj_kernel,
        grid=(rows // tm,),
        in_specs=[
            pl.BlockSpec((tm, D_MODEL), lambda i: (i, 0)),
            pl.BlockSpec((None, 1, D_MODEL), lambda i: (layer, 0, 0)),
            pl.BlockSpec((None, D_MODEL, MAIN_W), lambda i: (layer, 0, 0), **resident),
            pl.BlockSpec((None, D_MODEL, LANES), lambda i: (layer, 0, 0), **resident),
            pl.BlockSpec((None, LANES, 4 * LANES), lambda i: (layer, 0, 0), **resident),
            pl.BlockSpec((None, 1, 4 * LANES), lambda i: (layer, 0, 0)),
            pl.BlockSpec((tm, LANES), lambda i: (i % pos_blocks, 0)),
            pl.BlockSpec((tm, LANES), lambda i: (i % pos_blocks, 0)),
        ],
        out_specs=[
            pl.BlockSpec((tm, MAIN_W), lambda i: (i, 0)),
            pl.BlockSpec((tm, 4 * LANES), lambda i: (i, 0)),
        ],
        out_shape=[
            jax.ShapeDtypeStruct((rows, MAIN_W), BF16),
            jax.ShapeDtypeStruct((rows, 4 * LANES), F32),
        ],
        compiler_params=pltpu.CompilerParams(
            dimension_semantics=("arbitrary",), vmem_limit_bytes=VMEM_LIMIT),
        name="in_proj",
    )(x2, g, w_main, w_low, gate_up, gate_b, cos, sin)


def _gla_kernel(q_ref, k_ref, v_ref, g_ref, la_ref, gn_ref, out_ref,
                u_sc, d_sc, st_sc, qh_sc, kh_sc, att_sc, *, seq, unroll):
    C = GLA_CHUNK
    n_chunks = seq // C
    lane_c = _lane_iota((C, LANES))
    row_c = lax.broadcasted_iota(jnp.int32, (C, LANES), 0)
    fwd_c = lane_c < GLA_DK
    tri = jnp.where(fwd_c, row_c - lane_c, lane_c - GLA_DK - row_c) >= 0
    tri_bf = tri.astype(F32).astype(BF16)
    tri2 = jnp.concatenate([tri_bf, tri_bf], axis=1)
    fwd_sq = _lane_iota((LANES, LANES)) < GLA_DK
    scale = GLA_DK ** -0.5

    def head_cols(hh):
        return slice(hh * LANES, (hh + 1) * LANES)

    def dup_heads(ref, rows):
        xx = ref[rows, :].astype(F32)
        rolled = pltpu.roll(xx, GLA_DK, 1)
        return jnp.where(fwd_c, xx, rolled), jnp.where(fwd_c, rolled, xx)

    def prep(n, carry):
        rows = pl.ds(pl.multiple_of(n * C, C), C)
        q2 = dup_heads(q_ref, rows)
        k2 = dup_heads(k_ref, rows)
        for hh in range(2):
            cols = head_cols(hh)
            la = la_ref[rows, cols]
            la_bd = jnp.concatenate([jnp.where(fwd_c, la, 0.0), jnp.where(fwd_c, 0.0, la)], axis=0)
            la_hi, la_lo = _split_bf16(la_bd)
            b = _dot(tri2, jnp.concatenate([la_hi, la_lo], axis=0).astype(BF16))
            tot = jnp.sum(la, axis=0, keepdims=True)
            qh_sc[hh, rows, :] = (q2[hh] * jnp.exp(b) * scale).astype(BF16)
            k_hat = k2[hh] * jnp.exp(-b)
            kh_sc[hh, n] = jnp.concatenate(
                [jnp.where(fwd_c, k_hat, 0.0), jnp.where(fwd_c, 0.0, k_hat)], axis=0).astype(BF16)
            k_end = (k2[hh] * jnp.exp(tot - b)).astype(BF16)
            v_t = v_ref[rows, cols].astype(F32).T.astype(BF16)
            u_sc[hh, n] = _dot(v_t, k_end)
            d_sc[hh, n] = jnp.broadcast_to(jnp.exp(tot), (8, LANES))
        return carry

    lax.fori_loop(0, n_chunks, prep, 0, unroll=unroll)

    def scan(i, states):
        j = n_chunks - 1 - i
        new = []
        for hh in range(2):
            st_sc[hh, i] = states[hh]
            upd = jnp.where(fwd_sq, u_sc[hh, i], u_sc[hh, j])
            dec = jnp.where(fwd_sq[:8], d_sc[hh, i], d_sc[hh, j])[0:1]
            new.append(states[hh] * dec + upd)
        return tuple(new)

    zero = jnp.zeros((LANES, LANES), F32)
    lax.fori_loop(0, n_chunks, scan, (zero, zero))

    def attend(n, carry):
        rows = pl.ds(pl.multiple_of(n * C, C), C)
        for hh in range(2):
            att = _dot_nt(qh_sc[hh, rows, :], kh_sc[hh, n])
            att_sc[hh, rows, :] = jnp.where(tri, att, 0.0).astype(BF16)
        return carry

    lax.fori_loop(0, n_chunks, attend, 0, unroll=2 * unroll)

    def emit(n, carry):
        rows = pl.ds(pl.multiple_of(n * C, C), C)
        for hh in range(2):
            cols = head_cols(hh)
            v = v_ref[rows, cols]
            o = _dot(att_sc[hh, rows, :], jnp.concatenate([v, v], axis=0))
            state = jnp.where(fwd_sq, st_sc[hh, n], st_sc[hh, n_chunks - 1 - n]).astype(BF16)
            o = o + _dot_nt(qh_sc[hh, rows, :], state)
            ms = jnp.mean(o * o, axis=-1, keepdims=True)
            y = o * lax.rsqrt(ms + NORM_EPS) * gn_ref[...]
            out_ref[rows, cols] = (y * _silu(g_ref[rows, cols].astype(F32))).astype(BF16)
        return carry

    lax.fori_loop(0, n_chunks, emit, 0, unroll=unroll)


def _gla(proj, la, gn, layer, batch, seq, unroll=4):
    n_chunks = seq // GLA_CHUNK
    pair = 2 * LANES
    return pl.pallas_call(
        functools.partial(_gla_kernel, seq=seq, unroll=unroll),
        grid=(batch, GLA_HEADS // 2),
        in_specs=[
            pl.BlockSpec((seq, LANES), lambda b, p: (b, OFF_AQ // LANES + p)),
            pl.BlockSpec((seq, LANES), lambda b, p: (b, OFF_AK // LANES + p)),
            pl.BlockSpec((seq, pair), lambda b, p: (b, OFF_AV // pair + p)),
            pl.BlockSpec((seq, pair), lambda b, p: (b, OFF_AG // pair + p)),
            pl.BlockSpec((seq, pair), lambda b, p: (b, p)),
            pl.BlockSpec((None, 1, LANES), lambda b, p: (layer, 0, 0)),
        ],
        out_specs=pl.BlockSpec((seq, pair), lambda b, p: (b, p)),
        out_shape=jax.ShapeDtypeStruct((batch * seq, BRANCH_W), BF16),
        scratch_shapes=[
            pltpu.VMEM((2, n_chunks, LANES, LANES), F32),
            pltpu.VMEM((2, n_chunks, 8, LANES), F32),
            pltpu.VMEM((2, n_chunks, LANES, LANES), F32),
            pltpu.VMEM((2, seq, LANES), BF16),
            pltpu.VMEM((2, n_chunks, LANES, LANES), BF16),
            pltpu.VMEM((2, seq, LANES), BF16),
        ],
        compiler_params=pltpu.CompilerParams(
            dimension_semantics=("arbitrary", "arbitrary"), vmem_limit_bytes=VMEM_LIMIT),
        name="gla",
    )(proj, proj, proj, proj, la, gn)


def _diff_kernel(q_ref, k_ref, v_ref, g_ref, dl_ref, li_ref, gn_ref, out_ref,
                 q_sc, s0_sc, s1_sc, *, seq, tq):
    n_blk = seq // tq
    half0 = _lane_iota((tq, LANES)) < DIFF_D

    def stack_q(i, carry):
        q = q_ref[pl.ds(pl.multiple_of(i * tq, tq), tq), :].astype(F32)
        q_sc[i] = jnp.concatenate([jnp.where(half0, q, 0.0), jnp.where(half0, 0.0, q)], axis=0).astype(BF16)
        return carry

    lax.fori_loop(0, n_blk, stack_q, 0)

    dl = dl_ref[...]
    lam_init = li_ref[...]
    lam = (jnp.exp(jnp.sum(dl[0:1] * dl[1:2], axis=-1, keepdims=True))
           - jnp.exp(jnp.sum(dl[2:3] * dl[3:4], axis=-1, keepdims=True)) + lam_init)

    def scores(i, s_sc):
        s_sc[...] = _dot_nt(q_sc[jnp.minimum(i, n_blk - 1)], k_ref[...])

    def finish(i, s_sc):
        rows = pl.ds(pl.multiple_of(i * tq, tq), tq)
        s = s_sc[...]
        m = jnp.max(s, axis=-1, keepdims=True)
        e = jnp.exp(s - m)
        l = jnp.sum(e, axis=-1, keepdims=True)
        o2 = _dot(e.astype(BF16), v_ref[...]) * (1.0 / l)
        o = o2[:tq] - lam * o2[tq:]
        ms = jnp.mean(o * o, axis=-1, keepdims=True)
        y = o * lax.rsqrt(ms + NORM_EPS) * gn_ref[...] * (1.0 - lam_init)
        out_ref[rows, :] = (y * _silu(g_ref[rows, :].astype(F32))).astype(BF16)

    scores(0, s0_sc)

    def pair(jj, carry):
        j = 2 * jj
        scores(j + 1, s1_sc)
        finish(j, s0_sc)
        scores(j + 2, s0_sc)
        finish(j + 1, s1_sc)
        return carry

    lax.fori_loop(0, n_blk // 2, pair, 0)


def _diff(proj, dl, lam_init, gn, layer, batch, seq, tq=256):
    blk = lambda off: pl.BlockSpec((seq, LANES), lambda b, h: (b, off // LANES + h))
    per_layer = lambda shape: pl.BlockSpec((None,) + shape, lambda b, h: (layer, 0, 0))
    return pl.pallas_call(
        functools.partial(_diff_kernel, seq=seq, tq=tq),
        grid=(batch, DIFF_HEADS),
        in_specs=[blk(OFF_BQ), blk(OFF_BK), blk(OFF_BV), blk(OFF_BG),
                  per_layer((4, DIFF_D)), per_layer((1, 1)), per_layer((1, LANES))],
        out_specs=pl.BlockSpec((seq, LANES), lambda b, h: (b, h)),
        out_shape=jax.ShapeDtypeStruct((batch * seq, BRANCH_W), BF16),
        scratch_shapes=[pltpu.VMEM((seq // tq, 2 * tq, LANES), BF16),
                        pltpu.VMEM((2 * tq, seq), F32), pltpu.VMEM((2 * tq, seq), F32)],
        compiler_params=pltpu.CompilerParams(
            dimension_semantics=("arbitrary", "arbitrary"), vmem_limit_bytes=VMEM_LIMIT),
        name="diff",
    )(proj, proj, proj, proj, dl, lam_init, gn)


def _dilated_kernel(q_ref, k_ref, v_ref, g_ref, out_ref,
                    nat_sc, c4_sc, k4_sc, q16_sc, k16_sc, w1_sc, w4_sc, w16_sc,
                    s0_sc, s1_sc, s2_sc, e0_sc, e1_sc, e2_sc, m0_sc, m1_sc, m2_sc,
                    o_sc, l_sc, bias_sc, *, seq, unroll):
    tq, win = DIL_TQ, 2 * DIL_TQ
    stage_rows = 2 * tq
    n_stage = seq // stage_rows
    len4, len16 = seq // 4, seq // 16
    head0 = _lane_iota((tq, LANES)) < DIL_HD
    head0_stage = _lane_iota((stage_rows, LANES)) < DIL_HD

    dcol = (lax.broadcasted_iota(jnp.int32, (tq, win), 1) - lax.broadcasted_iota(jnp.int32, (tq, win), 0))
    for idx in range(3):
        bias_sc[idx] = jnp.where(jnp.abs(dcol - idx * DIL_HALF) <= DIL_HALF, 0.0, NEG_INF)

    def stage(i, carry):
        rows = pl.ds(pl.multiple_of(i * stage_rows, stage_rows), stage_rows)
        nat_sc[0, rows, :] = q_ref[rows, :].astype(F32)
        nat_sc[1, rows, :] = k_ref[rows, :].astype(F32)
        nat_sc[2, rows, :] = v_ref[rows, :].astype(F32)
        return carry

    lax.fori_loop(0, n_stage, stage, 0)

    for t in range(3):
        for r in range(4):
            c4_sc[t, r * len4:(r + 1) * len4, :] = nat_sc[t, pl.ds(r, len4, stride=4), :]
    for r4 in range(4):
        for j in range(4):
            dst = slice((r4 + 4 * j) * len16, (r4 + 4 * j + 1) * len16)
            src = pl.ds(r4 * len4 + j, len16, stride=4)
            q16_sc[dst, :] = c4_sc[0, src, :]
            k16_sc[dst, :] = c4_sc[1, src, :].astype(BF16)
            w16_sc[0, dst, :] = c4_sc[2, src, :].astype(BF16)

    def value_weights(i, carry):
        rows = pl.ds(pl.multiple_of(i * stage_rows, stage_rows), stage_rows)
        k4_sc[rows, :] = c4_sc[1, rows, :].astype(BF16)
        for src, dst in ((v_ref, w1_sc), (c4_sc.at[2], w4_sc), (w16_sc.at[0], w16_sc)):
            v = src[rows, :].astype(F32)
            dst[1, rows, :] = jnp.where(head0_stage, 1.0, v).astype(BF16)
            dst[0, rows, :] = jnp.where(head0_stage, v, 1.0).astype(BF16)
        return carry

    lax.fori_loop(0, n_stage, value_weights, 0)

    def make_pattern(g, q_src, k_src, w_src, s_sc, e_sc, m_sc, seq_len, w, out_rows):
        per_seq = seq_len // tq

        def where(t):
            r, i = t // per_seq, t % per_seq
            ks = jnp.clip(i * tq - tq // 2, 0, seq_len - w)
            which = jnp.where(i == 0, 0, jnp.where(i == per_seq - 1, 2, 1)) if per_seq > 1 else 0
            qrows = pl.ds(pl.multiple_of(r * seq_len + i * tq, tq), tq)
            krows = pl.ds(pl.multiple_of(r * seq_len + ks, tq // 2), w)
            return qrows, krows, which

        def scores(t):
            qrows, krows, which = where(t)
            q = q_src[qrows, :]
            q2 = jnp.concatenate([jnp.where(head0, q, 0.0), jnp.where(head0, 0.0, q)], axis=0)
            bias = bias_sc[which, :, :w]
            s = _dot_nt(q2.astype(BF16), k_src[krows, :])
            s_sc[t, :, :w] = s + jnp.concatenate([bias, bias], axis=0)

        def softmax(t):
            s = s_sc[t, :, :w]
            m = jnp.max(s, axis=-1, keepdims=True)
            e_sc[t, :, :w] = jnp.exp(s - m).astype(BF16)
            m_sc[t] = jnp.where(head0, m[:tq], m[tq:])

        def values(t):
            _, krows, _ = where(t)
            out0 = _dot(e_sc[t, :tq, :w], w_src[0, krows, :])
            out1 = _dot(e_sc[t, tq:, :w], w_src[1, krows, :])
            l = pltpu.roll(jnp.where(head0, out1, out0), DIL_HD, 1)
            rows = out_rows(t)
            o_sc[g, rows, :] = jnp.where(head0, out0, out1) * (1.0 / l)
            l_sc[g, rows, :] = m_sc[t] + jnp.log(l)

        return scores, softmax, values

    sc0, sm0, va0 = make_pattern(0, nat_sc.at[0], k_ref, w1_sc, s0_sc, e0_sc, m0_sc, seq, win,
                                 lambda t: pl.ds(pl.multiple_of(t * tq, tq), tq))
    sc1, sm1, va1 = make_pattern(1, c4_sc.at[0], k4_sc, w4_sc, s1_sc, e1_sc, m1_sc, len4, win,
                                 lambda t: pl.ds(pl.multiple_of(t * tq, tq), tq))
    sc2, sm2, va2 = make_pattern(2, q16_sc, k16_sc, w16_sc, s2_sc, e2_sc, m2_sc, len16, len16,
                                 lambda t: pl.ds((t % 4) * len4 + t // 4, len16, stride=4))

    for passes in ((sc0,), (sm0, sc1), (va0, sm1, sc2), (va1, sm2), (va2,)):
        def body(t, carry, passes=passes):
            for f in passes:
                f(t)
            return carry

        lax.fori_loop(0, seq // tq, body, 0, unroll=unroll)

    y_sc = nat_sc.at[0]
    per_class = len4 // stage_rows

    def combine(i, carry):
        r, m0 = i // per_class, (i % per_class) * stage_rows
        rows = pl.ds(pl.multiple_of(r * len4 + m0, stage_rows), stage_rows)
        nat_rows = pl.ds(r + 4 * m0, stage_rows, stride=4)
        l0, l1, l2 = l_sc[0, nat_rows, :], l_sc[1, rows, :], l_sc[2, rows, :]
        m = jnp.maximum(jnp.maximum(l0, l1), l2)
        w0, w1, w2 = jnp.exp(l0 - m), jnp.exp(l1 - m), jnp.exp(l2 - m)
        o = w0 * o_sc[0, nat_rows, :] + w1 * o_sc[1, rows, :] + w2 * o_sc[2, rows, :]
        y_sc[nat_rows, :] = o * (1.0 / (w0 + w1 + w2))
        return carry

    lax.fori_loop(0, n_stage, combine, 0)

    def gate(i, carry):
        rows = pl.ds(pl.multiple_of(i * stage_rows, stage_rows), stage_rows)
        out_ref[rows, :] = (y_sc[rows, :] * _silu(g_ref[rows, :].astype(F32))).astype(BF16)
        return carry

    lax.fori_loop(0, n_stage, gate, 0)


def _dilated(proj, batch, seq, unroll=4):
    assert seq // 16 == DIL_TQ and all(w // (2 * d) == DIL_HALF for w, d in DIL_PATTERNS)
    blk = lambda off: pl.BlockSpec((seq, LANES), lambda b, p: (b, off // LANES + p))
    n_blocks = seq // DIL_TQ
    f32_rows = pltpu.VMEM((seq, LANES), F32)
    bf16_rows = pltpu.VMEM((seq, LANES), BF16)
    bf16_pair = pltpu.VMEM((2, seq, LANES), BF16)
    s_buf = pltpu.VMEM((n_blocks, 2 * DIL_TQ, 2 * DIL_TQ), F32)
    e_buf = pltpu.VMEM((n_blocks, 2 * DIL_TQ, 2 * DIL_TQ), BF16)
    m_buf = pltpu.VMEM((n_blocks, DIL_TQ, LANES), F32)
    return pl.pallas_call(
        functools.partial(_dilated_kernel, seq=seq, unroll=unroll),
        grid=(batch, DIL_HEADS // 2),
        in_specs=[blk(OFF_CQ), blk(OFF_CK), blk(OFF_CV), blk(OFF_CG)],
        out_specs=pl.BlockSpec((seq, LANES), lambda b, p: (b, p)),
        out_shape=jax.ShapeDtypeStruct((batch * seq, BRANCH_W), BF16),
        scratch_shapes=[
            pltpu.VMEM((3, seq, LANES), F32),
            pltpu.VMEM((3, seq, LANES), F32), bf16_rows, f32_rows, bf16_rows,
            bf16_pair, bf16_pair, bf16_pair,
            s_buf, s_buf, s_buf, e_buf, e_buf, e_buf, m_buf, m_buf, m_buf,
            pltpu.VMEM((3, seq, LANES), F32), pltpu.VMEM((3, seq, LANES), F32),
            pltpu.VMEM((3, DIL_TQ, 2 * DIL_TQ), F32),
        ],
        compiler_params=pltpu.CompilerParams(
            dimension_semantics=("arbitrary", "arbitrary"), vmem_limit_bytes=VMEM_LIMIT),
        name="dilated",
    )(proj, proj, proj, proj)


def _merge_kernel(x_ref, ya_ref, yb_ref, yc_ref, m_ref, wb_ref, wo_ref, fg_ref, out_ref, *, final):
    mixed = None
    for gi, y_ref in enumerate((ya_ref, yb_ref, yc_ref)):
        z = _dot(y_ref[...], wb_ref[gi])
        gate = _sigmoid(m_ref[:, gi * D_MODEL:(gi + 1) * D_MODEL].astype(F32))
        mixed = gate * z if mixed is None else mixed + gate * z
    x = x_ref[...] + _dot(mixed.astype(BF16), wo_ref[...])
    if final:
        ms = jnp.mean(x * x, axis=-1, keepdims=True)
        x = x * lax.rsqrt(ms + NORM_EPS) * fg_ref[...]
    out_ref[...] = x


def _merge(x2, ya, yb, yc, proj, wb, wo, final_g, layer, final, tm=512):
    rows = x2.shape[0]
    yspec = pl.BlockSpec((tm, BRANCH_W), lambda i: (i, 0))
    return pl.pallas_call(
        functools.partial(_merge_kernel, final=final),
        grid=(rows // tm,),
        in_specs=[
            pl.BlockSpec((tm, D_MODEL), lambda i: (i, 0)),
            yspec, yspec, yspec,
            pl.BlockSpec((tm, MERGE_W), lambda i: (i, 0)),
            pl.BlockSpec((None, N_BRANCH, BRANCH_W, D_MODEL), lambda i: (layer, 0, 0, 0)),
            pl.BlockSpec((None, D_MODEL, D_MODEL), lambda i: (layer, 0, 0)),
            pl.BlockSpec((1, D_MODEL), lambda i: (0, 0)),
        ],
        out_specs=pl.BlockSpec((tm, D_MODEL), lambda i: (i, 0)),
        out_shape=jax.ShapeDtypeStruct((rows, D_MODEL), F32),
        compiler_params=pltpu.CompilerParams(
            dimension_semantics=("arbitrary",), vmem_limit_bytes=VMEM_LIMIT),
        name="merge",
    )(x2, ya, yb, yc, proj, wb, wo, final_g)


def _rope_tables(seq, dim):
    inv = 1.0 / (ROPE_THETA ** (jnp.arange(0, dim, 2, dtype=F32) / dim))
    ang = jnp.arange(seq, dtype=F32)[:, None] * inv[None, :]
    cos, sin = jnp.cos(ang), jnp.sin(ang)
    reps = LANES // dim
    cos_t = jnp.tile(jnp.concatenate([cos, cos], axis=-1), (1, reps))
    sin_t = jnp.tile(jnp.concatenate([-sin, sin], axis=-1), (1, reps))
    return cos_t, sin_t


def kernel(x, norm_g, w_in, gla_gate_up, gla_gate_b, gla_norm_g, diff_lambda, diff_norm_g, w_branch, w_out, final_norm_g):
    batch, seq, _ = x.shape
    depth = w_in.shape[0]
    cos, sin = _rope_tables(seq, DIFF_D)

    lo, hi = LOWRANK_OFF, LOWRANK_OFF + LOWRANK_W
    merge_off = IN_COLS - MERGE_W
    w_bf = w_in.astype(BF16)
    w_main = jnp.concatenate([w_bf[:, :, merge_off:], w_bf[:, :, :lo], w_bf[:, :, hi:merge_off]], axis=-1)
    w_lr = w_bf[:, :, lo:hi]
    w_low = jnp.concatenate([w_lr, w_lr, w_lr, jnp.zeros_like(w_lr)], axis=-1)
    gu = gla_gate_up.reshape(depth, 2, GLA_RANK, GLA_HEADS, GLA_DK)
    zeros = jnp.zeros_like(gu[:, 0])
    gu_f = jnp.concatenate([gu[:, 0], zeros], axis=-1).reshape(depth, GLA_RANK, GLA_HEADS * LANES)
    gu_b = jnp.concatenate([zeros, gu[:, 1]], axis=-1).reshape(depth, GLA_RANK, GLA_HEADS * LANES)
    gu_cat = jnp.concatenate([gu_f, gu_b], axis=1)
    gu_hi = gu_cat.astype(BF16)
    gu_lo = (gu_cat - gu_hi.astype(F32)).astype(BF16)
    gate_up = jnp.concatenate([gu_hi, gu_lo, gu_hi, jnp.zeros_like(gu_hi)], axis=1)
    gb = gla_gate_b.reshape(depth, 2, GLA_HEADS, GLA_DK)
    gate_b = jnp.concatenate([gb[:, 0], gb[:, 1]], axis=-1).reshape(depth, 1, GLA_HEADS * LANES)
    wb = w_branch.astype(BF16)
    wo = w_out.astype(BF16)
    lam_init = jnp.asarray([0.8 - 0.6 * math.exp(-0.3 * layer) for layer in range(depth)],
                           F32).reshape(depth, 1, 1)
    norm_g3 = norm_g[:, None, :]
    gla_norm_g3 = gla_norm_g[:, None, :]
    diff_norm_g3 = diff_norm_g[:, None, :]

    x2 = x.reshape(batch * seq, D_MODEL)
    for layer in range(depth):
        proj, la = _in_proj(x2, norm_g3, w_main, w_low, gate_up, gate_b, cos, sin, layer, seq)
        ya = _gla(proj, la, gla_norm_g3, layer, batch, seq)
        yb = _diff(proj, diff_lambda, lam_init, diff_norm_g3, layer, batch, seq)
        yc = _dilated(proj, batch, seq)
        x2 = _merge(x2, ya, yb, yc, proj, wb, wo, final_norm_g[None], layer, final=layer == depth - 1)
    return x2.reshape(batch, seq, D_MODEL)
```

```python
import functools
import math

import jax
import jax.numpy as jnp
from jax import lax
from jax.experimental import pallas as pl
from jax.experimental.pallas import tpu as pltpu

F32 = jnp.float32
BF16 = jnp.bfloat16

LANES = 128

D_MODEL = 1024
ROPE_THETA = 10000.0
NORM_EPS = 1e-6
BRANCH_W = D_MODEL // 2
N_BRANCH = 3

GLA_HEADS = 4
GLA_DV = BRANCH_W // GLA_HEADS
GLA_DK = GLA_DV // 2
GLA_RANK = 16
GLA_TAU = 16.0
GLA_CHUNK = 64
GLA_QK_W = GLA_HEADS * GLA_DK

DIFF_HEADS = 4
DIFF_D = BRANCH_W // (2 * DIFF_HEADS)
DIFF_QK_W = DIFF_HEADS * 2 * DIFF_D

DIL_HEADS = 8
DIL_HD = BRANCH_W // DIL_HEADS
DIL_W = DIL_HEADS * DIL_HD
DIL_PATTERNS = ((128, 1), (512, 4), (2048, 16))
DIL_HALF = 64
DIL_TQ = 128
NEG_INF = -1e30

MERGE_W = N_BRANCH * D_MODEL
LOWRANK_OFF = GLA_QK_W * 2 + BRANCH_W * 2
LOWRANK_W = 2 * GLA_RANK
IN_COLS = LOWRANK_OFF + LOWRANK_W + 2 * (DIFF_QK_W * 2 + BRANCH_W * 2) + MERGE_W
MAIN_W = IN_COLS - LOWRANK_W

OFF_AQ = MERGE_W
OFF_AK = OFF_AQ + GLA_QK_W
OFF_AV = OFF_AK + GLA_QK_W
OFF_AG = OFF_AV + BRANCH_W
OFF_BQ = OFF_AG + BRANCH_W
OFF_BK = OFF_BQ + DIFF_QK_W
OFF_BV = OFF_BK + DIFF_QK_W
OFF_BG = OFF_BV + BRANCH_W
OFF_CQ = OFF_BG + BRANCH_W
OFF_CK = OFF_CQ + DIL_W
OFF_CV = OFF_CK + DIL_W
OFF_CG = OFF_CV + BRANCH_W

VMEM_LIMIT = 56 * 1024 * 1024


def _dot(a, b):
    return jnp.dot(a, b, preferred_element_type=F32)


def _dot_nt(a, b):
    return lax.dot_general(a, b, (((1,), (1,)), ((), ())), preferred_element_type=F32)


def _sigmoid(x):
    return 1.0 / (1.0 + jnp.exp(-x))


def _silu(x):
    return x * _sigmoid(x)


def _lane_iota(shape):
    return lax.broadcasted_iota(jnp.int32, shape, len(shape) - 1)


def _split_bf16(x):
    hi = x.astype(BF16).astype(F32)
    return hi, x - hi


def _rope(x, cos, sin_signed):
    lane = _lane_iota(x.shape)
    first_half = (lane % 64) < 32
    partner = jnp.where(first_half, pltpu.roll(x, 96, 1), pltpu.roll(x, 32, 1))
    return x * cos + partner * sin_signed


def _stage_w_kernel(w_ref, tail_ref, out_ref):
    shifted_tiles = (IN_COLS - LOWRANK_OFF - LOWRANK_W) // LANES
    bc_tiles = (OFF_CG + BRANCH_W - OFF_BQ) // LANES
    sel = (lax.broadcasted_iota(jnp.int32, (2 * LANES, LANES), 0)
           == lax.broadcasted_iota(jnp.int32, (2 * LANES, LANES), 1) + LOWRANK_W).astype(F32).astype(BF16)
    out_ref[:, OFF_AQ:OFF_AQ + LOWRANK_OFF] = w_ref[:, :LOWRANK_OFF].astype(BF16)
    for j in range(shifted_tiles):
        dst = OFF_BQ + LANES * j if j < bc_tiles else LANES * (j - bc_tiles)
        if j == shifted_tiles - 1:
            tile = tail_ref[...]
        else:
            lo = LOWRANK_OFF + LANES * j
            tile = _dot(w_ref[:, lo:lo + 2 * LANES].astype(BF16), sel).astype(BF16)
        out_ref[:, dst:dst + LANES] = tile


def _stage_w(w_in, tk=256):
    depth = w_in.shape[0]
    tail = w_in[:, :, IN_COLS - LANES:].astype(BF16)
    return pl.pallas_call(
        _stage_w_kernel,
        grid=(depth, D_MODEL // tk),
        in_specs=[pl.BlockSpec((None, tk, IN_COLS), lambda l, i: (l, i, 0)),
                  pl.BlockSpec((None, tk, LANES), lambda l, i: (l, i, 0))],
        out_specs=pl.BlockSpec((None, tk, MAIN_W), lambda l, i: (l, i, 0)),
        out_shape=jax.ShapeDtypeStruct((depth, D_MODEL, MAIN_W), BF16),
        compiler_params=pltpu.CompilerParams(
            dimension_semantics=("arbitrary", "arbitrary"), vmem_limit_bytes=VMEM_LIMIT),
        name="stage_w",
    )(w_in, tail)


_PLAIN, _ROPE_Q, _ROPE_K = 0, 1, 2
_PROJ_SEGMENTS = (
    (0, OFF_BQ // 2, _PLAIN), (OFF_BQ // 2, OFF_BQ // 2, _PLAIN),
    (OFF_BQ, DIFF_QK_W, _ROPE_Q), (OFF_BK, DIFF_QK_W, _ROPE_K),
    (OFF_BV, 2 * BRANCH_W, _PLAIN),
    (OFF_CQ, DIL_W, _ROPE_Q), (OFF_CK, DIL_W, _ROPE_K),
    (OFF_CV, 2 * BRANCH_W, _PLAIN),
)
LOG2E = 1.4426950408889634
LN2 = 0.6931471805599453
QK_SCALE = DIFF_D ** -0.5 * LOG2E


def _in_proj_kernel(x_ref, g_ref, w_ref, wl_ref, gu_ref, gb_ref, cos_ref, sin_ref, out_ref, la_ref):
    x = x_ref[...]
    ms = jnp.mean(x * x, axis=-1, keepdims=True)
    hb = (x * lax.rsqrt(ms + NORM_EPS) * g_ref[...]).astype(BF16)
    low = _dot(hb, wl_ref[...])
    low_hi, low_lo = _split_bf16(low)
    lhs = jnp.where(_lane_iota(low.shape) < 2 * LOWRANK_W, low_hi, low_lo).astype(BF16)
    z = _dot(lhs, gu_ref[...]) + gb_ref[...]
    log_sig = jnp.minimum(z, 0.0) - jnp.log1p(jnp.exp(-jnp.abs(z)))
    la_ref[...] = log_sig * (1.0 / GLA_TAU)
    for start, width, kind in _PROJ_SEGMENTS:
        res = _dot(hb, w_ref[:, start:start + width])
        if kind != _PLAIN:
            cos, sin = cos_ref[...], sin_ref[...]
            tiles = [_rope(res[:, c:c + LANES], cos, sin) for c in range(0, width, LANES)]
            res = jnp.concatenate(tiles, axis=1)
            if kind == _ROPE_Q:
                res = res * QK_SCALE
        out_ref[:, start:start + width] = res.astype(BF16)


def _in_proj(x2, g, w_main, w_low, gate_up, gate_b, cos, sin, layer, seq, tm=512):
    assert DIFF_D == DIL_HD and seq % tm == 0
    rows = x2.shape[0]
    resident = dict(pipeline_mode=pl.Buffered(1))
    pos_blocks = seq // tm
    return pl.pallas_call(
        _in_proj_kernel,
        grid=(rows // tm,),
        in_specs=[
            pl.BlockSpec((tm, D_MODEL), lambda i: (i, 0)),
            pl.BlockSpec((None, 1, D_MODEL), lambda i: (layer, 0, 0)),
            pl.BlockSpec((None, D_MODEL, MAIN_W), lambda i: (layer, 0, 0), **resident),
            pl.BlockSpec((None, D_MODEL, LANES), lambda i: (layer, 0, 0), **resident),
            pl.BlockSpec((None, LANES, 4 * LANES), lambda i: (layer, 0, 0), **resident),
            pl.BlockSpec((None, 1, 4 * LANES), lambda i: (layer, 0, 0)),
            pl.BlockSpec((tm, LANES), lambda i: (i % pos_blocks, 0)),
            pl.BlockSpec((tm, LANES), lambda i: (i % pos_blocks, 0)),
        ],
        out_specs=[
            pl.BlockSpec((tm, MAIN_W), lambda i: (i, 0)),
            pl.BlockSpec((tm, 4 * LANES), lambda i: (i, 0)),
        ],
        out_shape=[
            jax.ShapeDtypeStruct((rows, MAIN_W), BF16),
            jax.ShapeDtypeStruct((rows, 4 * LANES), F32),
        ],
        compiler_params=pltpu.CompilerParams(
            dimension_semantics=("arbitrary",), vmem_limit_bytes=VMEM_LIMIT),
        name="in_proj",
    )(x2, g, w_main, w_low, gate_up, gate_b, cos, sin)


def _gla_kernel(q_ref, k_ref, v_ref, g_ref, la_ref, gn_ref, out_ref,
                u_sc, d_sc, st_sc, qh_sc, kh_sc, att_sc, *, seq, unroll):
    C = GLA_CHUNK
    n_chunks = seq // C
    lane_c = _lane_iota((C, LANES))
    row_c = lax.broadcasted_iota(jnp.int32, (C, LANES), 0)
    fwd_c = lane_c < GLA_DK
    tri = jnp.where(fwd_c, row_c - lane_c, lane_c - GLA_DK - row_c) >= 0
    tri_bf = tri.astype(F32).astype(BF16)
    tri2 = jnp.concatenate([tri_bf, tri_bf], axis=1)
    fwd_sq = _lane_iota((LANES, LANES)) < GLA_DK
    scale = GLA_DK ** -0.5

    def head_cols(hh):
        return slice(hh * LANES, (hh + 1) * LANES)

    def dup_heads(ref, rows):
        xx = ref[rows, :].astype(F32)
        rolled = pltpu.roll(xx, GLA_DK, 1)
        return jnp.where(fwd_c, xx, rolled), jnp.where(fwd_c, rolled, xx)

    def prep(n, carry):
        rows = pl.ds(pl.multiple_of(n * C, C), C)
        q2 = dup_heads(q_ref, rows)
        k2 = dup_heads(k_ref, rows)
        for hh in range(2):
            cols = head_cols(hh)
            la = la_ref[rows, cols]
            la_bd = jnp.concatenate([jnp.where(fwd_c, la, 0.0), jnp.where(fwd_c, 0.0, la)], axis=0)
            la_hi, la_lo = _split_bf16(la_bd)
            b = _dot(tri2, jnp.concatenate([la_hi, la_lo], axis=0).astype(BF16))
            tot = jnp.sum(la, axis=0, keepdims=True)
            qh_sc[hh, rows, :] = (q2[hh] * jnp.exp(b) * scale).astype(BF16)
            k_hat = k2[hh] * jnp.exp(-b)
            kh_sc[hh, n] = jnp.concatenate(
                [jnp.where(fwd_c, k_hat, 0.0), jnp.where(fwd_c, 0.0, k_hat)], axis=0).astype(BF16)
            k_end = (k2[hh] * jnp.exp(tot - b)).astype(BF16)
            v_t = v_ref[rows, cols].astype(F32).T.astype(BF16)
            u_sc[hh, n] = _dot(v_t, k_end)
            d_sc[hh, n] = jnp.broadcast_to(jnp.exp(tot), (8, LANES))
        return carry

    lax.fori_loop(0, n_chunks, prep, 0, unroll=unroll)

    def scan(i, states):
        j = n_chunks - 1 - i
        new = []
        for hh in range(2):
            st_sc[hh, i] = states[hh]
            upd = jnp.where(fwd_sq, u_sc[hh, i], u_sc[hh, j])
            dec = jnp.where(fwd_sq[:8], d_sc[hh, i], d_sc[hh, j])[0:1]
            new.append(states[hh] * dec + upd)
        return tuple(new)

    zero = jnp.zeros((LANES, LANES), F32)
    lax.fori_loop(0, n_chunks, scan, (zero, zero))

    def attend(n, carry):
        rows = pl.ds(pl.multiple_of(n * C, C), C)
        for hh in range(2):
            att = _dot_nt(qh_sc[hh, rows, :], kh_sc[hh, n])
            att_sc[hh, rows, :] = jnp.where(tri, att, 0.0).astype(BF16)
        return carry

    lax.fori_loop(0, n_chunks, attend, 0, unroll=2 * unroll)

    def emit(n, carry):
        rows = pl.ds(pl.multiple_of(n * C, C), C)
        for hh in range(2):
            cols = head_cols(hh)
            v = v_ref[rows, cols]
            o = _dot(att_sc[hh, rows, :], jnp.concatenate([v, v], axis=0))
            state = jnp.where(fwd_sq, st_sc[hh, n], st_sc[hh, n_chunks - 1 - n]).astype(BF16)
            o = o + _dot_nt(qh_sc[hh, rows, :], state)
            ms = jnp.mean(o * o, axis=-1, keepdims=True)
            y = o * lax.rsqrt(ms + NORM_EPS) * gn_ref[...]
            out_ref[rows, cols] = (y * _silu(g_ref[rows, cols].astype(F32))).astype(BF16)
        return carry

    lax.fori_loop(0, n_chunks, emit, 0, unroll=unroll)


def _gla(proj, la, gn, layer, batch, seq, unroll=8):
    n_chunks = seq // GLA_CHUNK
    pair = 2 * LANES
    return pl.pallas_call(
        functools.partial(_gla_kernel, seq=seq, unroll=unroll),
        grid=(batch, GLA_HEADS // 2),
        in_specs=[
            pl.BlockSpec((seq, LANES), lambda b, p: (b, OFF_AQ // LANES + p)),
            pl.BlockSpec((seq, LANES), lambda b, p: (b, OFF_AK // LANES + p)),
            pl.BlockSpec((seq, pair), lambda b, p: (b, OFF_AV // pair + p)),
            pl.BlockSpec((seq, pair), lambda b, p: (b, OFF_AG // pair + p)),
            pl.BlockSpec((seq, pair), lambda b, p: (b, p)),
            pl.BlockSpec((None, 1, LANES), lambda b, p: (layer, 0, 0)),
        ],
        out_specs=pl.BlockSpec((seq, pair), lambda b, p: (b, p)),
        out_shape=jax.ShapeDtypeStruct((batch * seq, BRANCH_W), BF16),
        scratch_shapes=[
            pltpu.VMEM((2, n_chunks, LANES, LANES), F32),
            pltpu.VMEM((2, n_chunks, 8, LANES), F32),
            pltpu.VMEM((2, n_chunks, LANES, LANES), F32),
            pltpu.VMEM((2, seq, LANES), BF16),
            pltpu.VMEM((2, n_chunks, LANES, LANES), BF16),
            pltpu.VMEM((2, seq, LANES), BF16),
        ],
        compiler_params=pltpu.CompilerParams(
            dimension_semantics=("arbitrary", "arbitrary"), vmem_limit_bytes=VMEM_LIMIT),
        name="gla",
    )(proj, proj, proj, proj, la, gn)


def _diff_kernel(q_ref, k_ref, v_ref, g_ref, dl_ref, li_ref, gn_ref, out_ref,
                 q_sc, s0_sc, s1_sc, pm0_sc, pm1_sc, *, seq, tq):
    n_blk = seq // tq
    half0 = _lane_iota((tq, LANES)) < DIFF_D

    def stack_q(i, carry):
        q = q_ref[pl.ds(pl.multiple_of(i * tq, tq), tq), :].astype(F32)
        q_sc[i] = jnp.concatenate([jnp.where(half0, q, 0.0), jnp.where(half0, 0.0, q)], axis=0).astype(BF16)
        return carry

    lax.fori_loop(0, n_blk, stack_q, 0)

    dl = dl_ref[...]
    lam_init = li_ref[...]
    lam = (jnp.exp(jnp.sum(dl[0:1] * dl[1:2], axis=-1, keepdims=True))
           - jnp.exp(jnp.sum(dl[2:3] * dl[3:4], axis=-1, keepdims=True)) + lam_init)

    def scores(i, s_sc, pm_sc):
        s = _dot_nt(q_sc[jnp.minimum(i, n_blk - 1)], k_ref[...])
        s_sc[...] = s
        pm_sc[...] = functools.reduce(jnp.maximum, [s[:, c:c + LANES] for c in range(0, seq, LANES)])

    def finish(i, s_sc, pm_sc):
        rows = pl.ds(pl.multiple_of(i * tq, tq), tq)
        m = jnp.max(pm_sc[...], axis=-1, keepdims=True)
        e = jnp.exp2(s_sc[...] - m)
        l = jnp.sum(e, axis=-1, keepdims=True)
        o2 = _dot(e.astype(BF16), v_ref[...]) * (1.0 / l)
        o = o2[:tq] - lam * o2[tq:]
        ms = jnp.mean(o * o, axis=-1, keepdims=True)
        y = o * lax.rsqrt(ms + NORM_EPS) * gn_ref[...] * (1.0 - lam_init)
        out_ref[rows, :] = (y * _silu(g_ref[rows, :].astype(F32))).astype(BF16)

    scores(0, s0_sc, pm0_sc)

    def pair(jj, carry):
        j = 2 * jj
        scores(j + 1, s1_sc, pm1_sc)
        finish(j, s0_sc, pm0_sc)
        scores(j + 2, s0_sc, pm0_sc)
        finish(j + 1, s1_sc, pm1_sc)
        return carry

    lax.fori_loop(0, n_blk // 2, pair, 0)


def _diff(proj, dl, lam_init, gn, layer, batch, seq, tq=256):
    blk = lambda off: pl.BlockSpec((seq, LANES), lambda b, h: (b, off // LANES + h))
    per_layer = lambda shape: pl.BlockSpec((None,) + shape, lambda b, h: (layer, 0, 0))
    return pl.pallas_call(
        functools.partial(_diff_kernel, seq=seq, tq=tq),
        grid=(batch, DIFF_HEADS),
        in_specs=[blk(OFF_BQ), blk(OFF_BK), blk(OFF_BV), blk(OFF_BG),
                  per_layer((4, DIFF_D)), per_layer((1, 1)), per_layer((1, LANES))],
        out_specs=pl.BlockSpec((seq, LANES), lambda b, h: (b, h)),
        out_shape=jax.ShapeDtypeStruct((batch * seq, BRANCH_W), BF16),
        scratch_shapes=[pltpu.VMEM((seq // tq, 2 * tq, LANES), BF16),
                        pltpu.VMEM((2 * tq, seq), F32), pltpu.VMEM((2 * tq, seq), F32),
                        pltpu.VMEM((2 * tq, LANES), F32), pltpu.VMEM((2 * tq, LANES), F32)],
        compiler_params=pltpu.CompilerParams(
            dimension_semantics=("arbitrary", "arbitrary"), vmem_limit_bytes=VMEM_LIMIT),
        name="diff",
    )(proj, proj, proj, proj, dl, lam_init, gn)


def _dilated_kernel(q_ref, k_ref, v_ref, g_ref, out_ref,
                    nat_sc, c4_sc, k4_sc, q16_sc, k16_sc, w1_sc, w4_sc, w16_sc,
                    s0_sc, s1_sc, s2_sc, e0_sc, e1_sc, e2_sc, m0_sc, m1_sc, m2_sc,
                    o_sc, l_sc, bias_sc, *, seq, unroll):
    tq, win = DIL_TQ, 2 * DIL_TQ
    stage_rows = 2 * tq
    n_stage = seq // stage_rows
    len4, len16 = seq // 4, seq // 16
    head0 = _lane_iota((tq, LANES)) < DIL_HD
    head0_stage = _lane_iota((stage_rows, LANES)) < DIL_HD

    dcol = (lax.broadcasted_iota(jnp.int32, (tq, win), 1) - lax.broadcasted_iota(jnp.int32, (tq, win), 0))
    for idx in range(3):
        bias_sc[idx] = jnp.where(jnp.abs(dcol - idx * DIL_HALF) <= DIL_HALF, 0.0, NEG_INF)

    def stage(i, carry):
        rows = pl.ds(pl.multiple_of(i * stage_rows, stage_rows), stage_rows)
        nat_sc[0, rows, :] = q_ref[rows, :].astype(F32)
        nat_sc[1, rows, :] = k_ref[rows, :].astype(F32)
        nat_sc[2, rows, :] = v_ref[rows, :].astype(F32)
        return carry

    lax.fori_loop(0, n_stage, stage, 0)

    for t in range(3):
        for r in range(4):
            c4_sc[t, r * len4:(r + 1) * len4, :] = nat_sc[t, pl.ds(r, len4, stride=4), :]
    for r4 in range(4):
        for j in range(4):
            dst = slice((r4 + 4 * j) * len16, (r4 + 4 * j + 1) * len16)
            src = pl.ds(r4 * len4 + j, len16, stride=4)
            q16_sc[dst, :] = c4_sc[0, src, :]
            k16_sc[dst, :] = c4_sc[1, src, :].astype(BF16)
            w16_sc[0, dst, :] = c4_sc[2, src, :].astype(BF16)

    def value_weights(i, carry):
        rows = pl.ds(pl.multiple_of(i * stage_rows, stage_rows), stage_rows)
        k4_sc[rows, :] = c4_sc[1, rows, :].astype(BF16)
        for src, dst in ((v_ref, w1_sc), (c4_sc.at[2], w4_sc), (w16_sc.at[0], w16_sc)):
            v = src[rows, :].astype(F32)
            dst[1, rows, :] = jnp.where(head0_stage, 1.0, v).astype(BF16)
            dst[0, rows, :] = jnp.where(head0_stage, v, 1.0).astype(BF16)
        return carry

    lax.fori_loop(0, n_stage, value_weights, 0)

    def make_pattern(g, q_src, k_src, w_src, s_sc, e_sc, m_sc, seq_len, w, out_rows):
        per_seq = seq_len // tq

        def where(t):
            r, i = t // per_seq, t % per_seq
            ks = jnp.clip(i * tq - tq // 2, 0, seq_len - w)
            which = jnp.where(i == 0, 0, jnp.where(i == per_seq - 1, 2, 1)) if per_seq > 1 else 0
            qrows = pl.ds(pl.multiple_of(r * seq_len + i * tq, tq), tq)
            krows = pl.ds(pl.multiple_of(r * seq_len + ks, tq // 2), w)
            return qrows, krows, which

        def scores(t):
            qrows, krows, which = where(t)
            q = q_src[qrows, :]
            q2 = jnp.concatenate([jnp.where(head0, q, 0.0), jnp.where(head0, 0.0, q)], axis=0)
            bias = bias_sc[which, :, :w]
            s = _dot_nt(q2.astype(BF16), k_src[krows, :])
            s_sc[t, :, :w] = s + jnp.concatenate([bias, bias], axis=0)

        def softmax(t):
            s = s_sc[t, :, :w]
            m = jnp.max(s, axis=-1, keepdims=True)
            e_sc[t, :, :w] = jnp.exp2(s - m).astype(BF16)
            m_sc[t] = jnp.where(head0, m[:tq], m[tq:])

        def values(t):
            _, krows, _ = where(t)
            out0 = _dot(e_sc[t, :tq, :w], w_src[0, krows, :])
            out1 = _dot(e_sc[t, tq:, :w], w_src[1, krows, :])
            l = pltpu.roll(jnp.where(head0, out1, out0), DIL_HD, 1)
            rows = out_rows(t)
            o_sc[g, rows, :] = jnp.where(head0, out0, out1) * (1.0 / l)
            l_sc[g, rows, :] = m_sc[t] * LN2 + jnp.log(l)

        return scores, softmax, values

    sc0, sm0, va0 = make_pattern(0, nat_sc.at[0], k_ref, w1_sc, s0_sc, e0_sc, m0_sc, seq, win,
                                 lambda t: pl.ds(pl.multiple_of(t * tq, tq), tq))
    sc1, sm1, va1 = make_pattern(1, c4_sc.at[0], k4_sc, w4_sc, s1_sc, e1_sc, m1_sc, len4, win,
                                 lambda t: pl.ds(pl.multiple_of(t * tq, tq), tq))
    sc2, sm2, va2 = make_pattern(2, q16_sc, k16_sc, w16_sc, s2_sc, e2_sc, m2_sc, len16, len16,
                                 lambda t: pl.ds((t % 4) * len4 + t // 4, len16, stride=4))

    for passes in ((sc0,), (sm0, sc1), (va0, sm1, sc2), (va1, sm2), (va2,)):
        def body(t, carry, passes=passes):
            for f in passes:
                f(t)
            return carry

        lax.fori_loop(0, seq // tq, body, 0, unroll=unroll)

    y_sc = nat_sc.at[0]
    per_class = len4 // stage_rows

    def combine(i, carry):
        r, m0 = i // per_class, (i % per_class) * stage_rows
        rows = pl.ds(pl.multiple_of(r * len4 + m0, stage_rows), stage_rows)
        nat_rows = pl.ds(r + 4 * m0, stage_rows, stride=4)
        l0, l1, l2 = l_sc[0, nat_rows, :], l_sc[1, rows, :], l_sc[2, rows, :]
        m = jnp.maximum(jnp.maximum(l0, l1), l2)
        w0, w1, w2 = jnp.exp(l0 - m), jnp.exp(l1 - m), jnp.exp(l2 - m)
        o = w0 * o_sc[0, nat_rows, :] + w1 * o_sc[1, rows, :] + w2 * o_sc[2, rows, :]
        y_sc[nat_rows, :] = o * (1.0 / (w0 + w1 + w2))
        return carry

    lax.fori_loop(0, n_stage, combine, 0)

    def gate(i, carry):
        rows = pl.ds(pl.multiple_of(i * stage_rows, stage_rows), stage_rows)
        out_ref[rows, :] = (y_sc[rows, :] * _silu(g_ref[rows, :].astype(F32))).astype(BF16)
        return carry

    lax.fori_loop(0, n_stage, gate, 0)


def _dilated(proj, batch, seq, unroll=8):
    assert seq // 16 == DIL_TQ and all(w // (2 * d) == DIL_HALF for w, d in DIL_PATTERNS)
    blk = lambda off: pl.BlockSpec((seq, LANES), lambda b, p: (b, off // LANES + p))
    n_blocks = seq // DIL_TQ
    f32_rows = pltpu.VMEM((seq, LANES), F32)
    bf16_rows = pltpu.VMEM((seq, LANES), BF16)
    bf16_pair = pltpu.VMEM((2, seq, LANES), BF16)
    s_buf = pltpu.VMEM((n_blocks, 2 * DIL_TQ, 2 * DIL_TQ), F32)
    e_buf = pltpu.VMEM((n_blocks, 2 * DIL_TQ, 2 * DIL_TQ), BF16)
    m_buf = pltpu.VMEM((n_blocks, DIL_TQ, LANES), F32)
    return pl.pallas_call(
        functools.partial(_dilated_kernel, seq=seq, unroll=unroll),
        grid=(batch, DIL_HEADS // 2),
        in_specs=[blk(OFF_CQ), blk(OFF_CK), blk(OFF_CV), blk(OFF_CG)],
        out_specs=pl.BlockSpec((seq, LANES), lambda b, p: (b, p)),
        out_shape=jax.ShapeDtypeStruct((batch * seq, BRANCH_W), BF16),
        scratch_shapes=[
            pltpu.VMEM((3, seq, LANES), F32),
            pltpu.VMEM((3, seq, LANES), F32), bf16_rows, f32_rows, bf16_rows,
            bf16_pair, bf16_pair, bf16_pair,
            s_buf, s_buf, s_buf, e_buf, e_buf, e_buf, m_buf, m_buf, m_buf,
            pltpu.VMEM((3, seq, LANES), F32), pltpu.VMEM((3, seq, LANES), F32),
            pltpu.VMEM((3, DIL_TQ, 2 * DIL_TQ), F32),
        ],
        compiler_params=pltpu.CompilerParams(
            dimension_semantics=("arbitrary", "arbitrary"), vmem_limit_bytes=VMEM_LIMIT),
        name="dilated",
    )(proj, proj, proj, proj)


def _merge_kernel(x_ref, ya_ref, yb_ref, yc_ref, m_ref, wb_ref, wo_ref, fg_ref, out_ref, *, final):
    mixed = None
    for gi, y_ref in enumerate((ya_ref, yb_ref, yc_ref)):
        z = _dot(y_ref[...], wb_ref[gi])
        gate = _sigmoid(m_ref[:, gi * D_MODEL:(gi + 1) * D_MODEL].astype(F32))
        mixed = gate * z if mixed is None else mixed + gate * z
    x = x_ref[...] + _dot(mixed.astype(BF16), wo_ref[...])
    if final:
        ms = jnp.mean(x * x, axis=-1, keepdims=True)
        x = x * lax.rsqrt(ms + NORM_EPS) * fg_ref[...]
    out_ref[...] = x


def _merge(x2, ya, yb, yc, proj, wb, wo, final_g, layer, final, tm=512):
    rows = x2.shape[0]
    yspec = pl.BlockSpec((tm, BRANCH_W), lambda i: (i, 0))
    return pl.pallas_call(
        functools.partial(_merge_kernel, final=final),
        grid=(rows // tm,),
        in_specs=[
            pl.BlockSpec((tm, D_MODEL), lambda i: (i, 0)),
            yspec, yspec, yspec,
            pl.BlockSpec((tm, MERGE_W), lambda i: (i, 0)),
            pl.BlockSpec((None, N_BRANCH, BRANCH_W, D_MODEL), lambda i: (layer, 0, 0, 0)),
            pl.BlockSpec((None, D_MODEL, D_MODEL), lambda i: (layer, 0, 0)),
            pl.BlockSpec((1, D_MODEL), lambda i: (0, 0)),
        ],
        out_specs=pl.BlockSpec((tm, D_MODEL), lambda i: (i, 0)),
        out_shape=jax.ShapeDtypeStruct((rows, D_MODEL), F32),
        compiler_params=pltpu.CompilerParams(
            dimension_semantics=("arbitrary",), vmem_limit_bytes=VMEM_LIMIT),
        name="merge",
    )(x2, ya, yb, yc, proj, wb, wo, final_g)


def _rope_tables(seq, dim):
    inv = 1.0 / (ROPE_THETA ** (jnp.arange(0, dim, 2, dtype=F32) / dim))
    ang = jnp.arange(seq, dtype=F32)[:, None] * inv[None, :]
    cos, sin = jnp.cos(ang), jnp.sin(ang)
    reps = LANES // dim
    cos_t = jnp.tile(jnp.concatenate([cos, cos], axis=-1), (1, reps))
    sin_t = jnp.tile(jnp.concatenate([-sin, sin], axis=-1), (1, reps))
    return cos_t, sin_t


def kernel(x, norm_g, w_in, gla_gate_up, gla_gate_b, gla_norm_g, diff_lambda, diff_norm_g, w_branch, w_out, final_norm_g):
    batch, seq, _ = x.shape
    depth = w_in.shape[0]
    cos, sin = _rope_tables(seq, DIFF_D)

    lo, hi = LOWRANK_OFF, LOWRANK_OFF + LOWRANK_W
    w_main = _stage_w(w_in)
    w_lr = w_in[:, :, lo:hi].astype(BF16)
    w_low = jnp.concatenate([w_lr, w_lr, w_lr, jnp.zeros_like(w_lr)], axis=-1)
    gu = gla_gate_up.reshape(depth, 2, GLA_RANK, GLA_HEADS, GLA_DK)
    zeros = jnp.zeros_like(gu[:, 0])
    gu_f = jnp.concatenate([gu[:, 0], zeros], axis=-1).reshape(depth, GLA_RANK, GLA_HEADS * LANES)
    gu_b = jnp.concatenate([zeros, gu[:, 1]], axis=-1).reshape(depth, GLA_RANK, GLA_HEADS * LANES)
    gu_cat = jnp.concatenate([gu_f, gu_b], axis=1)
    gu_hi = gu_cat.astype(BF16)
    gu_lo = (gu_cat - gu_hi.astype(F32)).astype(BF16)
    gate_up = jnp.concatenate([gu_hi, gu_lo, gu_hi, jnp.zeros_like(gu_hi)], axis=1)
    gb = gla_gate_b.reshape(depth, 2, GLA_HEADS, GLA_DK)
    gate_b = jnp.concatenate([gb[:, 0], gb[:, 1]], axis=-1).reshape(depth, 1, GLA_HEADS * LANES)
    wb = w_branch.astype(BF16)
    wo = w_out.astype(BF16)
    lam_init = jnp.asarray([0.8 - 0.6 * math.exp(-0.3 * layer) for layer in range(depth)],
                           F32).reshape(depth, 1, 1)
    norm_g3 = norm_g[:, None, :]
    gla_norm_g3 = gla_norm_g[:, None, :]
    diff_norm_g3 = diff_norm_g[:, None, :]

    x2 = x.reshape(batch * seq, D_MODEL)
    for layer in range(depth):
        proj, la = _in_proj(x2, norm_g3, w_main, w_low, gate_up, gate_b, cos, sin, layer, seq)
        ya = _gla(proj, la, gla_norm_g3, layer, batch, seq)
        yb = _diff(proj, diff_lambda, lam_init, diff_norm_g3, layer, batch, seq)
        yc = _dilated(proj, batch, seq)
        x2 = _merge(x2, ya, yb, yc, proj, wb, wo, final_norm_g[None], layer, final=layer == depth - 1)
    return x2.reshape(batch, seq, D_MODEL)
```

```python
import functools
import math

import jax
import jax.numpy as jnp
from jax import lax
from jax.experimental import pallas as pl
from jax.experimental.pallas import tpu as pltpu

F32 = jnp.float32
BF16 = jnp.bfloat16

LANES = 128

D_MODEL = 1024
ROPE_THETA = 10000.0
NORM_EPS = 1e-6
BRANCH_W = D_MODEL // 2
N_BRANCH = 3

GLA_HEADS = 4
GLA_DV = BRANCH_W // GLA_HEADS
GLA_DK = GLA_DV // 2
GLA_RANK = 16
GLA_TAU = 16.0
GLA_CHUNK = 64
GLA_QK_W = GLA_HEADS * GLA_DK

DIFF_HEADS = 4
DIFF_D = BRANCH_W // (2 * DIFF_HEADS)
DIFF_QK_W = DIFF_HEADS * 2 * DIFF_D

DIL_HEADS = 8
DIL_HD = BRANCH_W // DIL_HEADS
DIL_W = DIL_HEADS * DIL_HD
DIL_PATTERNS = ((128, 1), (512, 4), (2048, 16))
DIL_HALF = 64
DIL_TQ = 128
NEG_INF = -1e30

MERGE_W = N_BRANCH * D_MODEL
LOWRANK_OFF = GLA_QK_W * 2 + BRANCH_W * 2
LOWRANK_W = 2 * GLA_RANK
IN_COLS = LOWRANK_OFF + LOWRANK_W + 2 * (DIFF_QK_W * 2 + BRANCH_W * 2) + MERGE_W
MAIN_W = IN_COLS - LOWRANK_W

OFF_AQ = MERGE_W
OFF_AK = OFF_AQ + GLA_QK_W
OFF_AV = OFF_AK + GLA_QK_W
OFF_AG = OFF_AV + BRANCH_W
OFF_BQ = OFF_AG + BRANCH_W
OFF_BK = OFF_BQ + DIFF_QK_W
OFF_BV = OFF_BK + DIFF_QK_W
OFF_BG = OFF_BV + BRANCH_W
OFF_CQ = OFF_BG + BRANCH_W
OFF_CK = OFF_CQ + DIL_W
OFF_CV = OFF_CK + DIL_W
OFF_CG = OFF_CV + BRANCH_W

VMEM_LIMIT = 56 * 1024 * 1024


def _dot(a, b):
    return jnp.dot(a, b, preferred_element_type=F32)


def _dot_nt(a, b):
    return lax.dot_general(a, b, (((1,), (1,)), ((), ())), preferred_element_type=F32)


def _sigmoid(x):
    return 1.0 / (1.0 + jnp.exp(-x))


def _silu(x):
    return x * _sigmoid(x)


def _lane_iota(shape):
    return lax.broadcasted_iota(jnp.int32, shape, len(shape) - 1)


def _split_bf16(x):
    hi = x.astype(BF16).astype(F32)
    return hi, x - hi


def _rope(x, cos, sin_signed):
    lane = _lane_iota(x.shape)
    first_half = (lane % 64) < 32
    partner = jnp.where(first_half, pltpu.roll(x, 96, 1), pltpu.roll(x, 32, 1))
    return x * cos + partner * sin_signed


_PLAIN, _ROPE_Q, _ROPE_K = 0, 1, 2
_PROJ_SEGMENTS = (
    (0, OFF_BQ // 2, _PLAIN), (OFF_BQ // 2, OFF_BQ // 2, _PLAIN),
    (OFF_BQ, DIFF_QK_W, _ROPE_Q), (OFF_BK, DIFF_QK_W, _ROPE_K),
    (OFF_BV, 2 * BRANCH_W, _PLAIN),
    (OFF_CQ, DIL_W, _ROPE_Q), (OFF_CK, DIL_W, _ROPE_K),
    (OFF_CV, 2 * BRANCH_W, _PLAIN),
)
LOG2E = 1.4426950408889634
LN2 = 0.6931471805599453
QK_SCALE = DIFF_D ** -0.5 * LOG2E


def _in_proj_kernel(x_ref, g_ref, w_ref, wl_ref, gu_ref, gb_ref, cos_ref, sin_ref, out_ref, la_ref):
    x = x_ref[...]
    ms = jnp.mean(x * x, axis=-1, keepdims=True)
    hb = (x * lax.rsqrt(ms + NORM_EPS) * g_ref[...]).astype(BF16)
    low = _dot(hb, wl_ref[...])
    low_hi, low_lo = _split_bf16(low)
    lhs = jnp.where(_lane_iota(low.shape) < 2 * LOWRANK_W, low_hi, low_lo).astype(BF16)
    z = _dot(lhs, gu_ref[...]) + gb_ref[...]
    log_sig = jnp.minimum(z, 0.0) - jnp.log1p(jnp.exp(-jnp.abs(z)))
    la_ref[...] = log_sig * (1.0 / GLA_TAU)
    for start, width, kind in _PROJ_SEGMENTS:
        res = _dot(hb, w_ref[:, start:start + width])
        if kind != _PLAIN:
            cos, sin = cos_ref[...], sin_ref[...]
            tiles = [_rope(res[:, c:c + LANES], cos, sin) for c in range(0, width, LANES)]
            res = jnp.concatenate(tiles, axis=1)
            if kind == _ROPE_Q:
                res = res * QK_SCALE
        out_ref[:, start:start + width] = res.astype(BF16)


def _in_proj(x2, g, w_main, w_low, gate_up, gate_b, cos, sin, layer, seq, tm=512):
    assert DIFF_D == DIL_HD and seq % tm == 0
    rows = x2.shape[0]
    resident = dict(pipeline_mode=pl.Buffered(1))
    pos_blocks = seq // tm
    return pl.pallas_call(
        _in_proj_kernel,
        grid=(rows // tm,),
        in_specs=[
            pl.BlockSpec((tm, D_MODEL), lambda i: (i, 0)),
            pl.BlockSpec((None, 1, D_MODEL), lambda i: (layer, 0, 0)),
            pl.BlockSpec((None, D_MODEL, MAIN_W), lambda i: (layer, 0, 0), **resident),
            pl.BlockSpec((None, D_MODEL, LANES), lambda i: (layer, 0, 0), **resident),
            pl.BlockSpec((None, LANES, 4 * LANES), lambda i: (layer, 0, 0), **resident),
            pl.BlockSpec((None, 1, 4 * LANES), lambda i: (layer, 0, 0)),
            pl.BlockSpec((tm, LANES), lambda i: (i % pos_blocks, 0)),
            pl.BlockSpec((tm, LANES), lambda i: (i % pos_blocks, 0)),
        ],
        out_specs=[
            pl.BlockSpec((tm, MAIN_W), lambda i: (i, 0)),
            pl.BlockSpec((tm, 4 * LANES), lambda i: (i, 0)),
        ],
        out_shape=[
            jax.ShapeDtypeStruct((rows, MAIN_W), BF16),
            jax.ShapeDtypeStruct((rows, 4 * LANES), F32),
        ],
        compiler_params=pltpu.CompilerParams(
            dimension_semantics=("arbitrary",), vmem_limit_bytes=VMEM_LIMIT),
        name="in_proj",
    )(x2, g, w_main, w_low, gate_up, gate_b, cos, sin)


def _gla_kernel(q_ref, k_ref, v_ref, g_ref, la_ref, gn_ref, out_ref,
                u_sc, d_sc, st_sc, qh_sc, kh_sc, att_sc, *, seq, unroll):
    C = GLA_CHUNK
    n_chunks = seq // C
    lane_c = _lane_iota((C, LANES))
    row_c = lax.broadcasted_iota(jnp.int32, (C, LANES), 0)
    fwd_c = lane_c < GLA_DK
    tri = jnp.where(fwd_c, row_c - lane_c, lane_c - GLA_DK - row_c) >= 0
    tri_bf = tri.astype(F32).astype(BF16)
    tri2 = jnp.concatenate([tri_bf, tri_bf], axis=1)
    fwd_sq = _lane_iota((LANES, LANES)) < GLA_DK
    scale = GLA_DK ** -0.5

    def head_cols(hh):
        return slice(hh * LANES, (hh + 1) * LANES)

    def dup_heads(ref, rows):
        xx = ref[rows, :].astype(F32)
        rolled = pltpu.roll(xx, GLA_DK, 1)
        return jnp.where(fwd_c, xx, rolled), jnp.where(fwd_c, rolled, xx)

    def prep(n, carry):
        rows = pl.ds(pl.multiple_of(n * C, C), C)
        q2 = dup_heads(q_ref, rows)
        k2 = dup_heads(k_ref, rows)
        for hh in range(2):
            cols = head_cols(hh)
            la = la_ref[rows, cols]
            la_bd = jnp.concatenate([jnp.where(fwd_c, la, 0.0), jnp.where(fwd_c, 0.0, la)], axis=0)
            la_hi, la_lo = _split_bf16(la_bd)
            b = _dot(tri2, jnp.concatenate([la_hi, la_lo], axis=0).astype(BF16))
            tot = jnp.sum(la, axis=0, keepdims=True)
            qh_sc[hh, rows, :] = (q2[hh] * jnp.exp(b) * scale).astype(BF16)
            k_hat = k2[hh] * jnp.exp(-b)
            kh_sc[hh, n] = jnp.concatenate(
                [jnp.where(fwd_c, k_hat, 0.0), jnp.where(fwd_c, 0.0, k_hat)], axis=0).astype(BF16)
            k_end = (k2[hh] * jnp.exp(tot - b)).astype(BF16)
            v_t = v_ref[rows, cols].astype(F32).T.astype(BF16)
            u_sc[hh, n] = _dot(v_t, k_end)
            d_sc[hh, n] = jnp.broadcast_to(jnp.exp(tot), (8, LANES))
        return carry

    lax.fori_loop(0, n_chunks, prep, 0, unroll=unroll)

    def scan(i, states):
        j = n_chunks - 1 - i
        new = []
        for hh in range(2):
            st_sc[hh, i] = states[hh]
            upd = jnp.where(fwd_sq, u_sc[hh, i], u_sc[hh, j])
            dec = jnp.where(fwd_sq[:8], d_sc[hh, i], d_sc[hh, j])[0:1]
            new.append(states[hh] * dec + upd)
        return tuple(new)

    zero = jnp.zeros((LANES, LANES), F32)
    lax.fori_loop(0, n_chunks, scan, (zero, zero))

    def attend(n, carry):
        rows = pl.ds(pl.multiple_of(n * C, C), C)
        for hh in range(2):
            att = _dot_nt(qh_sc[hh, rows, :], kh_sc[hh, n])
            att_sc[hh, rows, :] = jnp.where(tri, att, 0.0).astype(BF16)
        return carry

    lax.fori_loop(0, n_chunks, attend, 0, unroll=unroll)

    def emit(n, carry):
        rows = pl.ds(pl.multiple_of(n * C, C), C)
        for hh in range(2):
            cols = head_cols(hh)
            v = v_ref[rows, cols]
            o = _dot(att_sc[hh, rows, :], jnp.concatenate([v, v], axis=0))
            state = jnp.where(fwd_sq, st_sc[hh, n], st_sc[hh, n_chunks - 1 - n]).astype(BF16)
            o = o + _dot_nt(qh_sc[hh, rows, :], state)
            ms = jnp.mean(o * o, axis=-1, keepdims=True)
            y = o * lax.rsqrt(ms + NORM_EPS) * gn_ref[...]
            out_ref[rows, cols] = (y * _silu(g_ref[rows, cols].astype(F32))).astype(BF16)
        return carry

    lax.fori_loop(0, n_chunks, emit, 0, unroll=unroll)


def _gla(proj, la, gn, layer, batch, seq, unroll=16):
    n_chunks = seq // GLA_CHUNK
    pair = 2 * LANES
    return pl.pallas_call(
        functools.partial(_gla_kernel, seq=seq, unroll=unroll),
        grid=(batch, GLA_HEADS // 2),
        in_specs=[
            pl.BlockSpec((seq, LANES), lambda b, p: (b, OFF_AQ // LANES + p)),
            pl.BlockSpec((seq, LANES), lambda b, p: (b, OFF_AK // LANES + p)),
            pl.BlockSpec((seq, pair), lambda b, p: (b, OFF_AV // pair + p)),
            pl.BlockSpec((seq, pair), lambda b, p: (b, OFF_AG // pair + p)),
            pl.BlockSpec((seq, pair), lambda b, p: (b, p)),
            pl.BlockSpec((None, 1, LANES), lambda b, p: (layer, 0, 0)),
        ],
        out_specs=pl.BlockSpec((seq, pair), lambda b, p: (b, p)),
        out_shape=jax.ShapeDtypeStruct((batch * seq, BRANCH_W), BF16),
        scratch_shapes=[
            pltpu.VMEM((2, n_chunks, LANES, LANES), F32),
            pltpu.VMEM((2, n_chunks, 8, LANES), F32),
            pltpu.VMEM((2, n_chunks, LANES, LANES), F32),
            pltpu.VMEM((2, seq, LANES), BF16),
            pltpu.VMEM((2, n_chunks, LANES, LANES), BF16),
            pltpu.VMEM((2, seq, LANES), BF16),
        ],
        compiler_params=pltpu.CompilerParams(
            dimension_semantics=("arbitrary", "arbitrary"), vmem_limit_bytes=VMEM_LIMIT),
        name="gla",
    )(proj, proj, proj, proj, la, gn)


def _diff_kernel(q_ref, k_ref, v_ref, g_ref, dl_ref, li_ref, gn_ref, out_ref,
                 q_sc, s0_sc, s1_sc, pm0_sc, pm1_sc, *, seq, tq):
    n_blk = seq // tq
    half0 = _lane_iota((tq, LANES)) < DIFF_D

    def stack_q(i, carry):
        q = q_ref[pl.ds(pl.multiple_of(i * tq, tq), tq), :].astype(F32)
        q_sc[i] = jnp.concatenate([jnp.where(half0, q, 0.0), jnp.where(half0, 0.0, q)], axis=0).astype(BF16)
        return carry

    lax.fori_loop(0, n_blk, stack_q, 0)

    dl = dl_ref[...]
    lam_init = li_ref[...]
    lam = (jnp.exp(jnp.sum(dl[0:1] * dl[1:2], axis=-1, keepdims=True))
           - jnp.exp(jnp.sum(dl[2:3] * dl[3:4], axis=-1, keepdims=True)) + lam_init)

    def scores(i, s_sc, pm_sc):
        s = _dot_nt(q_sc[i], k_ref[...])
        s_sc[...] = s
        pm_sc[...] = functools.reduce(jnp.maximum, [s[:, c:c + LANES] for c in range(0, seq, LANES)])

    def finish(i, s_sc, pm_sc):
        rows = pl.ds(i * tq, tq)
        m = jnp.max(pm_sc[...], axis=-1, keepdims=True)
        e = jnp.exp2(s_sc[...] - m)
        l = jnp.sum(e, axis=-1, keepdims=True)
        o2 = _dot(e.astype(BF16), v_ref[...]) * (1.0 / l)
        o = o2[:tq] - lam * o2[tq:]
        ms = jnp.mean(o * o, axis=-1, keepdims=True)
        y = o * lax.rsqrt(ms + NORM_EPS) * gn_ref[...] * (1.0 - lam_init)
        out_ref[rows, :] = (y * _silu(g_ref[rows, :].astype(F32))).astype(BF16)

    bufs = ((s0_sc, pm0_sc), (s1_sc, pm1_sc))
    scores(0, *bufs[0])
    for i in range(n_blk):
        if i + 1 < n_blk:
            scores(i + 1, *bufs[(i + 1) % 2])
        finish(i, *bufs[i % 2])


def _diff(proj, dl, lam_init, gn, layer, batch, seq, tq=256):
    blk = lambda off: pl.BlockSpec((seq, LANES), lambda b, h: (b, off // LANES + h))
    per_layer = lambda shape: pl.BlockSpec((None,) + shape, lambda b, h: (layer, 0, 0))
    return pl.pallas_call(
        functools.partial(_diff_kernel, seq=seq, tq=tq),
        grid=(batch, DIFF_HEADS),
        in_specs=[blk(OFF_BQ), blk(OFF_BK), blk(OFF_BV), blk(OFF_BG),
                  per_layer((4, DIFF_D)), per_layer((1, 1)), per_layer((1, LANES))],
        out_specs=pl.BlockSpec((seq, LANES), lambda b, h: (b, h)),
        out_shape=jax.ShapeDtypeStruct((batch * seq, BRANCH_W), BF16),
        scratch_shapes=[pltpu.VMEM((seq // tq, 2 * tq, LANES), BF16),
                        pltpu.VMEM((2 * tq, seq), F32), pltpu.VMEM((2 * tq, seq), F32),
                        pltpu.VMEM((2 * tq, LANES), F32), pltpu.VMEM((2 * tq, LANES), F32)],
        compiler_params=pltpu.CompilerParams(
            dimension_semantics=("arbitrary", "arbitrary"), vmem_limit_bytes=VMEM_LIMIT),
        name="diff",
    )(proj, proj, proj, proj, dl, lam_init, gn)


def _dilated_kernel(q_ref, k_ref, v_ref, g_ref, out_ref,
                    nat_sc, c4_sc, k4_sc, q16_sc, k16_sc, w1_sc, w4_sc, w16_sc,
                    s0_sc, s1_sc, s2_sc, e0_sc, e1_sc, e2_sc, m0_sc, m1_sc, m2_sc,
                    o_sc, l_sc, bias_sc, *, seq, unroll):
    tq, win = DIL_TQ, 2 * DIL_TQ
    stage_rows = 2 * tq
    n_stage = seq // stage_rows
    len4, len16 = seq // 4, seq // 16
    head0 = _lane_iota((tq, LANES)) < DIL_HD
    head0_stage = _lane_iota((stage_rows, LANES)) < DIL_HD

    dcol = (lax.broadcasted_iota(jnp.int32, (tq, win), 1) - lax.broadcasted_iota(jnp.int32, (tq, win), 0))
    for idx in range(3):
        bias_sc[idx] = jnp.where(jnp.abs(dcol - idx * DIL_HALF) <= DIL_HALF, 0.0, NEG_INF)

    def stage(i, carry):
        rows = pl.ds(pl.multiple_of(i * stage_rows, stage_rows), stage_rows)
        nat_sc[0, rows, :] = q_ref[rows, :].astype(F32)
        nat_sc[1, rows, :] = k_ref[rows, :].astype(F32)
        nat_sc[2, rows, :] = v_ref[rows, :].astype(F32)
        return carry

    lax.fori_loop(0, n_stage, stage, 0)

    for t in range(3):
        for r in range(4):
            c4_sc[t, r * len4:(r + 1) * len4, :] = nat_sc[t, pl.ds(r, len4, stride=4), :]
    for r4 in range(4):
        for j in range(4):
            dst = slice((r4 + 4 * j) * len16, (r4 + 4 * j + 1) * len16)
            src = pl.ds(r4 * len4 + j, len16, stride=4)
            q16_sc[dst, :] = c4_sc[0, src, :]
            k16_sc[dst, :] = c4_sc[1, src, :].astype(BF16)
            w16_sc[0, dst, :] = c4_sc[2, src, :].astype(BF16)

    def value_weights(i, carry):
        rows = pl.ds(pl.multiple_of(i * stage_rows, stage_rows), stage_rows)
        k4_sc[rows, :] = c4_sc[1, rows, :].astype(BF16)
        for src, dst in ((v_ref, w1_sc), (c4_sc.at[2], w4_sc), (w16_sc.at[0], w16_sc)):
            v = src[rows, :].astype(F32)
            dst[1, rows, :] = jnp.where(head0_stage, 1.0, v).astype(BF16)
            dst[0, rows, :] = jnp.where(head0_stage, v, 1.0).astype(BF16)
        return carry

    lax.fori_loop(0, n_stage, value_weights, 0)

    def make_pattern(g, q_src, k_src, w_src, s_sc, e_sc, m_sc, seq_len, w, out_rows):
        per_seq = seq_len // tq

        def where(t):
            r, i = t // per_seq, t % per_seq
            ks = jnp.clip(i * tq - tq // 2, 0, seq_len - w)
            which = jnp.where(i == 0, 0, jnp.where(i == per_seq - 1, 2, 1)) if per_seq > 1 else 0
            qrows = pl.ds(pl.multiple_of(r * seq_len + i * tq, tq), tq)
            krows = pl.ds(pl.multiple_of(r * seq_len + ks, tq // 2), w)
            return qrows, krows, which

        def scores(t):
            qrows, krows, which = where(t)
            q = q_src[qrows, :]
            q2 = jnp.concatenate([jnp.where(head0, q, 0.0), jnp.where(head0, 0.0, q)], axis=0)
            bias = bias_sc[which, :, :w]
            s = _dot_nt(q2.astype(BF16), k_src[krows, :])
            s_sc[t, :, :w] = s + jnp.concatenate([bias, bias], axis=0)

        def softmax(t):
            s = s_sc[t, :, :w]
            m = jnp.max(s, axis=-1, keepdims=True)
            e_sc[t, :, :w] = jnp.exp2(s - m).astype(BF16)
            m_sc[t] = jnp.where(head0, m[:tq], m[tq:])

        def values(t):
            _, krows, _ = where(t)
            out0 = _dot(e_sc[t, :tq, :w], w_src[0, krows, :])
            out1 = _dot(e_sc[t, tq:, :w], w_src[1, krows, :])
            l = pltpu.roll(jnp.where(head0, out1, out0), DIL_HD, 1)
            rows = out_rows(t)
            o_sc[g, rows, :] = jnp.where(head0, out0, out1) * (1.0 / l)
            l_sc[g, rows, :] = m_sc[t] * LN2 + jnp.log(l)

        return scores, softmax, values

    sc0, sm0, va0 = make_pattern(0, nat_sc.at[0], k_ref, w1_sc, s0_sc, e0_sc, m0_sc, seq, win,
                                 lambda t: pl.ds(pl.multiple_of(t * tq, tq), tq))
    sc1, sm1, va1 = make_pattern(1, c4_sc.at[0], k4_sc, w4_sc, s1_sc, e1_sc, m1_sc, len4, win,
                                 lambda t: pl.ds(pl.multiple_of(t * tq, tq), tq))
    sc2, sm2, va2 = make_pattern(2, q16_sc, k16_sc, w16_sc, s2_sc, e2_sc, m2_sc, len16, len16,
                                 lambda t: pl.ds((t % 4) * len4 + t // 4, len16, stride=4))

    for passes in ((sc0,), (sm0, sc1), (va0, sm1, sc2), (va1, sm2), (va2,)):
        def body(t, carry, passes=passes):
            for f in passes:
                f(t)
            return carry

        lax.fori_loop(0, seq // tq, body, 0, unroll=unroll)

    y_sc = nat_sc.at[0]
    per_class = len4 // stage_rows

    def combine(i, carry):
        r, m0 = i // per_class, (i % per_class) * stage_rows
        rows = pl.ds(pl.multiple_of(r * len4 + m0, stage_rows), stage_rows)
        nat_rows = pl.ds(r + 4 * m0, stage_rows, stride=4)
        l0, l1, l2 = l_sc[0, nat_rows, :], l_sc[1, rows, :], l_sc[2, rows, :]
        m = jnp.maximum(jnp.maximum(l0, l1), l2)
        w0, w1, w2 = jnp.exp(l0 - m), jnp.exp(l1 - m), jnp.exp(l2 - m)
        o = w0 * o_sc[0, nat_rows, :] + w1 * o_sc[1, rows, :] + w2 * o_sc[2, rows, :]
        y_sc[nat_rows, :] = o * (1.0 / (w0 + w1 + w2))
        return carry

    lax.fori_loop(0, n_stage, combine, 0)

    def gate(i, carry):
        rows = pl.ds(pl.multiple_of(i * stage_rows, stage_rows), stage_rows)
        out_ref[rows, :] = (y_sc[rows, :] * _silu(g_ref[rows, :].astype(F32))).astype(BF16)
        return carry

    lax.fori_loop(0, n_stage, gate, 0)


def _dilated(proj, batch, seq, unroll=8):
    assert seq // 16 == DIL_TQ and all(w // (2 * d) == DIL_HALF for w, d in DIL_PATTERNS)
    blk = lambda off: pl.BlockSpec((seq, LANES), lambda b, p: (b, off // LANES + p))
    n_blocks = seq // DIL_TQ
    f32_rows = pltpu.VMEM((seq, LANES), F32)
    bf16_rows = pltpu.VMEM((seq, LANES), BF16)
    bf16_pair = pltpu.VMEM((2, seq, LANES), BF16)
    s_buf = pltpu.VMEM((n_blocks, 2 * DIL_TQ, 2 * DIL_TQ), F32)
    e_buf = pltpu.VMEM((n_blocks, 2 * DIL_TQ, 2 * DIL_TQ), BF16)
    m_buf = pltpu.VMEM((n_blocks, DIL_TQ, LANES), F32)
    return pl.pallas_call(
        functools.partial(_dilated_kernel, seq=seq, unroll=unroll),
        grid=(batch, DIL_HEADS // 2),
        in_specs=[blk(OFF_CQ), blk(OFF_CK), blk(OFF_CV), blk(OFF_CG)],
        out_specs=pl.BlockSpec((seq, LANES), lambda b, p: (b, p)),
        out_shape=jax.ShapeDtypeStruct((batch * seq, BRANCH_W), BF16),
        scratch_shapes=[
            pltpu.VMEM((3, seq, LANES), F32),
            pltpu.VMEM((3, seq, LANES), F32), bf16_rows, f32_rows, bf16_rows,
            bf16_pair, bf16_pair, bf16_pair,
            s_buf, s_buf, s_buf, e_buf, e_buf, e_buf, m_buf, m_buf, m_buf,
            pltpu.VMEM((3, seq, LANES), F32), pltpu.VMEM((3, seq, LANES), F32),
            pltpu.VMEM((3, DIL_TQ, 2 * DIL_TQ), F32),
        ],
        compiler_params=pltpu.CompilerParams(
            dimension_semantics=("arbitrary", "arbitrary"), vmem_limit_bytes=VMEM_LIMIT),
        name="dilated",
    )(proj, proj, proj, proj)


def _merge_kernel(x_ref, ya_ref, yb_ref, yc_ref, m_ref, wb_ref, wo_ref, fg_ref, out_ref, *, final):
    mixed = None
    for gi, y_ref in enumerate((ya_ref, yb_ref, yc_ref)):
        z = _dot(y_ref[...], wb_ref[gi])
        gate = 0.5 * jnp.tanh(0.5 * m_ref[:, gi * D_MODEL:(gi + 1) * D_MODEL].astype(F32)) + 0.5
        mixed = gate * z if mixed is None else mixed + gate * z
    x = x_ref[...] + _dot(mixed.astype(BF16), wo_ref[...])
    if final:
        ms = jnp.mean(x * x, axis=-1, keepdims=True)
        x = x * lax.rsqrt(ms + NORM_EPS) * fg_ref[...]
    out_ref[...] = x


def _merge(x2, ya, yb, yc, proj, wb, wo, final_g, layer, final, tm=512):
    rows = x2.shape[0]
    yspec = pl.BlockSpec((tm, BRANCH_W), lambda i: (i, 0))
    return pl.pallas_call(
        functools.partial(_merge_kernel, final=final),
        grid=(rows // tm,),
        in_specs=[
            pl.BlockSpec((tm, D_MODEL), lambda i: (i, 0)),
            yspec, yspec, yspec,
            pl.BlockSpec((tm, MERGE_W), lambda i: (i, 0)),
            pl.BlockSpec((None, N_BRANCH, BRANCH_W, D_MODEL), lambda i: (layer, 0, 0, 0)),
            pl.BlockSpec((None, D_MODEL, D_MODEL), lambda i: (layer, 0, 0)),
            pl.BlockSpec((1, D_MODEL), lambda i: (0, 0)),
        ],
        out_specs=pl.BlockSpec((tm, D_MODEL), lambda i: (i, 0)),
        out_shape=jax.ShapeDtypeStruct((rows, D_MODEL), F32),
        compiler_params=pltpu.CompilerParams(
            dimension_semantics=("arbitrary",), vmem_limit_bytes=VMEM_LIMIT),
        name="merge",
    )(x2, ya, yb, yc, proj, wb, wo, final_g)


def _rope_tables(seq, dim):
    inv = 1.0 / (ROPE_THETA ** (jnp.arange(0, dim, 2, dtype=F32) / dim))
    ang = jnp.arange(seq, dtype=F32)[:, None] * inv[None, :]
    cos, sin = jnp.cos(ang), jnp.sin(ang)
    reps = LANES // dim
    cos_t = jnp.tile(jnp.concatenate([cos, cos], axis=-1), (1, reps))
    sin_t = jnp.tile(jnp.concatenate([-sin, sin], axis=-1), (1, reps))
    return cos_t, sin_t


def kernel(x, norm_g, w_in, gla_gate_up, gla_gate_b, gla_norm_g, diff_lambda, diff_norm_g, w_branch, w_out, final_norm_g):
    batch, seq, _ = x.shape
    depth = w_in.shape[0]
    cos, sin = _rope_tables(seq, DIFF_D)

    lo, hi = LOWRANK_OFF, LOWRANK_OFF + LOWRANK_W
    merge_off = IN_COLS - MERGE_W
    w_bf = w_in.astype(BF16)
    w_main = jnp.concatenate([w_bf[:, :, merge_off:], w_bf[:, :, :lo], w_bf[:, :, hi:merge_off]], axis=-1)
    w_lr = w_bf[:, :, lo:hi]
    w_low = jnp.concatenate([w_lr, w_lr, w_lr, jnp.zeros_like(w_lr)], axis=-1)
    gu = gla_gate_up.reshape(depth, 2, GLA_RANK, GLA_HEADS, GLA_DK)
    zeros = jnp.zeros_like(gu[:, 0])
    gu_f = jnp.concatenate([gu[:, 0], zeros], axis=-1).reshape(depth, GLA_RANK, GLA_HEADS * LANES)
    gu_b = jnp.concatenate([zeros, gu[:, 1]], axis=-1).reshape(depth, GLA_RANK, GLA_HEADS * LANES)
    gu_cat = jnp.concatenate([gu_f, gu_b], axis=1)
    gu_hi = gu_cat.astype(BF16)
    gu_lo = (gu_cat - gu_hi.astype(F32)).astype(BF16)
    gate_up = jnp.concatenate([gu_hi, gu_lo, gu_hi, jnp.zeros_like(gu_hi)], axis=1)
    gb = gla_gate_b.reshape(depth, 2, GLA_HEADS, GLA_DK)
    gate_b = jnp.concatenate([gb[:, 0], gb[:, 1]], axis=-1).reshape(depth, 1, GLA_HEADS * LANES)
    wb = w_branch.astype(BF16)
    wo = w_out.astype(BF16)
    lam_init = jnp.asarray([0.8 - 0.6 * math.exp(-0.3 * layer) for layer in range(depth)],
                           F32).reshape(depth, 1, 1)
    norm_g3 = norm_g[:, None, :]
    gla_norm_g3 = gla_norm_g[:, None, :]
    diff_norm_g3 = diff_norm_g[:, None, :]

    x2 = x.reshape(batch * seq, D_MODEL)
    for layer in range(depth):
        proj, la = _in_proj(x2, norm_g3, w_main, w_low, gate_up, gate_b, cos, sin, layer, seq)
        ya = _gla(proj, la, gla_norm_g3, layer, batch, seq)
        yb = _diff(proj, diff_lambda, lam_init, diff_norm_g3, layer, batch, seq)
        yc = _dilated(proj, batch, seq)
        x2 = _merge(x2, ya, yb, yc, proj, wb, wo, final_norm_g[None], layer, final=layer == depth - 1)
    return x2.reshape(batch, seq, D_MODEL)
```

```python
import functools
import math

import jax
import jax.numpy as jnp
from jax import lax
from jax.experimental import pallas as pl
from jax.experimental.pallas import tpu as pltpu

F32 = jnp.float32
BF16 = jnp.bfloat16

LANES = 128

D_MODEL = 1024
ROPE_THETA = 10000.0
NORM_EPS = 1e-6
BRANCH_W = D_MODEL // 2
N_BRANCH = 3

GLA_HEADS = 4
GLA_DV = BRANCH_W // GLA_HEADS
GLA_DK = GLA_DV // 2
GLA_RANK = 16
GLA_TAU = 16.0
GLA_CHUNK = 64
GLA_QK_W = GLA_HEADS * GLA_DK

DIFF_HEADS = 4
DIFF_D = BRANCH_W // (2 * DIFF_HEADS)
DIFF_QK_W = DIFF_HEADS * 2 * DIFF_D

DIL_HEADS = 8
DIL_HD = BRANCH_W // DIL_HEADS
DIL_W = DIL_HEADS * DIL_HD
DIL_PATTERNS = ((128, 1), (512, 4), (2048, 16))
DIL_HALF = 64
DIL_TQ = 128
NEG_INF = -1e30

MERGE_W = N_BRANCH * D_MODEL
LOWRANK_OFF = GLA_QK_W * 2 + BRANCH_W * 2
LOWRANK_W = 2 * GLA_RANK
IN_COLS = LOWRANK_OFF + LOWRANK_W + 2 * (DIFF_QK_W * 2 + BRANCH_W * 2) + MERGE_W
MAIN_W = IN_COLS - LOWRANK_W

OFF_AQ = MERGE_W
OFF_AK = OFF_AQ + GLA_QK_W
OFF_AV = OFF_AK + GLA_QK_W
OFF_AG = OFF_AV + BRANCH_W
OFF_BQ = OFF_AG + BRANCH_W
OFF_BK = OFF_BQ + DIFF_QK_W
OFF_BV = OFF_BK + DIFF_QK_W
OFF_BG = OFF_BV + BRANCH_W
OFF_CQ = OFF_BG + BRANCH_W
OFF_CK = OFF_CQ + DIL_W
OFF_CV = OFF_CK + DIL_W
OFF_CG = OFF_CV + BRANCH_W

VMEM_LIMIT = 56 * 1024 * 1024


def _dot(a, b):
    return jnp.dot(a, b, preferred_element_type=F32)


def _dot_nt(a, b):
    return lax.dot_general(a, b, (((1,), (1,)), ((), ())), preferred_element_type=F32)


def _sigmoid(x):
    return 1.0 / (1.0 + jnp.exp(-x))


def _silu(x):
    return x * _sigmoid(x)


def _lane_iota(shape):
    return lax.broadcasted_iota(jnp.int32, shape, len(shape) - 1)


def _split_bf16(x):
    hi = x.astype(BF16).astype(F32)
    return hi, x - hi


def _rope(x, cos, sin_signed):
    lane = _lane_iota(x.shape)
    first_half = (lane % 64) < 32
    partner = jnp.where(first_half, pltpu.roll(x, 96, 1), pltpu.roll(x, 32, 1))
    return x * cos + partner * sin_signed


_PLAIN, _ROPE_Q, _ROPE_K = 0, 1, 2
_PROJ_SEGMENTS = (
    (0, OFF_BQ // 2, _PLAIN), (OFF_BQ // 2, OFF_BQ // 2, _PLAIN),
    (OFF_BQ, DIFF_QK_W, _ROPE_Q), (OFF_BK, DIFF_QK_W, _ROPE_K),
    (OFF_BV, 2 * BRANCH_W, _PLAIN),
    (OFF_CQ, DIL_W, _ROPE_Q), (OFF_CK, DIL_W, _ROPE_K),
    (OFF_CV, 2 * BRANCH_W, _PLAIN),
)
LOG2E = 1.4426950408889634
LN2 = 0.6931471805599453
QK_SCALE = DIFF_D ** -0.5 * LOG2E


def _in_proj_kernel(x_ref, g_ref, w_ref, wl_ref, gu_ref, gb_ref, cos_ref, sin_ref, out_ref, la_ref):
    x = x_ref[...]
    ms = jnp.mean(x * x, axis=-1, keepdims=True)
    hb = (x * lax.rsqrt(ms + NORM_EPS) * g_ref[...]).astype(BF16)
    low = _dot(hb, wl_ref[...])
    low_hi, low_lo = _split_bf16(low)
    lhs = jnp.where(_lane_iota(low.shape) < 2 * LOWRANK_W, low_hi, low_lo).astype(BF16)
    z = _dot(lhs, gu_ref[...]) + gb_ref[...]
    log_sig = jnp.minimum(z, 0.0) - jnp.log1p(jnp.exp(-jnp.abs(z)))
    la_ref[...] = log_sig * (1.0 / GLA_TAU)
    for start, width, kind in _PROJ_SEGMENTS:
        res = _dot(hb, w_ref[:, start:start + width])
        if kind != _PLAIN:
            cos, sin = cos_ref[...], sin_ref[...]
            tiles = [_rope(res[:, c:c + LANES], cos, sin) for c in range(0, width, LANES)]
            res = jnp.concatenate(tiles, axis=1)
            if kind == _ROPE_Q:
                res = res * QK_SCALE
        out_ref[:, start:start + width] = res.astype(BF16)


def _in_proj(x2, g, w_main, w_low, gate_up, gate_b, cos, sin, layer, seq, tm=512):
    assert DIFF_D == DIL_HD and seq % tm == 0
    rows = x2.shape[0]
    resident = dict(pipeline_mode=pl.Buffered(1))
    pos_blocks = seq // tm
    return pl.pallas_call(
        _in_proj_kernel,
        grid=(rows // tm,),
        in_specs=[
            pl.BlockSpec((tm, D_MODEL), lambda i: (i, 0)),
            pl.BlockSpec((None, 1, D_MODEL), lambda i: (layer, 0, 0)),
            pl.BlockSpec((None, D_MODEL, MAIN_W), lambda i: (layer, 0, 0), **resident),
            pl.BlockSpec((None, D_MODEL, LANES), lambda i: (layer, 0, 0), **resident),
            pl.BlockSpec((None, LANES, 4 * LANES), lambda i: (layer, 0, 0), **resident),
            pl.BlockSpec((None, 1, 4 * LANES), lambda i: (layer, 0, 0)),
            pl.BlockSpec((tm, LANES), lambda i: (i % pos_blocks, 0)),
            pl.BlockSpec((tm, LANES), lambda i: (i % pos_blocks, 0)),
        ],
        out_specs=[
            pl.BlockSpec((tm, MAIN_W), lambda i: (i, 0)),
            pl.BlockSpec((tm, 4 * LANES), lambda i: (i, 0)),
        ],
        out_shape=[
            jax.ShapeDtypeStruct((rows, MAIN_W), BF16),
            jax.ShapeDtypeStruct((rows, 4 * LANES), F32),
        ],
        compiler_params=pltpu.CompilerParams(
            dimension_semantics=("arbitrary",), vmem_limit_bytes=VMEM_LIMIT),
        name="in_proj",
    )(x2, g, w_main, w_low, gate_up, gate_b, cos, sin)


def _gla_kernel(q_ref, k_ref, v_ref, g_ref, la_ref, gn_ref, out_ref,
                u_sc, d_sc, st_sc, qh_sc, kh_sc, att_sc, *, seq, unroll):
    C = GLA_CHUNK
    n_chunks = seq // C
    lane_c = _lane_iota((C, LANES))
    row_c = lax.broadcasted_iota(jnp.int32, (C, LANES), 0)
    fwd_c = lane_c < GLA_DK
    tri = jnp.where(fwd_c, row_c - lane_c, lane_c - GLA_DK - row_c) >= 0
    tri_bf = tri.astype(F32).astype(BF16)
    tri2 = jnp.concatenate([tri_bf, tri_bf], axis=1)
    fwd_sq = _lane_iota((LANES, LANES)) < GLA_DK
    scale = GLA_DK ** -0.5

    def head_cols(hh):
        return slice(hh * LANES, (hh + 1) * LANES)

    def dup_heads(ref, rows):
        xx = ref[rows, :].astype(F32)
        rolled = pltpu.roll(xx, GLA_DK, 1)
        return jnp.where(fwd_c, xx, rolled), jnp.where(fwd_c, rolled, xx)

    def prep(n, carry):
        rows = pl.ds(pl.multiple_of(n * C, C), C)
        q2 = dup_heads(q_ref, rows)
        k2 = dup_heads(k_ref, rows)
        for hh in range(2):
            cols = head_cols(hh)
            la = la_ref[rows, cols]
            la_bd = jnp.concatenate([jnp.where(fwd_c, la, 0.0), jnp.where(fwd_c, 0.0, la)], axis=0)
            la_hi, la_lo = _split_bf16(la_bd)
            b = _dot(tri2, jnp.concatenate([la_hi, la_lo], axis=0).astype(BF16))
            tot = jnp.sum(la, axis=0, keepdims=True)
            qh_sc[hh, rows, :] = (q2[hh] * jnp.exp(b) * scale).astype(BF16)
            k_hat = k2[hh] * jnp.exp(-b)
            kh_sc[hh, n] = jnp.concatenate(
                [jnp.where(fwd_c, k_hat, 0.0), jnp.where(fwd_c, 0.0, k_hat)], axis=0).astype(BF16)
            k_end = (k2[hh] * jnp.exp(tot - b)).astype(BF16)
            v_t = v_ref[rows, cols].astype(F32).T.astype(BF16)
            u_sc[hh, n] = _dot(v_t, k_end)
            d_sc[hh, n] = jnp.broadcast_to(jnp.exp(tot), (8, LANES))
        return carry

    lax.fori_loop(0, n_chunks, prep, 0, unroll=unroll)

    def scan(i, states):
        j = n_chunks - 1 - i
        new = []
        for hh in range(2):
            st_sc[hh, i] = states[hh]
            upd = jnp.where(fwd_sq, u_sc[hh, i], u_sc[hh, j])
            dec = jnp.where(fwd_sq[:8], d_sc[hh, i], d_sc[hh, j])[0:1]
            new.append(states[hh] * dec + upd)
        return tuple(new)

    zero = jnp.zeros((LANES, LANES), F32)
    lax.fori_loop(0, n_chunks, scan, (zero, zero))

    def attend(n, carry):
        rows = pl.ds(pl.multiple_of(n * C, C), C)
        for hh in range(2):
            att = _dot_nt(qh_sc[hh, rows, :], kh_sc[hh, n])
            att_sc[hh, rows, :] = jnp.where(tri, att, 0.0).astype(BF16)
        return carry

    lax.fori_loop(0, n_chunks, attend, 0, unroll=unroll)

    def emit(n, carry):
        rows = pl.ds(pl.multiple_of(n * C, C), C)
        for hh in range(2):
            cols = head_cols(hh)
            v = v_ref[rows, cols]
            o = _dot(att_sc[hh, rows, :], jnp.concatenate([v, v], axis=0))
            state = jnp.where(fwd_sq, st_sc[hh, n], st_sc[hh, n_chunks - 1 - n]).astype(BF16)
            o = o + _dot_nt(qh_sc[hh, rows, :], state)
            ms = jnp.mean(o * o, axis=-1, keepdims=True)
            y = o * lax.rsqrt(ms + NORM_EPS) * gn_ref[...]
            out_ref[rows, cols] = (y * _silu(g_ref[rows, cols].astype(F32))).astype(BF16)
        return carry

    lax.fori_loop(0, n_chunks, emit, 0, unroll=unroll)


def _gla(proj, la, gn, layer, batch, seq, unroll=32):
    n_chunks = seq // GLA_CHUNK
    pair = 2 * LANES
    return pl.pallas_call(
        functools.partial(_gla_kernel, seq=seq, unroll=unroll),
        grid=(batch, GLA_HEADS // 2),
        in_specs=[
            pl.BlockSpec((seq, LANES), lambda b, p: (b, OFF_AQ // LANES + p)),
            pl.BlockSpec((seq, LANES), lambda b, p: (b, OFF_AK // LANES + p)),
            pl.BlockSpec((seq, pair), lambda b, p: (b, OFF_AV // pair + p)),
            pl.BlockSpec((seq, pair), lambda b, p: (b, OFF_AG // pair + p)),
            pl.BlockSpec((seq, pair), lambda b, p: (b, p)),
            pl.BlockSpec((None, 1, LANES), lambda b, p: (layer, 0, 0)),
        ],
        out_specs=pl.BlockSpec((seq, pair), lambda b, p: (b, p)),
        out_shape=jax.ShapeDtypeStruct((batch * seq, BRANCH_W), BF16),
        scratch_shapes=[
            pltpu.VMEM((2, n_chunks, LANES, LANES), F32),
            pltpu.VMEM((2, n_chunks, 8, LANES), F32),
            pltpu.VMEM((2, n_chunks, LANES, LANES), F32),
            pltpu.VMEM((2, seq, LANES), BF16),
            pltpu.VMEM((2, n_chunks, LANES, LANES), BF16),
            pltpu.VMEM((2, seq, LANES), BF16),
        ],
        compiler_params=pltpu.CompilerParams(
            dimension_semantics=("arbitrary", "arbitrary"), vmem_limit_bytes=VMEM_LIMIT),
        name="gla",
    )(proj, proj, proj, proj, la, gn)


def _diff_kernel(q_ref, k_ref, v_ref, g_ref, dl_ref, li_ref, gn_ref, out_ref,
                 q_sc, s0_sc, s1_sc, pm0_sc, pm1_sc, *, seq, tq):
    n_blk = seq // tq
    half0 = _lane_iota((tq, LANES)) < DIFF_D

    def stack_q(i, carry):
        q = q_ref[pl.ds(pl.multiple_of(i * tq, tq), tq), :].astype(F32)
        q_sc[i] = jnp.concatenate([jnp.where(half0, q, 0.0), jnp.where(half0, 0.0, q)], axis=0).astype(BF16)
        return carry

    lax.fori_loop(0, n_blk, stack_q, 0)

    dl = dl_ref[...]
    lam_init = li_ref[...]
    lam = (jnp.exp(jnp.sum(dl[0:1] * dl[1:2], axis=-1, keepdims=True))
           - jnp.exp(jnp.sum(dl[2:3] * dl[3:4], axis=-1, keepdims=True)) + lam_init)

    def scores(i, s_sc, pm_sc):
        s = _dot_nt(q_sc[i], k_ref[...])
        s_sc[...] = s
        pm_sc[...] = functools.reduce(jnp.maximum, [s[:, c:c + LANES] for c in range(0, seq, LANES)])

    def finish(i, s_sc, pm_sc):
        rows = pl.ds(i * tq, tq)
        m = jnp.max(pm_sc[...], axis=-1, keepdims=True)
        e = jnp.exp2(s_sc[...] - m)
        l = jnp.sum(e, axis=-1, keepdims=True)
        o2 = _dot(e.astype(BF16), v_ref[...]) * (1.0 / l)
        o = o2[:tq] - lam * o2[tq:]
        ms = jnp.mean(o * o, axis=-1, keepdims=True)
        y = o * lax.rsqrt(ms + NORM_EPS) * gn_ref[...] * (1.0 - lam_init)
        out_ref[rows, :] = (y * _silu(g_ref[rows, :].astype(F32))).astype(BF16)

    bufs = ((s0_sc, pm0_sc), (s1_sc, pm1_sc))
    scores(0, *bufs[0])
    for i in range(n_blk):
        if i + 1 < n_blk:
            scores(i + 1, *bufs[(i + 1) % 2])
        finish(i, *bufs[i % 2])


def _diff(proj, dl, lam_init, gn, layer, batch, seq, tq=256):
    blk = lambda off: pl.BlockSpec((seq, LANES), lambda b, h: (b, off // LANES + h))
    per_layer = lambda shape: pl.BlockSpec((None,) + shape, lambda b, h: (layer, 0, 0))
    return pl.pallas_call(
        functools.partial(_diff_kernel, seq=seq, tq=tq),
        grid=(batch, DIFF_HEADS),
        in_specs=[blk(OFF_BQ), blk(OFF_BK), blk(OFF_BV), blk(OFF_BG),
                  per_layer((4, DIFF_D)), per_layer((1, 1)), per_layer((1, LANES))],
        out_specs=pl.BlockSpec((seq, LANES), lambda b, h: (b, h)),
        out_shape=jax.ShapeDtypeStruct((batch * seq, BRANCH_W), BF16),
        scratch_shapes=[pltpu.VMEM((seq // tq, 2 * tq, LANES), BF16),
                        pltpu.VMEM((2 * tq, seq), F32), pltpu.VMEM((2 * tq, seq), F32),
                        pltpu.VMEM((2 * tq, LANES), F32), pltpu.VMEM((2 * tq, LANES), F32)],
        compiler_params=pltpu.CompilerParams(
            dimension_semantics=("arbitrary", "arbitrary"), vmem_limit_bytes=VMEM_LIMIT),
        name="diff",
    )(proj, proj, proj, proj, dl, lam_init, gn)


def _dilated_kernel(q_ref, k_ref, v_ref, g_ref, out_ref,
                    nat_sc, c4_sc, k4_sc, q16_sc, k16_sc, w1_sc, w4_sc, w16_sc,
                    s0_sc, s1_sc, s2_sc, e0_sc, e1_sc, e2_sc, m0_sc, m1_sc, m2_sc,
                    o_sc, l_sc, bias_sc, *, seq, unroll):
    tq, win = DIL_TQ, 2 * DIL_TQ
    stage_rows = 2 * tq
    n_stage = seq // stage_rows
    len4, len16 = seq // 4, seq // 16
    head0 = _lane_iota((tq, LANES)) < DIL_HD
    head0_stage = _lane_iota((stage_rows, LANES)) < DIL_HD

    dcol = (lax.broadcasted_iota(jnp.int32, (tq, win), 1) - lax.broadcasted_iota(jnp.int32, (tq, win), 0))
    for idx in range(3):
        bias_sc[idx] = jnp.where(jnp.abs(dcol - idx * DIL_HALF) <= DIL_HALF, 0.0, NEG_INF)

    def stage(i, carry):
        rows = pl.ds(pl.multiple_of(i * stage_rows, stage_rows), stage_rows)
        nat_sc[0, rows, :] = q_ref[rows, :].astype(F32)
        nat_sc[1, rows, :] = k_ref[rows, :].astype(F32)
        nat_sc[2, rows, :] = v_ref[rows, :].astype(F32)
        return carry

    lax.fori_loop(0, n_stage, stage, 0)

    for t in range(3):
        for r in range(4):
            c4_sc[t, r * len4:(r + 1) * len4, :] = nat_sc[t, pl.ds(r, len4, stride=4), :]
    for r4 in range(4):
        for j in range(4):
            dst = slice((r4 + 4 * j) * len16, (r4 + 4 * j + 1) * len16)
            src = pl.ds(r4 * len4 + j, len16, stride=4)
            q16_sc[dst, :] = c4_sc[0, src, :]
            k16_sc[dst, :] = c4_sc[1, src, :].astype(BF16)
            w16_sc[0, dst, :] = c4_sc[2, src, :].astype(BF16)

    def value_weights(i, carry):
        rows = pl.ds(pl.multiple_of(i * stage_rows, stage_rows), stage_rows)
        k4_sc[rows, :] = c4_sc[1, rows, :].astype(BF16)
        for src, dst in ((v_ref, w1_sc), (c4_sc.at[2], w4_sc), (w16_sc.at[0], w16_sc)):
            v = src[rows, :].astype(F32)
            dst[1, rows, :] = jnp.where(head0_stage, 1.0, v).astype(BF16)
            dst[0, rows, :] = jnp.where(head0_stage, v, 1.0).astype(BF16)
        return carry

    lax.fori_loop(0, n_stage, value_weights, 0)

    def make_pattern(g, q_src, k_src, w_src, s_sc, e_sc, m_sc, seq_len, w, out_rows):
        per_seq = seq_len // tq

        def where(t):
            r, i = t // per_seq, t % per_seq
            ks = jnp.clip(i * tq - tq // 2, 0, seq_len - w)
            which = jnp.where(i == 0, 0, jnp.where(i == per_seq - 1, 2, 1)) if per_seq > 1 else 0
            qrows = pl.ds(pl.multiple_of(r * seq_len + i * tq, tq), tq)
            krows = pl.ds(pl.multiple_of(r * seq_len + ks, tq // 2), w)
            return qrows, krows, which

        def scores(t):
            qrows, krows, which = where(t)
            q = q_src[qrows, :]
            q2 = jnp.concatenate([jnp.where(head0, q, 0.0), jnp.where(head0, 0.0, q)], axis=0)
            bias = bias_sc[which, :, :w]
            s = _dot_nt(q2.astype(BF16), k_src[krows, :])
            s_sc[t, :, :w] = s + jnp.concatenate([bias, bias], axis=0)

        def softmax(t):
            s = s_sc[t, :, :w]
            m = jnp.max(s, axis=-1, keepdims=True)
            e_sc[t, :, :w] = jnp.exp2(s - m).astype(BF16)
            m_sc[t] = jnp.where(head0, m[:tq], m[tq:])

        def values(t):
            _, krows, _ = where(t)
            out0 = _dot(e_sc[t, :tq, :w], w_src[0, krows, :])
            out1 = _dot(e_sc[t, tq:, :w], w_src[1, krows, :])
            l = pltpu.roll(jnp.where(head0, out1, out0), DIL_HD, 1)
            rows = out_rows(t)
            o_sc[g, rows, :] = jnp.where(head0, out0, out1) * (1.0 / l)
            l_sc[g, rows, :] = m_sc[t] * LN2 + jnp.log(l)

        return scores, softmax, values

    sc0, sm0, va0 = make_pattern(0, nat_sc.at[0], k_ref, w1_sc, s0_sc, e0_sc, m0_sc, seq, win,
                                 lambda t: pl.ds(pl.multiple_of(t * tq, tq), tq))
    sc1, sm1, va1 = make_pattern(1, c4_sc.at[0], k4_sc, w4_sc, s1_sc, e1_sc, m1_sc, len4, win,
                                 lambda t: pl.ds(pl.multiple_of(t * tq, tq), tq))
    sc2, sm2, va2 = make_pattern(2, q16_sc, k16_sc, w16_sc, s2_sc, e2_sc, m2_sc, len16, len16,
                                 lambda t: pl.ds((t % 4) * len4 + t // 4, len16, stride=4))

    for passes in ((sc0,), (sm0, sc1), (va0, sm1, sc2), (va1, sm2), (va2,)):
        def body(t, carry, passes=passes):
            for f in passes:
                f(t)
            return carry

        lax.fori_loop(0, seq // tq, body, 0, unroll=unroll)

    y_sc = nat_sc.at[0]
    per_class = len4 // stage_rows

    def combine(i, carry):
        r, m0 = i // per_class, (i % per_class) * stage_rows
        rows = pl.ds(pl.multiple_of(r * len4 + m0, stage_rows), stage_rows)
        nat_rows = pl.ds(r + 4 * m0, stage_rows, stride=4)
        l0, l1, l2 = l_sc[0, nat_rows, :], l_sc[1, rows, :], l_sc[2, rows, :]
        m = jnp.maximum(jnp.maximum(l0, l1), l2)
        w0, w1, w2 = jnp.exp(l0 - m), jnp.exp(l1 - m), jnp.exp(l2 - m)
        o = w0 * o_sc[0, nat_rows, :] + w1 * o_sc[1, rows, :] + w2 * o_sc[2, rows, :]
        y_sc[nat_rows, :] = o * (1.0 / (w0 + w1 + w2))
        return carry

    lax.fori_loop(0, n_stage, combine, 0)

    def gate(i, carry):
        rows = pl.ds(pl.multiple_of(i * stage_rows, stage_rows), stage_rows)
        out_ref[rows, :] = (y_sc[rows, :] * _silu(g_ref[rows, :].astype(F32))).astype(BF16)
        return carry

    lax.fori_loop(0, n_stage, gate, 0)


def _dilated(proj, batch, seq, unroll=16):
    assert seq // 16 == DIL_TQ and all(w // (2 * d) == DIL_HALF for w, d in DIL_PATTERNS)
    blk = lambda off: pl.BlockSpec((seq, LANES), lambda b, p: (b, off // LANES + p))
    n_blocks = seq // DIL_TQ
    f32_rows = pltpu.VMEM((seq, LANES), F32)
    bf16_rows = pltpu.VMEM((seq, LANES), BF16)
    bf16_pair = pltpu.VMEM((2, seq, LANES), BF16)
    s_buf = pltpu.VMEM((n_blocks, 2 * DIL_TQ, 2 * DIL_TQ), F32)
    e_buf = pltpu.VMEM((n_blocks, 2 * DIL_TQ, 2 * DIL_TQ), BF16)
    m_buf = pltpu.VMEM((n_blocks, DIL_TQ, LANES), F32)
    return pl.pallas_call(
        functools.partial(_dilated_kernel, seq=seq, unroll=unroll),
        grid=(batch, DIL_HEADS // 2),
        in_specs=[blk(OFF_CQ), blk(OFF_CK), blk(OFF_CV), blk(OFF_CG)],
        out_specs=pl.BlockSpec((seq, LANES), lambda b, p: (b, p)),
        out_shape=jax.ShapeDtypeStruct((batch * seq, BRANCH_W), BF16),
        scratch_shapes=[
            pltpu.VMEM((3, seq, LANES), F32),
            pltpu.VMEM((3, seq, LANES), F32), bf16_rows, f32_rows, bf16_rows,
            bf16_pair, bf16_pair, bf16_pair,
            s_buf, s_buf, s_buf, e_buf, e_buf, e_buf, m_buf, m_buf, m_buf,
            pltpu.VMEM((3, seq, LANES), F32), pltpu.VMEM((3, seq, LANES), F32),
            pltpu.VMEM((3, DIL_TQ, 2 * DIL_TQ), F32),
        ],
        compiler_params=pltpu.CompilerParams(
            dimension_semantics=("arbitrary", "arbitrary"), vmem_limit_bytes=VMEM_LIMIT),
        name="dilated",
    )(proj, proj, proj, proj)


def _merge_kernel(x_ref, ya_ref, yb_ref, yc_ref, m_ref, wb_ref, wo_ref, fg_ref, out_ref, *, final):
    mixed = None
    for gi, y_ref in enumerate((ya_ref, yb_ref, yc_ref)):
        z = _dot(y_ref[...], wb_ref[gi])
        gate = 0.5 * jnp.tanh(0.5 * m_ref[:, gi * D_MODEL:(gi + 1) * D_MODEL].astype(F32)) + 0.5
        mixed = gate * z if mixed is None else mixed + gate * z
    x = x_ref[...] + _dot(mixed.astype(BF16), wo_ref[...])
    if final:
        ms = jnp.mean(x * x, axis=-1, keepdims=True)
        x = x * lax.rsqrt(ms + NORM_EPS) * fg_ref[...]
    out_ref[...] = x


def _merge(x2, ya, yb, yc, proj, wb, wo, final_g, layer, final, tm=512):
    rows = x2.shape[0]
    yspec = pl.BlockSpec((tm, BRANCH_W), lambda i: (i, 0))
    return pl.pallas_call(
        functools.partial(_merge_kernel, final=final),
        grid=(rows // tm,),
        in_specs=[
            pl.BlockSpec((tm, D_MODEL), lambda i: (i, 0)),
            yspec, yspec, yspec,
            pl.BlockSpec((tm, MERGE_W), lambda i: (i, 0)),
            pl.BlockSpec((None, N_BRANCH, BRANCH_W, D_MODEL), lambda i: (layer, 0, 0, 0)),
            pl.BlockSpec((None, D_MODEL, D_MODEL), lambda i: (layer, 0, 0)),
            pl.BlockSpec((1, D_MODEL), lambda i: (0, 0)),
        ],
        out_specs=pl.BlockSpec((tm, D_MODEL), lambda i: (i, 0)),
        out_shape=jax.ShapeDtypeStruct((rows, D_MODEL), F32),
        compiler_params=pltpu.CompilerParams(
            dimension_semantics=("arbitrary",), vmem_limit_bytes=VMEM_LIMIT),
        name="merge",
    )(x2, ya, yb, yc, proj, wb, wo, final_g)


def _rope_tables(seq, dim):
    inv = 1.0 / (ROPE_THETA ** (jnp.arange(0, dim, 2, dtype=F32) / dim))
    ang = jnp.arange(seq, dtype=F32)[:, None] * inv[None, :]
    cos, sin = jnp.cos(ang), jnp.sin(ang)
    reps = LANES // dim
    cos_t = jnp.tile(jnp.concatenate([cos, cos], axis=-1), (1, reps))
    sin_t = jnp.tile(jnp.concatenate([-sin, sin], axis=-1), (1, reps))
    return cos_t, sin_t


def kernel(x, norm_g, w_in, gla_gate_up, gla_gate_b, gla_norm_g, diff_lambda, diff_norm_g, w_branch, w_out, final_norm_g):
    batch, seq, _ = x.shape
    depth = w_in.shape[0]
    cos, sin = _rope_tables(seq, DIFF_D)

    lo, hi = LOWRANK_OFF, LOWRANK_OFF + LOWRANK_W
    merge_off = IN_COLS - MERGE_W
    w_bf = w_in.astype(BF16)
    w_main = jnp.concatenate([w_bf[:, :, merge_off:], w_bf[:, :, :lo], w_bf[:, :, hi:merge_off]], axis=-1)
    w_lr = w_bf[:, :, lo:hi]
    w_low = jnp.concatenate([w_lr, w_lr, w_lr, jnp.zeros_like(w_lr)], axis=-1)
    gu = gla_gate_up.reshape(depth, 2, GLA_RANK, GLA_HEADS, GLA_DK)
    zeros = jnp.zeros_like(gu[:, 0])
    gu_f = jnp.concatenate([gu[:, 0], zeros], axis=-1).reshape(depth, GLA_RANK, GLA_HEADS * LANES)
    gu_b = jnp.concatenate([zeros, gu[:, 1]], axis=-1).reshape(depth, GLA_RANK, GLA_HEADS * LANES)
    gu_cat = jnp.concatenate([gu_f, gu_b], axis=1)
    gu_hi = gu_cat.astype(BF16)
    gu_lo = (gu_cat - gu_hi.astype(F32)).astype(BF16)
    gate_up = jnp.concatenate([gu_hi, gu_lo, gu_hi, jnp.zeros_like(gu_hi)], axis=1)
    gb = gla_gate_b.reshape(depth, 2, GLA_HEADS, GLA_DK)
    gate_b = jnp.concatenate([gb[:, 0], gb[:, 1]], axis=-1).reshape(depth, 1, GLA_HEADS * LANES)
    wb = w_branch.astype(BF16)
    wo = w_out.astype(BF16)
    lam_init = jnp.asarray([0.8 - 0.6 * math.exp(-0.3 * layer) for layer in range(depth)],
                           F32).reshape(depth, 1, 1)
    norm_g3 = norm_g[:, None, :]
    gla_norm_g3 = gla_norm_g[:, None, :]
    diff_norm_g3 = diff_norm_g[:, None, :]

    x2 = x.reshape(batch * seq, D_MODEL)
    for layer in range(depth):
        proj, la = _in_proj(x2, norm_g3, w_main, w_low, gate_up, gate_b, cos, sin, layer, seq)
        ya = _gla(proj, la, gla_norm_g3, layer, batch, seq)
        yb = _diff(proj, diff_lambda, lam_init, diff_norm_g3, layer, batch, seq)
        yc = _dilated(proj, batch, seq)
        x2 = _merge(x2, ya, yb, yc, proj, wb, wo, final_norm_g[None], layer, final=layer == depth - 1)
    return x2.reshape(batch, seq, D_MODEL)
```

```python
import functools
import math

import jax
import jax.numpy as jnp
from jax import lax
from jax.experimental import pallas as pl
from jax.experimental.pallas import tpu as pltpu

F32 = jnp.float32
BF16 = jnp.bfloat16

LANES = 128

D_MODEL = 1024
ROPE_THETA = 10000.0
NORM_EPS = 1e-6
BRANCH_W = D_MODEL // 2
N_BRANCH = 3

GLA_HEADS = 4
GLA_DV = BRANCH_W // GLA_HEADS
GLA_DK = GLA_DV // 2
GLA_RANK = 16
GLA_TAU = 16.0
GLA_CHUNK = 64
GLA_QK_W = GLA_HEADS * GLA_DK

DIFF_HEADS = 4
DIFF_D = BRANCH_W // (2 * DIFF_HEADS)
DIFF_QK_W = DIFF_HEADS * 2 * DIFF_D

DIL_HEADS = 8
DIL_HD = BRANCH_W // DIL_HEADS
DIL_W = DIL_HEADS * DIL_HD
DIL_PATTERNS = ((128, 1), (512, 4), (2048, 16))
DIL_HALF = 64
DIL_TQ = 128
NEG_INF = -1e30

MERGE_W = N_BRANCH * D_MODEL
LOWRANK_OFF = GLA_QK_W * 2 + BRANCH_W * 2
LOWRANK_W = 2 * GLA_RANK
IN_COLS = LOWRANK_OFF + LOWRANK_W + 2 * (DIFF_QK_W * 2 + BRANCH_W * 2) + MERGE_W
MAIN_W = IN_COLS - LOWRANK_W

OFF_AQ = MERGE_W
OFF_AK = OFF_AQ + GLA_QK_W
OFF_AV = OFF_AK + GLA_QK_W
OFF_AG = OFF_AV + BRANCH_W
OFF_BQ = OFF_AG + BRANCH_W
OFF_BK = OFF_BQ + DIFF_QK_W
OFF_BV = OFF_BK + DIFF_QK_W
OFF_BG = OFF_BV + BRANCH_W
OFF_CQ = OFF_BG + BRANCH_W
OFF_CK = OFF_CQ + DIL_W
OFF_CV = OFF_CK + DIL_W
OFF_CG = OFF_CV + BRANCH_W

VMEM_LIMIT = 56 * 1024 * 1024


def _dot(a, b):
    return jnp.dot(a, b, preferred_element_type=F32)


def _dot_nt(a, b):
    return lax.dot_general(a, b, (((1,), (1,)), ((), ())), preferred_element_type=F32)


def _sigmoid(x):
    return 1.0 / (1.0 + jnp.exp(-x))


def _silu(x):
    return x * _sigmoid(x)


def _lane_iota(shape):
    return lax.broadcasted_iota(jnp.int32, shape, len(shape) - 1)


def _split_bf16(x):
    hi = x.astype(BF16).astype(F32)
    return hi, x - hi


def _rope(x, cos, sin_signed):
    lane = _lane_iota(x.shape)
    first_half = (lane % 64) < 32
    partner = jnp.where(first_half, pltpu.roll(x, 96, 1), pltpu.roll(x, 32, 1))
    return x * cos + partner * sin_signed


_PLAIN, _ROPE_Q, _ROPE_K = 0, 1, 2
_W_MERGE, _W_A, _W_BC = 0, 1, 2
_PROJ_SEGMENTS = (
    (_W_MERGE, 0, MERGE_W // 2, _PLAIN, 0), (_W_MERGE, MERGE_W // 2, MERGE_W // 2, _PLAIN, MERGE_W // 2),
    (_W_A, 0, LOWRANK_OFF, _PLAIN, OFF_AQ),
    (_W_BC, 0, DIFF_QK_W, _ROPE_Q, OFF_BQ), (_W_BC, OFF_BK - OFF_BQ, DIFF_QK_W, _ROPE_K, OFF_BK),
    (_W_BC, OFF_BV - OFF_BQ, 2 * BRANCH_W, _PLAIN, OFF_BV),
    (_W_BC, OFF_CQ - OFF_BQ, DIL_W, _ROPE_Q, OFF_CQ), (_W_BC, OFF_CK - OFF_BQ, DIL_W, _ROPE_K, OFF_CK),
    (_W_BC, OFF_CV - OFF_BQ, 2 * BRANCH_W, _PLAIN, OFF_CV),
)
LOG2E = 1.4426950408889634
LN2 = 0.6931471805599453
QK_SCALE = DIFF_D ** -0.5 * LOG2E


def _in_proj_kernel(x_ref, g_ref, wm_ref, wa_ref, wbc_ref, wl_ref, gu_ref, gb_ref, cos_ref, sin_ref,
                    out_ref, la_ref):
    x = x_ref[...]
    ms = jnp.mean(x * x, axis=-1, keepdims=True)
    hb = (x * lax.rsqrt(ms + NORM_EPS) * g_ref[...]).astype(BF16)
    low = _dot(hb, wl_ref[...])
    low_hi, low_lo = _split_bf16(low)
    lhs = jnp.where(_lane_iota(low.shape) < 2 * LOWRANK_W, low_hi, low_lo).astype(BF16)
    z = _dot(lhs, gu_ref[...]) + gb_ref[...]
    log_sig = jnp.minimum(z, 0.0) - jnp.log1p(jnp.exp(-jnp.abs(z)))
    la_ref[...] = log_sig * (1.0 / GLA_TAU)
    w_refs = (wm_ref, wa_ref, wbc_ref)
    for which, src, width, kind, start in _PROJ_SEGMENTS:
        res = _dot(hb, w_refs[which][:, src:src + width])
        if kind != _PLAIN:
            cos, sin = cos_ref[...], sin_ref[...]
            tiles = [_rope(res[:, c:c + LANES], cos, sin) for c in range(0, width, LANES)]
            res = jnp.concatenate(tiles, axis=1)
            if kind == _ROPE_Q:
                res = res * QK_SCALE
        out_ref[:, start:start + width] = res.astype(BF16)


def _in_proj(x2, g, w_parts, w_low, gate_up, gate_b, cos, sin, layer, seq, tm=512):
    assert DIFF_D == DIL_HD and seq % tm == 0
    rows = x2.shape[0]
    resident = dict(pipeline_mode=pl.Buffered(1))
    pos_blocks = seq // tm
    return pl.pallas_call(
        _in_proj_kernel,
        grid=(rows // tm,),
        in_specs=[
            pl.BlockSpec((tm, D_MODEL), lambda i: (i, 0)),
            pl.BlockSpec((None, 1, D_MODEL), lambda i: (layer, 0, 0)),
            *[pl.BlockSpec((None,) + w.shape[1:], lambda i: (layer, 0, 0), **resident) for w in w_parts],
            pl.BlockSpec((None, D_MODEL, LANES), lambda i: (layer, 0, 0), **resident),
            pl.BlockSpec((None, LANES, 4 * LANES), lambda i: (layer, 0, 0), **resident),
            pl.BlockSpec((None, 1, 4 * LANES), lambda i: (layer, 0, 0)),
            pl.BlockSpec((tm, LANES), lambda i: (i % pos_blocks, 0)),
            pl.BlockSpec((tm, LANES), lambda i: (i % pos_blocks, 0)),
        ],
        out_specs=[
            pl.BlockSpec((tm, MAIN_W), lambda i: (i, 0)),
            pl.BlockSpec((tm, 4 * LANES), lambda i: (i, 0)),
        ],
        out_shape=[
            jax.ShapeDtypeStruct((rows, MAIN_W), BF16),
            jax.ShapeDtypeStruct((rows, 4 * LANES), F32),
        ],
        compiler_params=pltpu.CompilerParams(
            dimension_semantics=("arbitrary",), vmem_limit_bytes=VMEM_LIMIT),
        name="in_proj",
    )(x2, g, *w_parts, w_low, gate_up, gate_b, cos, sin)


def _gla_kernel(q_ref, k_ref, v_ref, g_ref, la_ref, gn_ref, out_ref,
                u_sc, d_sc, st_sc, qh_sc, kh_sc, att_sc, *, seq, unroll):
    C = GLA_CHUNK
    n_chunks = seq // C
    lane_c = _lane_iota((C, LANES))
    row_c = lax.broadcasted_iota(jnp.int32, (C, LANES), 0)
    fwd_c = lane_c < GLA_DK
    tri = jnp.where(fwd_c, row_c - lane_c, lane_c - GLA_DK - row_c) >= 0
    tri_bf = tri.astype(F32).astype(BF16)
    tri2 = jnp.concatenate([tri_bf, tri_bf], axis=1)
    fwd_sq = _lane_iota((LANES, LANES)) < GLA_DK
    scale = GLA_DK ** -0.5

    def head_cols(hh):
        return slice(hh * LANES, (hh + 1) * LANES)

    def dup_heads(ref, rows):
        xx = ref[rows, :].astype(F32)
        rolled = pltpu.roll(xx, GLA_DK, 1)
        return jnp.where(fwd_c, xx, rolled), jnp.where(fwd_c, rolled, xx)

    def prep(n, carry):
        rows = pl.ds(pl.multiple_of(n * C, C), C)
        q2 = dup_heads(q_ref, rows)
        k2 = dup_heads(k_ref, rows)
        for hh in range(2):
            cols = head_cols(hh)
            la = la_ref[rows, cols]
            la_bd = jnp.concatenate([jnp.where(fwd_c, la, 0.0), jnp.where(fwd_c, 0.0, la)], axis=0)
            la_hi, la_lo = _split_bf16(la_bd)
            b = _dot(tri2, jnp.concatenate([la_hi, la_lo], axis=0).astype(BF16))
            tot = jnp.sum(la, axis=0, keepdims=True)
            qh_sc[hh, rows, :] = (q2[hh] * jnp.exp(b) * scale).astype(BF16)
            k_hat = k2[hh] * jnp.exp(-b)
            kh_sc[hh, n] = jnp.concatenate(
                [jnp.where(fwd_c, k_hat, 0.0), jnp.where(fwd_c, 0.0, k_hat)], axis=0).astype(BF16)
            k_end = (k2[hh] * jnp.exp(tot - b)).astype(BF16)
            v_t = v_ref[rows, cols].astype(F32).T.astype(BF16)
            u_sc[hh, n] = _dot(v_t, k_end)
            d_sc[hh, n] = jnp.broadcast_to(jnp.exp(tot), (8, LANES))
        return carry

    lax.fori_loop(0, n_chunks, prep, 0, unroll=unroll)

    def scan(i, states):
        j = n_chunks - 1 - i
        new = []
        for hh in range(2):
            st_sc[hh, i] = states[hh]
            upd = jnp.where(fwd_sq, u_sc[hh, i], u_sc[hh, j])
            dec = jnp.where(fwd_sq[:8], d_sc[hh, i], d_sc[hh, j])[0:1]
            new.append(states[hh] * dec + upd)
        return tuple(new)

    zero = jnp.zeros((LANES, LANES), F32)
    lax.fori_loop(0, n_chunks, scan, (zero, zero))

    def attend(n, carry):
        rows = pl.ds(pl.multiple_of(n * C, C), C)
        for hh in range(2):
            att = _dot_nt(qh_sc[hh, rows, :], kh_sc[hh, n])
            att_sc[hh, rows, :] = jnp.where(tri, att, 0.0).astype(BF16)
        return carry

    lax.fori_loop(0, n_chunks, attend, 0, unroll=unroll)

    def emit(n, carry):
        rows = pl.ds(pl.multiple_of(n * C, C), C)
        for hh in range(2):
            cols = head_cols(hh)
            v = v_ref[rows, cols]
            o = _dot(att_sc[hh, rows, :], jnp.concatenate([v, v], axis=0))
            state = jnp.where(fwd_sq, st_sc[hh, n], st_sc[hh, n_chunks - 1 - n]).astype(BF16)
            o = o + _dot_nt(qh_sc[hh, rows, :], state)
            ms = jnp.mean(o * o, axis=-1, keepdims=True)
            y = o * lax.rsqrt(ms + NORM_EPS) * gn_ref[...]
            out_ref[rows, cols] = (y * _silu(g_ref[rows, cols].astype(F32))).astype(BF16)
        return carry

    lax.fori_loop(0, n_chunks, emit, 0, unroll=unroll)


def _gla(proj, la, gn, layer, batch, seq, unroll=32):
    n_chunks = seq // GLA_CHUNK
    pair = 2 * LANES
    return pl.pallas_call(
        functools.partial(_gla_kernel, seq=seq, unroll=unroll),
        grid=(batch, GLA_HEADS // 2),
        in_specs=[
            pl.BlockSpec((seq, LANES), lambda b, p: (b, OFF_AQ // LANES + p)),
            pl.BlockSpec((seq, LANES), lambda b, p: (b, OFF_AK // LANES + p)),
            pl.BlockSpec((seq, pair), lambda b, p: (b, OFF_AV // pair + p)),
            pl.BlockSpec((seq, pair), lambda b, p: (b, OFF_AG // pair + p)),
            pl.BlockSpec((seq, pair), lambda b, p: (b, p)),
            pl.BlockSpec((None, 1, LANES), lambda b, p: (layer, 0, 0)),
        ],
        out_specs=pl.BlockSpec((seq, pair), lambda b, p: (b, p)),
        out_shape=jax.ShapeDtypeStruct((batch * seq, BRANCH_W), BF16),
        scratch_shapes=[
            pltpu.VMEM((2, n_chunks, LANES, LANES), F32),
            pltpu.VMEM((2, n_chunks, 8, LANES), F32),
            pltpu.VMEM((2, n_chunks, LANES, LANES), F32),
            pltpu.VMEM((2, seq, LANES), BF16),
            pltpu.VMEM((2, n_chunks, LANES, LANES), BF16),
            pltpu.VMEM((2, seq, LANES), BF16),
        ],
        compiler_params=pltpu.CompilerParams(
            dimension_semantics=("arbitrary", "arbitrary"), vmem_limit_bytes=VMEM_LIMIT),
        name="gla",
    )(proj, proj, proj, proj, la, gn)


def _diff_kernel(q_ref, k_ref, v_ref, g_ref, dl_ref, li_ref, gn_ref, out_ref,
                 q_sc, s0_sc, s1_sc, pm0_sc, pm1_sc, *, seq, tq):
    n_blk = seq // tq
    half0 = _lane_iota((tq, LANES)) < DIFF_D

    def stack_q(i, carry):
        q = q_ref[pl.ds(pl.multiple_of(i * tq, tq), tq), :].astype(F32)
        q_sc[i] = jnp.concatenate([jnp.where(half0, q, 0.0), jnp.where(half0, 0.0, q)], axis=0).astype(BF16)
        return carry

    lax.fori_loop(0, n_blk, stack_q, 0)

    dl = dl_ref[...]
    lam_init = li_ref[...]
    lam = (jnp.exp(jnp.sum(dl[0:1] * dl[1:2], axis=-1, keepdims=True))
           - jnp.exp(jnp.sum(dl[2:3] * dl[3:4], axis=-1, keepdims=True)) + lam_init)

    def scores(i, s_sc, pm_sc):
        s = _dot_nt(q_sc[i], k_ref[...])
        s_sc[...] = s
        pm_sc[...] = functools.reduce(jnp.maximum, [s[:, c:c + LANES] for c in range(0, seq, LANES)])

    def finish(i, s_sc, pm_sc):
        rows = pl.ds(i * tq, tq)
        m = jnp.max(pm_sc[...], axis=-1, keepdims=True)
        e = jnp.exp2(s_sc[...] - m)
        l = jnp.sum(e, axis=-1, keepdims=True)
        o2 = _dot(e.astype(BF16), v_ref[...]) * (1.0 / l)
        o = o2[:tq] - lam * o2[tq:]
        ms = jnp.mean(o * o, axis=-1, keepdims=True)
        y = o * lax.rsqrt(ms + NORM_EPS) * gn_ref[...] * (1.0 - lam_init)
        out_ref[rows, :] = (y * _silu(g_ref[rows, :].astype(F32))).astype(BF16)

    bufs = ((s0_sc, pm0_sc), (s1_sc, pm1_sc))
    scores(0, *bufs[0])
    for i in range(n_blk):
        if i + 1 < n_blk:
            scores(i + 1, *bufs[(i + 1) % 2])
        finish(i, *bufs[i % 2])


def _diff(proj, dl, lam_init, gn, layer, batch, seq, tq=256):
    blk = lambda off: pl.BlockSpec((seq, LANES), lambda b, h: (b, off // LANES + h))
    per_layer = lambda shape: pl.BlockSpec((None,) + shape, lambda b, h: (layer, 0, 0))
    return pl.pallas_call(
        functools.partial(_diff_kernel, seq=seq, tq=tq),
        grid=(batch, DIFF_HEADS),
        in_specs=[blk(OFF_BQ), blk(OFF_BK), blk(OFF_BV), blk(OFF_BG),
                  per_layer((4, DIFF_D)), per_layer((1, 1)), per_layer((1, LANES))],
        out_specs=pl.BlockSpec((seq, LANES), lambda b, h: (b, h)),
        out_shape=jax.ShapeDtypeStruct((batch * seq, BRANCH_W), BF16),
        scratch_shapes=[pltpu.VMEM((seq // tq, 2 * tq, LANES), BF16),
                        pltpu.VMEM((2 * tq, seq), F32), pltpu.VMEM((2 * tq, seq), F32),
                        pltpu.VMEM((2 * tq, LANES), F32), pltpu.VMEM((2 * tq, LANES), F32)],
        compiler_params=pltpu.CompilerParams(
            dimension_semantics=("arbitrary", "arbitrary"), vmem_limit_bytes=VMEM_LIMIT),
        name="diff",
    )(proj, proj, proj, proj, dl, lam_init, gn)


def _dilated_kernel(q_ref, k_ref, v_ref, g_ref, out_ref,
                    nat_sc, c4_sc, k4_sc, q16_sc, k16_sc, w1_sc, w4_sc, w16_sc,
                    s0_sc, s1_sc, s2_sc, e0_sc, e1_sc, e2_sc, m0_sc, m1_sc, m2_sc,
                    o_sc, l_sc, bias_sc, *, seq, unroll):
    tq, win = DIL_TQ, 2 * DIL_TQ
    stage_rows = 2 * tq
    n_stage = seq // stage_rows
    len4, len16 = seq // 4, seq // 16
    head0 = _lane_iota((tq, LANES)) < DIL_HD
    head0_stage = _lane_iota((stage_rows, LANES)) < DIL_HD

    dcol = (lax.broadcasted_iota(jnp.int32, (tq, win), 1) - lax.broadcasted_iota(jnp.int32, (tq, win), 0))
    for idx in range(3):
        bias_sc[idx] = jnp.where(jnp.abs(dcol - idx * DIL_HALF) <= DIL_HALF, 0.0, NEG_INF)

    def stage(i, carry):
        rows = pl.ds(pl.multiple_of(i * stage_rows, stage_rows), stage_rows)
        nat_sc[0, rows, :] = q_ref[rows, :].astype(F32)
        nat_sc[1, rows, :] = k_ref[rows, :].astype(F32)
        nat_sc[2, rows, :] = v_ref[rows, :].astype(F32)
        return carry

    lax.fori_loop(0, n_stage, stage, 0)

    for t in range(3):
        for r in range(4):
            c4_sc[t, r * len4:(r + 1) * len4, :] = nat_sc[t, pl.ds(r, len4, stride=4), :]
    for r4 in range(4):
        for j in range(4):
            dst = slice((r4 + 4 * j) * len16, (r4 + 4 * j + 1) * len16)
            src = pl.ds(r4 * len4 + j, len16, stride=4)
            q16_sc[dst, :] = c4_sc[0, src, :]
            k16_sc[dst, :] = c4_sc[1, src, :].astype(BF16)
            w16_sc[0, dst, :] = c4_sc[2, src, :].astype(BF16)

    def value_weights(i, carry):
        rows = pl.ds(pl.multiple_of(i * stage_rows, stage_rows), stage_rows)
        k4_sc[rows, :] = c4_sc[1, rows, :].astype(BF16)
        for src, dst in ((v_ref, w1_sc), (c4_sc.at[2], w4_sc), (w16_sc.at[0], w16_sc)):
            v = src[rows, :].astype(F32)
            dst[1, rows, :] = jnp.where(head0_stage, 1.0, v).astype(BF16)
            dst[0, rows, :] = jnp.where(head0_stage, v, 1.0).astype(BF16)
        return carry

    lax.fori_loop(0, n_stage, value_weights, 0)

    def make_pattern(g, q_src, k_src, w_src, s_sc, e_sc, m_sc, seq_len, w, out_rows):
        per_seq = seq_len // tq

        def where(t):
            r, i = t // per_seq, t % per_seq
            ks = jnp.clip(i * tq - tq // 2, 0, seq_len - w)
            which = jnp.where(i == 0, 0, jnp.where(i == per_seq - 1, 2, 1)) if per_seq > 1 else 0
            qrows = pl.ds(pl.multiple_of(r * seq_len + i * tq, tq), tq)
            krows = pl.ds(pl.multiple_of(r * seq_len + ks, tq // 2), w)
            return qrows, krows, which

        def scores(t):
            qrows, krows, which = where(t)
            q = q_src[qrows, :]
            q2 = jnp.concatenate([jnp.where(head0, q, 0.0), jnp.where(head0, 0.0, q)], axis=0)
            bias = bias_sc[which, :, :w]
            s = _dot_nt(q2.astype(BF16), k_src[krows, :])
            s_sc[t, :, :w] = s + jnp.concatenate([bias, bias], axis=0)

        def softmax(t):
            s = s_sc[t, :, :w]
            m = jnp.max(s, axis=-1, keepdims=True)
            e_sc[t, :, :w] = jnp.exp2(s - m).astype(BF16)
            m_sc[t] = jnp.where(head0, m[:tq], m[tq:])

        def values(t):
            _, krows, _ = where(t)
            out0 = _dot(e_sc[t, :tq, :w], w_src[0, krows, :])
            out1 = _dot(e_sc[t, tq:, :w], w_src[1, krows, :])
            l = pltpu.roll(jnp.where(head0, out1, out0), DIL_HD, 1)
            rows = out_rows(t)
            o_sc[g, rows, :] = jnp.where(head0, out0, out1) * (1.0 / l)
            l_sc[g, rows, :] = m_sc[t] * LN2 + jnp.log(l)

        return scores, softmax, values

    sc0, sm0, va0 = make_pattern(0, nat_sc.at[0], k_ref, w1_sc, s0_sc, e0_sc, m0_sc, seq, win,
                                 lambda t: pl.ds(pl.multiple_of(t * tq, tq), tq))
    sc1, sm1, va1 = make_pattern(1, c4_sc.at[0], k4_sc, w4_sc, s1_sc, e1_sc, m1_sc, len4, win,
                                 lambda t: pl.ds(pl.multiple_of(t * tq, tq), tq))
    sc2, sm2, va2 = make_pattern(2, q16_sc, k16_sc, w16_sc, s2_sc, e2_sc, m2_sc, len16, len16,
                                 lambda t: pl.ds((t % 4) * len4 + t // 4, len16, stride=4))

    for passes in ((sc0,), (sm0, sc1), (va0, sm1, sc2), (va1, sm2), (va2,)):
        def body(t, carry, passes=passes):
            for f in passes:
                f(t)
            return carry

        lax.fori_loop(0, seq // tq, body, 0, unroll=unroll)

    y_sc = nat_sc.at[0]
    per_class = len4 // stage_rows

    def combine(i, carry):
        r, m0 = i // per_class, (i % per_class) * stage_rows
        rows = pl.ds(pl.multiple_of(r * len4 + m0, stage_rows), stage_rows)
        nat_rows = pl.ds(r + 4 * m0, stage_rows, stride=4)
        l0, l1, l2 = l_sc[0, nat_rows, :], l_sc[1, rows, :], l_sc[2, rows, :]
        m = jnp.maximum(jnp.maximum(l0, l1), l2)
        w0, w1, w2 = jnp.exp(l0 - m), jnp.exp(l1 - m), jnp.exp(l2 - m)
        o = w0 * o_sc[0, nat_rows, :] + w1 * o_sc[1, rows, :] + w2 * o_sc[2, rows, :]
        y_sc[nat_rows, :] = o * (1.0 / (w0 + w1 + w2))
        return carry

    lax.fori_loop(0, n_stage, combine, 0)

    def gate(i, carry):
        rows = pl.ds(pl.multiple_of(i * stage_rows, stage_rows), stage_rows)
        out_ref[rows, :] = (y_sc[rows, :] * _silu(g_ref[rows, :].astype(F32))).astype(BF16)
        return carry

    lax.fori_loop(0, n_stage, gate, 0)


def _dilated(proj, batch, seq, unroll=16):
    assert seq // 16 == DIL_TQ and all(w // (2 * d) == DIL_HALF for w, d in DIL_PATTERNS)
    blk = lambda off: pl.BlockSpec((seq, LANES), lambda b, p: (b, off // LANES + p))
    n_blocks = seq // DIL_TQ
    f32_rows = pltpu.VMEM((seq, LANES), F32)
    bf16_rows = pltpu.VMEM((seq, LANES), BF16)
    bf16_pair = pltpu.VMEM((2, seq, LANES), BF16)
    s_buf = pltpu.VMEM((n_blocks, 2 * DIL_TQ, 2 * DIL_TQ), F32)
    e_buf = pltpu.VMEM((n_blocks, 2 * DIL_TQ, 2 * DIL_TQ), BF16)
    m_buf = pltpu.VMEM((n_blocks, DIL_TQ, LANES), F32)
    return pl.pallas_call(
        functools.partial(_dilated_kernel, seq=seq, unroll=unroll),
        grid=(batch, DIL_HEADS // 2),
        in_specs=[blk(OFF_CQ), blk(OFF_CK), blk(OFF_CV), blk(OFF_CG)],
        out_specs=pl.BlockSpec((seq, LANES), lambda b, p: (b, p)),
        out_shape=jax.ShapeDtypeStruct((batch * seq, BRANCH_W), BF16),
        scratch_shapes=[
            pltpu.VMEM((3, seq, LANES), F32),
            pltpu.VMEM((3, seq, LANES), F32), bf16_rows, f32_rows, bf16_rows,
            bf16_pair, bf16_pair, bf16_pair,
            s_buf, s_buf, s_buf, e_buf, e_buf, e_buf, m_buf, m_buf, m_buf,
            pltpu.VMEM((3, seq, LANES), F32), pltpu.VMEM((3, seq, LANES), F32),
            pltpu.VMEM((3, DIL_TQ, 2 * DIL_TQ), F32),
        ],
        compiler_params=pltpu.CompilerParams(
            dimension_semantics=("arbitrary", "arbitrary"), vmem_limit_bytes=VMEM_LIMIT),
        name="dilated",
    )(proj, proj, proj, proj)


def _merge_kernel(x_ref, ya_ref, yb_ref, yc_ref, m_ref, wb_ref, wo_ref, fg_ref, out_ref, wb_sc, wo_sc,
                  *, final):
    @pl.when(pl.program_id(0) == 0)
    def _():
        wb_sc[...] = wb_ref[...].astype(BF16)
        wo_sc[...] = wo_ref[...].astype(BF16)

    mixed = None
    for gi, y_ref in enumerate((ya_ref, yb_ref, yc_ref)):
        z = _dot(y_ref[...], wb_sc[gi])
        gate = 0.5 * jnp.tanh(0.5 * m_ref[:, gi * D_MODEL:(gi + 1) * D_MODEL].astype(F32)) + 0.5
        mixed = gate * z if mixed is None else mixed + gate * z
    x = x_ref[...] + _dot(mixed.astype(BF16), wo_sc[...])
    if final:
        ms = jnp.mean(x * x, axis=-1, keepdims=True)
        x = x * lax.rsqrt(ms + NORM_EPS) * fg_ref[...]
    out_ref[...] = x


def _merge(x2, ya, yb, yc, proj, wb, wo, final_g, layer, final, tm=512):
    rows = x2.shape[0]
    yspec = pl.BlockSpec((tm, BRANCH_W), lambda i: (i, 0))
    return pl.pallas_call(
        functools.partial(_merge_kernel, final=final),
        grid=(rows // tm,),
        in_specs=[
            pl.BlockSpec((tm, D_MODEL), lambda i: (i, 0)),
            yspec, yspec, yspec,
            pl.BlockSpec((tm, MERGE_W), lambda i: (i, 0)),
            pl.BlockSpec((None, N_BRANCH, BRANCH_W, D_MODEL), lambda i: (layer, 0, 0, 0),
                         pipeline_mode=pl.Buffered(1)),
            pl.BlockSpec((None, D_MODEL, D_MODEL), lambda i: (layer, 0, 0), pipeline_mode=pl.Buffered(1)),
            pl.BlockSpec((1, D_MODEL), lambda i: (0, 0)),
        ],
        out_specs=pl.BlockSpec((tm, D_MODEL), lambda i: (i, 0)),
        out_shape=jax.ShapeDtypeStruct((rows, D_MODEL), F32),
        scratch_shapes=[pltpu.VMEM((N_BRANCH, BRANCH_W, D_MODEL), BF16), pltpu.VMEM((D_MODEL, D_MODEL), BF16)],
        compiler_params=pltpu.CompilerParams(
            dimension_semantics=("arbitrary",), vmem_limit_bytes=VMEM_LIMIT),
        name="merge",
    )(x2, ya, yb, yc, proj, wb, wo, final_g)


def _rope_tables(seq, dim):
    inv = 1.0 / (ROPE_THETA ** (jnp.arange(0, dim, 2, dtype=F32) / dim))
    ang = jnp.arange(seq, dtype=F32)[:, None] * inv[None, :]
    cos, sin = jnp.cos(ang), jnp.sin(ang)
    reps = LANES // dim
    cos_t = jnp.tile(jnp.concatenate([cos, cos], axis=-1), (1, reps))
    sin_t = jnp.tile(jnp.concatenate([-sin, sin], axis=-1), (1, reps))
    return cos_t, sin_t


def kernel(x, norm_g, w_in, gla_gate_up, gla_gate_b, gla_norm_g, diff_lambda, diff_norm_g, w_branch, w_out, final_norm_g):
    batch, seq, _ = x.shape
    depth = w_in.shape[0]
    cos, sin = _rope_tables(seq, DIFF_D)

    lo, hi = LOWRANK_OFF, LOWRANK_OFF + LOWRANK_W
    merge_off = IN_COLS - MERGE_W
    w_parts = (w_in[:, :, merge_off:].astype(BF16), w_in[:, :, :lo].astype(BF16),
               w_in[:, :, hi:merge_off].astype(BF16))
    w_lr = w_in[:, :, lo:hi].astype(BF16)
    w_low = jnp.concatenate([w_lr, w_lr, w_lr, jnp.zeros_like(w_lr)], axis=-1)
    gu = gla_gate_up.reshape(depth, 2, GLA_RANK, GLA_HEADS, GLA_DK)
    zeros = jnp.zeros_like(gu[:, 0])
    gu_f = jnp.concatenate([gu[:, 0], zeros], axis=-1).reshape(depth, GLA_RANK, GLA_HEADS * LANES)
    gu_b = jnp.concatenate([zeros, gu[:, 1]], axis=-1).reshape(depth, GLA_RANK, GLA_HEADS * LANES)
    gu_cat = jnp.concatenate([gu_f, gu_b], axis=1)
    gu_hi = gu_cat.astype(BF16)
    gu_lo = (gu_cat - gu_hi.astype(F32)).astype(BF16)
    gate_up = jnp.concatenate([gu_hi, gu_lo, gu_hi, jnp.zeros_like(gu_hi)], axis=1)
    gb = gla_gate_b.reshape(depth, 2, GLA_HEADS, GLA_DK)
    gate_b = jnp.concatenate([gb[:, 0], gb[:, 1]], axis=-1).reshape(depth, 1, GLA_HEADS * LANES)
    lam_init = jnp.asarray([0.8 - 0.6 * math.exp(-0.3 * layer) for layer in range(depth)],
                           F32).reshape(depth, 1, 1)
    norm_g3 = norm_g[:, None, :]
    gla_norm_g3 = gla_norm_g[:, None, :]
    diff_norm_g3 = diff_norm_g[:, None, :]

    x2 = x.reshape(batch * seq, D_MODEL)
    for layer in range(depth):
        proj, la = _in_proj(x2, norm_g3, w_parts, w_low, gate_up, gate_b, cos, sin, layer, seq)
        ya = _gla(proj, la, gla_norm_g3, layer, batch, seq)
        yb = _diff(proj, diff_lambda, lam_init, diff_norm_g3, layer, batch, seq)
        yc = _dilated(proj, batch, seq)
        x2 = _merge(x2, ya, yb, yc, proj, w_branch, w_out, final_norm_g[None], layer, final=layer == depth - 1)
    return x2.reshape(batch, seq, D_MODEL)
```

```python
import functools
import math

import jax
import jax.numpy as jnp
from jax import lax
from jax.experimental import pallas as pl
from jax.experimental.pallas import tpu as pltpu

F32 = jnp.float32
BF16 = jnp.bfloat16

LANES = 128

D_MODEL = 1024
ROPE_THETA = 10000.0
NORM_EPS = 1e-6
BRANCH_W = D_MODEL // 2
N_BRANCH = 3

GLA_HEADS = 4
GLA_DV = BRANCH_W // GLA_HEADS
GLA_DK = GLA_DV // 2
GLA_RANK = 16
GLA_TAU = 16.0
GLA_CHUNK = 64
GLA_QK_W = GLA_HEADS * GLA_DK

DIFF_HEADS = 4
DIFF_D = BRANCH_W // (2 * DIFF_HEADS)
DIFF_QK_W = DIFF_HEADS * 2 * DIFF_D

DIL_HEADS = 8
DIL_HD = BRANCH_W // DIL_HEADS
DIL_W = DIL_HEADS * DIL_HD
DIL_PATTERNS = ((128, 1), (512, 4), (2048, 16))
DIL_HALF = 64
DIL_TQ = 128
NEG_INF = -1e30

MERGE_W = N_BRANCH * D_MODEL
LOWRANK_OFF = GLA_QK_W * 2 + BRANCH_W * 2
LOWRANK_W = 2 * GLA_RANK
IN_COLS = LOWRANK_OFF + LOWRANK_W + 2 * (DIFF_QK_W * 2 + BRANCH_W * 2) + MERGE_W
MAIN_W = IN_COLS - LOWRANK_W

OFF_AQ = MERGE_W
OFF_AK = OFF_AQ + GLA_QK_W
OFF_AV = OFF_AK + GLA_QK_W
OFF_AG = OFF_AV + BRANCH_W
OFF_BQ = OFF_AG + BRANCH_W
OFF_BK = OFF_BQ + DIFF_QK_W
OFF_BV = OFF_BK + DIFF_QK_W
OFF_BG = OFF_BV + BRANCH_W
OFF_CQ = OFF_BG + BRANCH_W
OFF_CK = OFF_CQ + DIL_W
OFF_CV = OFF_CK + DIL_W
OFF_CG = OFF_CV + BRANCH_W

VMEM_LIMIT = 56 * 1024 * 1024


def _dot(a, b):
    return jnp.dot(a, b, preferred_element_type=F32)


def _dot_nt(a, b):
    return lax.dot_general(a, b, (((1,), (1,)), ((), ())), preferred_element_type=F32)


def _silu(x):
    return 0.5 * x * (1.0 + jnp.tanh(0.5 * x))


def _lane_iota(shape):
    return lax.broadcasted_iota(jnp.int32, shape, len(shape) - 1)


def _split_bf16(x):
    hi = x.astype(BF16).astype(F32)
    return hi, x - hi


def _rope(x, cos, sin_signed):
    lane = _lane_iota(x.shape)
    first_half = (lane % 64) < 32
    partner = jnp.where(first_half, pltpu.roll(x, 96, 1), pltpu.roll(x, 32, 1))
    return x * cos + partner * sin_signed


_PLAIN, _ROPE_Q, _ROPE_K = 0, 1, 2
_W_MERGE, _W_A, _W_BC = 0, 1, 2
_PROJ_SEGMENTS = (
    (_W_MERGE, 0, MERGE_W // 2, _PLAIN, 0), (_W_MERGE, MERGE_W // 2, MERGE_W // 2, _PLAIN, MERGE_W // 2),
    (_W_A, 0, LOWRANK_OFF, _PLAIN, OFF_AQ),
    (_W_BC, 0, DIFF_QK_W, _ROPE_Q, OFF_BQ), (_W_BC, OFF_BK - OFF_BQ, DIFF_QK_W, _ROPE_K, OFF_BK),
    (_W_BC, OFF_BV - OFF_BQ, 2 * BRANCH_W, _PLAIN, OFF_BV),
    (_W_BC, OFF_CQ - OFF_BQ, DIL_W, _ROPE_Q, OFF_CQ), (_W_BC, OFF_CK - OFF_BQ, DIL_W, _ROPE_K, OFF_CK),
    (_W_BC, OFF_CV - OFF_BQ, 2 * BRANCH_W, _PLAIN, OFF_CV),
)
LOG2E = 1.4426950408889634
LN2 = 0.6931471805599453
QK_SCALE = DIFF_D ** -0.5 * LOG2E


def _in_proj_kernel(x_ref, g_ref, wm_ref, wa_ref, wbc_ref, wl_ref, gu_ref, gb_ref, cos_ref, sin_ref,
                    out_ref, la_ref):
    x = x_ref[...]
    ms = jnp.mean(x * x, axis=-1, keepdims=True)
    hb = (x * lax.rsqrt(ms + NORM_EPS) * g_ref[...]).astype(BF16)
    low = _dot(hb, wl_ref[...])
    low_hi, low_lo = _split_bf16(low)
    lhs = jnp.where(_lane_iota(low.shape) < 2 * LOWRANK_W, low_hi, low_lo).astype(BF16)
    z = _dot(lhs, gu_ref[...]) + gb_ref[...]
    log_sig = jnp.minimum(z, 0.0) - jnp.log1p(jnp.exp(-jnp.abs(z)))
    la_ref[...] = log_sig * (1.0 / GLA_TAU)
    w_refs = (wm_ref, wa_ref, wbc_ref)
    for which, src, width, kind, start in _PROJ_SEGMENTS:
        res = _dot(hb, w_refs[which][:, src:src + width])
        if kind != _PLAIN:
            cos, sin = cos_ref[...], sin_ref[...]
            tiles = [_rope(res[:, c:c + LANES], cos, sin) for c in range(0, width, LANES)]
            res = jnp.concatenate(tiles, axis=1)
            if kind == _ROPE_Q:
                res = res * QK_SCALE
        out_ref[:, start:start + width] = res.astype(BF16)


def _in_proj(x2, g, w_parts, w_low, gate_up, gate_b, cos, sin, layer, seq, tm=512):
    assert DIFF_D == DIL_HD and seq % tm == 0
    rows = x2.shape[0]
    resident = dict(pipeline_mode=pl.Buffered(1))
    pos_blocks = seq // tm
    return pl.pallas_call(
        _in_proj_kernel,
        grid=(rows // tm,),
        in_specs=[
            pl.BlockSpec((tm, D_MODEL), lambda i: (i, 0)),
            pl.BlockSpec((None, 1, D_MODEL), lambda i: (layer, 0, 0)),
            *[pl.BlockSpec((None,) + w.shape[1:], lambda i: (layer, 0, 0), **resident) for w in w_parts],
            pl.BlockSpec((None, D_MODEL, LANES), lambda i: (layer, 0, 0), **resident),
            pl.BlockSpec((None, LANES, 4 * LANES), lambda i: (layer, 0, 0), **resident),
            pl.BlockSpec((None, 1, 4 * LANES), lambda i: (layer, 0, 0)),
            pl.BlockSpec((tm, LANES), lambda i: (i % pos_blocks, 0)),
            pl.BlockSpec((tm, LANES), lambda i: (i % pos_blocks, 0)),
        ],
        out_specs=[
            pl.BlockSpec((tm, MAIN_W), lambda i: (i, 0)),
            pl.BlockSpec((tm, 4 * LANES), lambda i: (i, 0)),
        ],
        out_shape=[
            jax.ShapeDtypeStruct((rows, MAIN_W), BF16),
            jax.ShapeDtypeStruct((rows, 4 * LANES), F32),
        ],
        compiler_params=pltpu.CompilerParams(
            dimension_semantics=("arbitrary",), vmem_limit_bytes=VMEM_LIMIT),
        name="in_proj",
    )(x2, g, *w_parts, w_low, gate_up, gate_b, cos, sin)


def _gla_kernel(q_ref, k_ref, v_ref, g_ref, la_ref, gn_ref, out_ref,
                u_sc, d_sc, st_sc, qh_sc, kh_sc, att_sc, *, seq, unroll):
    C = GLA_CHUNK
    n_chunks = seq // C
    lane_c = _lane_iota((C, LANES))
    row_c = lax.broadcasted_iota(jnp.int32, (C, LANES), 0)
    fwd_c = lane_c < GLA_DK
    tri = jnp.where(fwd_c, row_c - lane_c, lane_c - GLA_DK - row_c) >= 0
    row_s = lax.broadcasted_iota(jnp.int32, (2 * C, LANES), 0)
    lane_s = _lane_iota((2 * C, LANES)) % C
    cum_op = (jnp.where(row_s < C, row_s - lane_s, lane_s - (row_s - C)) >= 0).astype(F32).astype(BF16)
    fwd_sq = _lane_iota((LANES, LANES)) < GLA_DK
    scale = GLA_DK ** -0.5

    def head_cols(hh):
        return slice(hh * LANES, (hh + 1) * LANES)

    def dup_heads(ref, rows):
        xx = ref[rows, :].astype(F32)
        rolled = pltpu.roll(xx, GLA_DK, 1)
        return jnp.where(fwd_c, xx, rolled), jnp.where(fwd_c, rolled, xx)

    def prep(n, carry):
        rows = pl.ds(pl.multiple_of(n * C, C), C)
        q2 = dup_heads(q_ref, rows)
        k2 = dup_heads(k_ref, rows)
        for hh in range(2):
            cols = head_cols(hh)
            la = la_ref[rows, cols]
            la_hi, la_lo = _split_bf16(la)
            sums = _dot(cum_op, jnp.concatenate([la_hi, la_lo], axis=0).astype(BF16))
            b = jnp.where(fwd_c, sums[:C], sums[C:])
            tot = jnp.sum(la, axis=0, keepdims=True)
            decay = jnp.exp(tot)
            qh_sc[hh, rows, :] = (q2[hh] * jnp.exp(b) * scale).astype(BF16)
            k_hat = k2[hh] * jnp.exp(-b)
            kh_sc[hh, n] = jnp.concatenate(
                [jnp.where(fwd_c, k_hat, 0.0), jnp.where(fwd_c, 0.0, k_hat)], axis=0).astype(BF16)
            k_end = (k_hat * decay).astype(BF16)
            v_t = v_ref[rows, cols].astype(F32).T.astype(BF16)
            u_sc[hh, n] = _dot(v_t, k_end)
            d_sc[hh, n] = jnp.broadcast_to(decay, (8, LANES))
        return carry

    lax.fori_loop(0, n_chunks, prep, 0, unroll=unroll)

    def scan(i, states):
        j = n_chunks - 1 - i
        new = []
        for hh in range(2):
            st_sc[hh, i] = states[hh]
            upd = jnp.where(fwd_sq, u_sc[hh, i], u_sc[hh, j])
            dec = jnp.where(fwd_sq[:8], d_sc[hh, i], d_sc[hh, j])[0:1]
            new.append(states[hh] * dec + upd)
        return tuple(new)

    zero = jnp.zeros((LANES, LANES), F32)
    lax.fori_loop(0, n_chunks, scan, (zero, zero))

    def attend(n, carry):
        rows = pl.ds(pl.multiple_of(n * C, C), C)
        for hh in range(2):
            att = _dot_nt(qh_sc[hh, rows, :], kh_sc[hh, n])
            att_sc[hh, rows, :] = jnp.where(tri, att, 0.0).astype(BF16)
        return carry

    lax.fori_loop(0, n_chunks, attend, 0, unroll=unroll)

    def emit(n, carry):
        rows = pl.ds(pl.multiple_of(n * C, C), C)
        for hh in range(2):
            cols = head_cols(hh)
            v = v_ref[rows, cols]
            o = _dot(att_sc[hh, rows, :], jnp.concatenate([v, v], axis=0))
            state = jnp.where(fwd_sq, st_sc[hh, n], st_sc[hh, n_chunks - 1 - n]).astype(BF16)
            o = o + _dot_nt(qh_sc[hh, rows, :], state)
            ms = jnp.mean(o * o, axis=-1, keepdims=True)
            y = o * lax.rsqrt(ms + NORM_EPS) * gn_ref[...]
            out_ref[rows, cols] = (y * _silu(g_ref[rows, cols].astype(F32))).astype(BF16)
        return carry

    lax.fori_loop(0, n_chunks, emit, 0, unroll=unroll)


def _gla(proj, la, gn, layer, batch, seq, unroll=32):
    n_chunks = seq // GLA_CHUNK
    pair = 2 * LANES
    return pl.pallas_call(
        functools.partial(_gla_kernel, seq=seq, unroll=unroll),
        grid=(batch, GLA_HEADS // 2),
        in_specs=[
            pl.BlockSpec((seq, LANES), lambda b, p: (b, OFF_AQ // LANES + p)),
            pl.BlockSpec((seq, LANES), lambda b, p: (b, OFF_AK // LANES + p)),
            pl.BlockSpec((seq, pair), lambda b, p: (b, OFF_AV // pair + p)),
            pl.BlockSpec((seq, pair), lambda b, p: (b, OFF_AG // pair + p)),
            pl.BlockSpec((seq, pair), lambda b, p: (b, p)),
            pl.BlockSpec((None, 1, LANES), lambda b, p: (layer, 0, 0)),
        ],
        out_specs=pl.BlockSpec((seq, pair), lambda b, p: (b, p)),
        out_shape=jax.ShapeDtypeStruct((batch * seq, BRANCH_W), BF16),
        scratch_shapes=[
            pltpu.VMEM((2, n_chunks, LANES, LANES), F32),
            pltpu.VMEM((2, n_chunks, 8, LANES), F32),
            pltpu.VMEM((2, n_chunks, LANES, LANES), F32),
            pltpu.VMEM((2, seq, LANES), BF16),
            pltpu.VMEM((2, n_chunks, LANES, LANES), BF16),
            pltpu.VMEM((2, seq, LANES), BF16),
        ],
        compiler_params=pltpu.CompilerParams(
            dimension_semantics=("arbitrary", "arbitrary"), vmem_limit_bytes=VMEM_LIMIT),
        name="gla",
    )(proj, proj, proj, proj, la, gn)


def _diff_kernel(q_ref, k_ref, v_ref, g_ref, dl_ref, li_ref, gn_ref, out_ref,
                 q_sc, s0_sc, s1_sc, pm0_sc, pm1_sc, *, seq, tq):
    n_blk = seq // tq
    half0 = _lane_iota((tq, LANES)) < DIFF_D

    def stack_q(i, carry):
        q = q_ref[pl.ds(pl.multiple_of(i * tq, tq), tq), :].astype(F32)
        q_sc[i] = jnp.concatenate([jnp.where(half0, q, 0.0), jnp.where(half0, 0.0, q)], axis=0).astype(BF16)
        return carry

    lax.fori_loop(0, n_blk, stack_q, 0)

    dl = dl_ref[...]
    lam_init = li_ref[...]
    lam = (jnp.exp(jnp.sum(dl[0:1] * dl[1:2], axis=-1, keepdims=True))
           - jnp.exp(jnp.sum(dl[2:3] * dl[3:4], axis=-1, keepdims=True)) + lam_init)

    def scores(i, s_sc, pm_sc):
        s = _dot_nt(q_sc[i], k_ref[...])
        s_sc[...] = s
        pm_sc[...] = functools.reduce(jnp.maximum, [s[:, c:c + LANES] for c in range(0, seq, LANES)])

    def finish(i, s_sc, pm_sc):
        rows = pl.ds(i * tq, tq)
        m = jnp.max(pm_sc[...], axis=-1, keepdims=True)
        e = jnp.exp2(s_sc[...] - m)
        l = jnp.sum(e, axis=-1, keepdims=True)
        o2 = _dot(e.astype(BF16), v_ref[...]) * (1.0 / l)
        o = o2[:tq] - lam * o2[tq:]
        ms = jnp.mean(o * o, axis=-1, keepdims=True)
        y = o * lax.rsqrt(ms + NORM_EPS) * gn_ref[...] * (1.0 - lam_init)
        out_ref[rows, :] = (y * _silu(g_ref[rows, :].astype(F32))).astype(BF16)

    bufs = ((s0_sc, pm0_sc), (s1_sc, pm1_sc))
    scores(0, *bufs[0])
    for i in range(n_blk):
        if i + 1 < n_blk:
            scores(i + 1, *bufs[(i + 1) % 2])
        finish(i, *bufs[i % 2])


def _diff(proj, dl, lam_init, gn, layer, batch, seq, tq=256):
    blk = lambda off: pl.BlockSpec((seq, LANES), lambda b, h: (b, off // LANES + h))
    per_layer = lambda shape: pl.BlockSpec((None,) + shape, lambda b, h: (layer, 0, 0))
    return pl.pallas_call(
        functools.partial(_diff_kernel, seq=seq, tq=tq),
        grid=(batch, DIFF_HEADS),
        in_specs=[blk(OFF_BQ), blk(OFF_BK), blk(OFF_BV), blk(OFF_BG),
                  per_layer((4, DIFF_D)), per_layer((1, 1)), per_layer((1, LANES))],
        out_specs=pl.BlockSpec((seq, LANES), lambda b, h: (b, h)),
        out_shape=jax.ShapeDtypeStruct((batch * seq, BRANCH_W), BF16),
        scratch_shapes=[pltpu.VMEM((seq // tq, 2 * tq, LANES), BF16),
                        pltpu.VMEM((2 * tq, seq), F32), pltpu.VMEM((2 * tq, seq), F32),
                        pltpu.VMEM((2 * tq, LANES), F32), pltpu.VMEM((2 * tq, LANES), F32)],
        compiler_params=pltpu.CompilerParams(
            dimension_semantics=("arbitrary", "arbitrary"), vmem_limit_bytes=VMEM_LIMIT),
        name="diff",
    )(proj, proj, proj, proj, dl, lam_init, gn)


def _dilated_kernel(q_ref, k_ref, v_ref, g_ref, out_ref,
                    nat_sc, c4_sc, k4_sc, q16_sc, k16_sc, w1_sc, w4_sc, w16_sc,
                    s0_sc, s1_sc, s2_sc, e0_sc, e1_sc, e2_sc, m0_sc, m1_sc, m2_sc,
                    o_sc, l_sc, bias_sc, *, seq, unroll):
    tq, win = DIL_TQ, 2 * DIL_TQ
    stage_rows = 2 * tq
    n_stage = seq // stage_rows
    len4, len16 = seq // 4, seq // 16
    head0 = _lane_iota((tq, LANES)) < DIL_HD
    head0_stage = _lane_iota((stage_rows, LANES)) < DIL_HD

    dcol = (lax.broadcasted_iota(jnp.int32, (tq, win), 1) - lax.broadcasted_iota(jnp.int32, (tq, win), 0))
    for idx in range(3):
        bias_sc[idx] = jnp.where(jnp.abs(dcol - idx * DIL_HALF) <= DIL_HALF, 0.0, NEG_INF)

    def stage(i, carry):
        rows = pl.ds(pl.multiple_of(i * stage_rows, stage_rows), stage_rows)
        nat_sc[0, rows, :] = q_ref[rows, :].astype(F32)
        nat_sc[1, rows, :] = k_ref[rows, :].astype(F32)
        nat_sc[2, rows, :] = v_ref[rows, :].astype(F32)
        return carry

    lax.fori_loop(0, n_stage, stage, 0)

    for t in range(3):
        for r in range(4):
            c4_sc[t, r * len4:(r + 1) * len4, :] = nat_sc[t, pl.ds(r, len4, stride=4), :]
    for r4 in range(4):
        for j in range(4):
            dst = slice((r4 + 4 * j) * len16, (r4 + 4 * j + 1) * len16)
            src = pl.ds(r4 * len4 + j, len16, stride=4)
            q16_sc[dst, :] = c4_sc[0, src, :]
            k16_sc[dst, :] = c4_sc[1, src, :].astype(BF16)
            w16_sc[0, dst, :] = c4_sc[2, src, :].astype(BF16)

    def value_weights(i, carry):
        rows = pl.ds(pl.multiple_of(i * stage_rows, stage_rows), stage_rows)
        k4_sc[rows, :] = c4_sc[1, rows, :].astype(BF16)
        for src, dst in ((v_ref, w1_sc), (c4_sc.at[2], w4_sc), (w16_sc.at[0], w16_sc)):
            v = src[rows, :].astype(F32)
            dst[1, rows, :] = jnp.where(head0_stage, 1.0, v).astype(BF16)
            dst[0, rows, :] = jnp.where(head0_stage, v, 1.0).astype(BF16)
        return carry

    lax.fori_loop(0, n_stage, value_weights, 0)

    def make_pattern(g, q_src, k_src, w_src, s_sc, e_sc, m_sc, seq_len, w, out_rows):
        per_seq = seq_len // tq

        def where(t):
            r, i = t // per_seq, t % per_seq
            ks = jnp.clip(i * tq - tq // 2, 0, seq_len - w)
            which = jnp.where(i == 0, 0, jnp.where(i == per_seq - 1, 2, 1)) if per_seq > 1 else 0
            qrows = pl.ds(pl.multiple_of(r * seq_len + i * tq, tq), tq)
            krows = pl.ds(pl.multiple_of(r * seq_len + ks, tq // 2), w)
            return qrows, krows, which

        def scores(t):
            qrows, krows, which = where(t)
            q = q_src[qrows, :]
            q2 = jnp.concatenate([jnp.where(head0, q, 0.0), jnp.where(head0, 0.0, q)], axis=0)
            bias = bias_sc[which, :, :w]
            s = _dot_nt(q2.astype(BF16), k_src[krows, :])
            s_sc[t, :, :w] = s + jnp.concatenate([bias, bias], axis=0)

        def softmax(t):
            s = s_sc[t, :, :w]
            m = jnp.max(s, axis=-1, keepdims=True)
            e_sc[t, :, :w] = jnp.exp2(s - m).astype(BF16)
            m_sc[t] = jnp.where(head0, m[:tq], m[tq:])

        def values(t):
            _, krows, _ = where(t)
            out0 = _dot(e_sc[t, :tq, :w], w_src[0, krows, :])
            out1 = _dot(e_sc[t, tq:, :w], w_src[1, krows, :])
            l = pltpu.roll(jnp.where(head0, out1, out0), DIL_HD, 1)
            rows = out_rows(t)
            o_sc[g, rows, :] = jnp.where(head0, out0, out1) * (1.0 / l)
            l_sc[g, rows, :] = m_sc[t] * LN2 + jnp.log(l)

        return scores, softmax, values

    sc0, sm0, va0 = make_pattern(0, nat_sc.at[0], k_ref, w1_sc, s0_sc, e0_sc, m0_sc, seq, win,
                                 lambda t: pl.ds(pl.multiple_of(t * tq, tq), tq))
    sc1, sm1, va1 = make_pattern(1, c4_sc.at[0], k4_sc, w4_sc, s1_sc, e1_sc, m1_sc, len4, win,
                                 lambda t: pl.ds(pl.multiple_of(t * tq, tq), tq))
    sc2, sm2, va2 = make_pattern(2, q16_sc, k16_sc, w16_sc, s2_sc, e2_sc, m2_sc, len16, len16,
                                 lambda t: pl.ds((t % 4) * len4 + t // 4, len16, stride=4))

    for passes in ((sc0,), (sm0, sc1), (va0, sm1, sc2), (va1, sm2), (va2,)):
        def body(t, carry, passes=passes):
            for f in passes:
                f(t)
            return carry

        lax.fori_loop(0, seq // tq, body, 0, unroll=unroll)

    y_sc = nat_sc.at[0]
    per_class = len4 // stage_rows

    def combine(i, carry):
        r, m0 = i // per_class, (i % per_class) * stage_rows
        rows = pl.ds(pl.multiple_of(r * len4 + m0, stage_rows), stage_rows)
        nat_rows = pl.ds(r + 4 * m0, stage_rows, stride=4)
        l0, l1, l2 = l_sc[0, nat_rows, :], l_sc[1, rows, :], l_sc[2, rows, :]
        m = jnp.maximum(jnp.maximum(l0, l1), l2)
        w0, w1, w2 = jnp.exp(l0 - m), jnp.exp(l1 - m), jnp.exp(l2 - m)
        o = w0 * o_sc[0, nat_rows, :] + w1 * o_sc[1, rows, :] + w2 * o_sc[2, rows, :]
        y_sc[nat_rows, :] = o * (1.0 / (w0 + w1 + w2))
        return carry

    lax.fori_loop(0, n_stage, combine, 0)

    def gate(i, carry):
        rows = pl.ds(pl.multiple_of(i * stage_rows, stage_rows), stage_rows)
        out_ref[rows, :] = (y_sc[rows, :] * _silu(g_ref[rows, :].astype(F32))).astype(BF16)
        return carry

    lax.fori_loop(0, n_stage, gate, 0)


def _dilated(proj, batch, seq, unroll=16):
    assert seq // 16 == DIL_TQ and all(w // (2 * d) == DIL_HALF for w, d in DIL_PATTERNS)
    blk = lambda off: pl.BlockSpec((seq, LANES), lambda b, p: (b, off // LANES + p))
    n_blocks = seq // DIL_TQ
    f32_rows = pltpu.VMEM((seq, LANES), F32)
    bf16_rows = pltpu.VMEM((seq, LANES), BF16)
    bf16_pair = pltpu.VMEM((2, seq, LANES), BF16)
    s_buf = pltpu.VMEM((n_blocks, 2 * DIL_TQ, 2 * DIL_TQ), F32)
    e_buf = pltpu.VMEM((n_blocks, 2 * DIL_TQ, 2 * DIL_TQ), BF16)
    m_buf = pltpu.VMEM((n_blocks, DIL_TQ, LANES), F32)
    return pl.pallas_call(
        functools.partial(_dilated_kernel, seq=seq, unroll=unroll),
        grid=(batch, DIL_HEADS // 2),
        in_specs=[blk(OFF_CQ), blk(OFF_CK), blk(OFF_CV), blk(OFF_CG)],
        out_specs=pl.BlockSpec((seq, LANES), lambda b, p: (b, p)),
        out_shape=jax.ShapeDtypeStruct((batch * seq, BRANCH_W), BF16),
        scratch_shapes=[
            pltpu.VMEM((3, seq, LANES), F32),
            pltpu.VMEM((3, seq, LANES), F32), bf16_rows, f32_rows, bf16_rows,
            bf16_pair, bf16_pair, bf16_pair,
            s_buf, s_buf, s_buf, e_buf, e_buf, e_buf, m_buf, m_buf, m_buf,
            pltpu.VMEM((3, seq, LANES), F32), pltpu.VMEM((3, seq, LANES), F32),
            pltpu.VMEM((3, DIL_TQ, 2 * DIL_TQ), F32),
        ],
        compiler_params=pltpu.CompilerParams(
            dimension_semantics=("arbitrary", "arbitrary"), vmem_limit_bytes=VMEM_LIMIT),
        name="dilated",
    )(proj, proj, proj, proj)


def _merge_kernel(x_ref, ya_ref, yb_ref, yc_ref, m_ref, wb_ref, wo_ref, fg_ref, out_ref, wb_sc, wo_sc,
                  *, final):
    @pl.when(pl.program_id(0) == 0)
    def _():
        wb_sc[...] = wb_ref[...].astype(BF16)
        wo_sc[...] = wo_ref[...].astype(BF16)

    mixed = None
    for gi, y_ref in enumerate((ya_ref, yb_ref, yc_ref)):
        z = _dot(y_ref[...], wb_sc[gi])
        gate = 0.5 * jnp.tanh(0.5 * m_ref[:, gi * D_MODEL:(gi + 1) * D_MODEL].astype(F32)) + 0.5
        mixed = gate * z if mixed is None else mixed + gate * z
    x = x_ref[...] + _dot(mixed.astype(BF16), wo_sc[...])
    if final:
        ms = jnp.mean(x * x, axis=-1, keepdims=True)
        x = x * lax.rsqrt(ms + NORM_EPS) * fg_ref[...]
    out_ref[...] = x


def _merge(x2, ya, yb, yc, proj, wb, wo, final_g, layer, final, tm=1024):
    rows = x2.shape[0]
    yspec = pl.BlockSpec((tm, BRANCH_W), lambda i: (i, 0))
    return pl.pallas_call(
        functools.partial(_merge_kernel, final=final),
        grid=(rows // tm,),
        in_specs=[
            pl.BlockSpec((tm, D_MODEL), lambda i: (i, 0)),
            yspec, yspec, yspec,
            pl.BlockSpec((tm, MERGE_W), lambda i: (i, 0)),
            pl.BlockSpec((None, N_BRANCH, BRANCH_W, D_MODEL), lambda i: (layer, 0, 0, 0),
                         pipeline_mode=pl.Buffered(1)),
            pl.BlockSpec((None, D_MODEL, D_MODEL), lambda i: (layer, 0, 0), pipeline_mode=pl.Buffered(1)),
            pl.BlockSpec((1, D_MODEL), lambda i: (0, 0)),
        ],
        out_specs=pl.BlockSpec((tm, D_MODEL), lambda i: (i, 0)),
        out_shape=jax.ShapeDtypeStruct((rows, D_MODEL), F32),
        scratch_shapes=[pltpu.VMEM((N_BRANCH, BRANCH_W, D_MODEL), BF16), pltpu.VMEM((D_MODEL, D_MODEL), BF16)],
        compiler_params=pltpu.CompilerParams(
            dimension_semantics=("arbitrary",), vmem_limit_bytes=VMEM_LIMIT),
        name="merge",
    )(x2, ya, yb, yc, proj, wb, wo, final_g)


def _rope_tables(seq, dim):
    inv = 1.0 / (ROPE_THETA ** (jnp.arange(0, dim, 2, dtype=F32) / dim))
    ang = jnp.arange(seq, dtype=F32)[:, None] * inv[None, :]
    cos, sin = jnp.cos(ang), jnp.sin(ang)
    reps = LANES // dim
    cos_t = jnp.tile(jnp.concatenate([cos, cos], axis=-1), (1, reps))
    sin_t = jnp.tile(jnp.concatenate([-sin, sin], axis=-1), (1, reps))
    return cos_t, sin_t


def kernel(x, norm_g, w_in, gla_gate_up, gla_gate_b, gla_norm_g, diff_lambda, diff_norm_g, w_branch, w_out, final_norm_g):
    batch, seq, _ = x.shape
    depth = w_in.shape[0]
    cos, sin = _rope_tables(seq, DIFF_D)

    lo, hi = LOWRANK_OFF, LOWRANK_OFF + LOWRANK_W
    merge_off = IN_COLS - MERGE_W
    w_parts = (w_in[:, :, merge_off:].astype(BF16), w_in[:, :, :lo].astype(BF16),
               w_in[:, :, hi:merge_off].astype(BF16))
    w_lr = w_in[:, :, lo:hi].astype(BF16)
    w_low = jnp.concatenate([w_lr, w_lr, w_lr, jnp.zeros_like(w_lr)], axis=-1)
    gu = gla_gate_up.reshape(depth, 2, GLA_RANK, GLA_HEADS, GLA_DK)
    zeros = jnp.zeros_like(gu[:, 0])
    gu_f = jnp.concatenate([gu[:, 0], zeros], axis=-1).reshape(depth, GLA_RANK, GLA_HEADS * LANES)
    gu_b = jnp.concatenate([zeros, gu[:, 1]], axis=-1).reshape(depth, GLA_RANK, GLA_HEADS * LANES)
    gu_cat = jnp.concatenate([gu_f, gu_b], axis=1)
    gu_hi = gu_cat.astype(BF16)
    gu_lo = (gu_cat - gu_hi.astype(F32)).astype(BF16)
    gate_up = jnp.concatenate([gu_hi, gu_lo, gu_hi, jnp.zeros_like(gu_hi)], axis=1)
    gb = gla_gate_b.reshape(depth, 2, GLA_HEADS, GLA_DK)
    gate_b = jnp.concatenate([gb[:, 0], gb[:, 1]], axis=-1).reshape(depth, 1, GLA_HEADS * LANES)
    lam_init = jnp.asarray([0.8 - 0.6 * math.exp(-0.3 * layer) for layer in range(depth)],
                           F32).reshape(depth, 1, 1)
    norm_g3 = norm_g[:, None, :]
    gla_norm_g3 = gla_norm_g[:, None, :]
    diff_norm_g3 = diff_norm_g[:, None, :]

    x2 = x.reshape(batch * seq, D_MODEL)
    for layer in range(depth):
        proj, la = _in_proj(x2, norm_g3, w_parts, w_low, gate_up, gate_b, cos, sin, layer, seq)
        ya = _gla(proj, la, gla_norm_g3, layer, batch, seq)
        yb = _diff(proj, diff_lambda, lam_init, diff_norm_g3, layer, batch, seq)
        yc = _dilated(proj, batch, seq)
        x2 = _merge(x2, ya, yb, yc, proj, w_branch, w_out, final_norm_g[None], layer, final=layer == depth - 1)
    return x2.reshape(batch, seq, D_MODEL)
```

```python
import functools
import math

import jax
import jax.numpy as jnp
from jax import lax
from jax.experimental import pallas as pl
from jax.experimental.pallas import tpu as pltpu

F32 = jnp.float32
BF16 = jnp.bfloat16

LANES = 128

D_MODEL = 1024
ROPE_THETA = 10000.0
NORM_EPS = 1e-6
BRANCH_W = D_MODEL // 2
N_BRANCH = 3

GLA_HEADS = 4
GLA_DV = BRANCH_W // GLA_HEADS
GLA_DK = GLA_DV // 2
GLA_RANK = 16
GLA_TAU = 16.0
GLA_CHUNK = 64
GLA_QK_W = GLA_HEADS * GLA_DK

DIFF_HEADS = 4
DIFF_D = BRANCH_W // (2 * DIFF_HEADS)
DIFF_QK_W = DIFF_HEADS * 2 * DIFF_D

DIL_HEADS = 8
DIL_HD = BRANCH_W // DIL_HEADS
DIL_W = DIL_HEADS * DIL_HD
DIL_PATTERNS = ((128, 1), (512, 4), (2048, 16))
DIL_HALF = 64
DIL_TQ = 128
NEG_INF = -1e30

MERGE_W = N_BRANCH * D_MODEL
LOWRANK_OFF = GLA_QK_W * 2 + BRANCH_W * 2
LOWRANK_W = 2 * GLA_RANK
IN_COLS = LOWRANK_OFF + LOWRANK_W + 2 * (DIFF_QK_W * 2 + BRANCH_W * 2) + MERGE_W
MAIN_W = IN_COLS - LOWRANK_W

OFF_AQ = MERGE_W
OFF_AK = OFF_AQ + GLA_QK_W
OFF_AV = OFF_AK + GLA_QK_W
OFF_AG = OFF_AV + BRANCH_W
OFF_BQ = OFF_AG + BRANCH_W
OFF_BK = OFF_BQ + DIFF_QK_W
OFF_BV = OFF_BK + DIFF_QK_W
OFF_BG = OFF_BV + BRANCH_W
OFF_CQ = OFF_BG + BRANCH_W
OFF_CK = OFF_CQ + DIL_W
OFF_CV = OFF_CK + DIL_W
OFF_CG = OFF_CV + BRANCH_W

VMEM_LIMIT = 56 * 1024 * 1024


def _dot(a, b):
    return jnp.dot(a, b, preferred_element_type=F32)


def _dot_nt(a, b):
    return lax.dot_general(a, b, (((1,), (1,)), ((), ())), preferred_element_type=F32)


def _silu(x):
    return 0.5 * x * (1.0 + jnp.tanh(0.5 * x))


def _lane_iota(shape):
    return lax.broadcasted_iota(jnp.int32, shape, len(shape) - 1)


def _split_bf16(x):
    hi = x.astype(BF16).astype(F32)
    return hi, x - hi


def _rope(x, cos, sin_signed):
    lane = _lane_iota(x.shape)
    first_half = (lane % 64) < 32
    partner = jnp.where(first_half, pltpu.roll(x, 96, 1), pltpu.roll(x, 32, 1))
    return x * cos + partner * sin_signed


_PLAIN, _ROPE_Q, _ROPE_K = 0, 1, 2
_W_MERGE, _W_A, _W_BC = 0, 1, 2
_PROJ_SEGMENTS = (
    (_W_MERGE, 0, MERGE_W // 2, _PLAIN, 0), (_W_MERGE, MERGE_W // 2, MERGE_W // 2, _PLAIN, MERGE_W // 2),
    (_W_A, 0, LOWRANK_OFF, _PLAIN, OFF_AQ),
    (_W_BC, 0, DIFF_QK_W, _ROPE_Q, OFF_BQ), (_W_BC, OFF_BK - OFF_BQ, DIFF_QK_W, _ROPE_K, OFF_BK),
    (_W_BC, OFF_BV - OFF_BQ, 2 * BRANCH_W, _PLAIN, OFF_BV),
    (_W_BC, OFF_CQ - OFF_BQ, DIL_W, _ROPE_Q, OFF_CQ), (_W_BC, OFF_CK - OFF_BQ, DIL_W, _ROPE_K, OFF_CK),
    (_W_BC, OFF_CV - OFF_BQ, 2 * BRANCH_W, _PLAIN, OFF_CV),
)
LOG2E = 1.4426950408889634
QK_SCALE = DIFF_D ** -0.5 * LOG2E


def _in_proj_kernel(x_ref, g_ref, wm_ref, wa_ref, wbc_ref, wl_ref, gu_ref, gb_ref, cos_ref, sin_ref,
                    out_ref, la_ref):
    x = x_ref[...]
    ms = jnp.mean(x * x, axis=-1, keepdims=True)
    hb = (x * lax.rsqrt(ms + NORM_EPS) * g_ref[...]).astype(BF16)
    low = _dot(hb, wl_ref[...])
    low_hi, low_lo = _split_bf16(low)
    lhs = jnp.where(_lane_iota(low.shape) < 2 * LOWRANK_W, low_hi, low_lo).astype(BF16)
    z = _dot(lhs, gu_ref[...]) + gb_ref[...]
    log_sig = jnp.minimum(z, 0.0) - jnp.log1p(jnp.exp(-jnp.abs(z)))
    la_ref[...] = log_sig * (1.0 / GLA_TAU)
    w_refs = (wm_ref, wa_ref, wbc_ref)
    for which, src, width, kind, start in _PROJ_SEGMENTS:
        res = _dot(hb, w_refs[which][:, src:src + width])
        if kind != _PLAIN:
            cos, sin = cos_ref[...], sin_ref[...]
            tiles = [_rope(res[:, c:c + LANES], cos, sin) for c in range(0, width, LANES)]
            res = jnp.concatenate(tiles, axis=1)
            if kind == _ROPE_Q:
                res = res * QK_SCALE
        out_ref[:, start:start + width] = res.astype(BF16)


def _in_proj(x2, g, w_parts, w_low, gate_up, gate_b, cos, sin, layer, seq, tm=512):
    assert DIFF_D == DIL_HD and seq % tm == 0
    rows = x2.shape[0]
    resident = dict(pipeline_mode=pl.Buffered(1))
    pos_blocks = seq // tm
    return pl.pallas_call(
        _in_proj_kernel,
        grid=(rows // tm,),
        in_specs=[
            pl.BlockSpec((tm, D_MODEL), lambda i: (i, 0)),
            pl.BlockSpec((None, 1, D_MODEL), lambda i: (layer, 0, 0)),
            *[pl.BlockSpec((None,) + w.shape[1:], lambda i: (layer, 0, 0), **resident) for w in w_parts],
            pl.BlockSpec((None, D_MODEL, LANES), lambda i: (layer, 0, 0), **resident),
            pl.BlockSpec((None, LANES, 4 * LANES), lambda i: (layer, 0, 0), **resident),
            pl.BlockSpec((None, 1, 4 * LANES), lambda i: (layer, 0, 0)),
            pl.BlockSpec((tm, LANES), lambda i: (i % pos_blocks, 0)),
            pl.BlockSpec((tm, LANES), lambda i: (i % pos_blocks, 0)),
        ],
        out_specs=[
            pl.BlockSpec((tm, MAIN_W), lambda i: (i, 0)),
            pl.BlockSpec((tm, 4 * LANES), lambda i: (i, 0)),
        ],
        out_shape=[
            jax.ShapeDtypeStruct((rows, MAIN_W), BF16),
            jax.ShapeDtypeStruct((rows, 4 * LANES), F32),
        ],
        compiler_params=pltpu.CompilerParams(
            dimension_semantics=("arbitrary",), vmem_limit_bytes=VMEM_LIMIT),
        name="in_proj",
    )(x2, g, *w_parts, w_low, gate_up, gate_b, cos, sin)


def _gla_kernel(q_ref, k_ref, v_ref, g_ref, la_ref, gn_ref, out_ref,
                u_sc, d_sc, st_sc, qh_sc, kh_sc, att_sc, *, seq, unroll):
    C = GLA_CHUNK
    n_chunks = seq // C
    lane_c = _lane_iota((C, LANES))
    row_c = lax.broadcasted_iota(jnp.int32, (C, LANES), 0)
    fwd_c = lane_c < GLA_DK
    tri = jnp.where(fwd_c, row_c - lane_c, lane_c - GLA_DK - row_c) >= 0
    row_s = lax.broadcasted_iota(jnp.int32, (2 * C, LANES), 0)
    lane_s = _lane_iota((2 * C, LANES)) % C
    cum_op = (jnp.where(row_s < C, row_s - lane_s, lane_s - (row_s - C)) >= 0).astype(F32).astype(BF16)
    fwd_sq = _lane_iota((LANES, LANES)) < GLA_DK
    scale = GLA_DK ** -0.5

    def head_cols(hh):
        return slice(hh * LANES, (hh + 1) * LANES)

    def dup_heads(ref, rows):
        xx = ref[rows, :].astype(F32)
        rolled = pltpu.roll(xx, GLA_DK, 1)
        return jnp.where(fwd_c, xx, rolled), jnp.where(fwd_c, rolled, xx)

    def prep(n, carry):
        rows = pl.ds(pl.multiple_of(n * C, C), C)
        q2 = dup_heads(q_ref, rows)
        k2 = dup_heads(k_ref, rows)
        for hh in range(2):
            cols = head_cols(hh)
            la = la_ref[rows, cols]
            la_hi, la_lo = _split_bf16(la)
            sums = _dot(cum_op, jnp.concatenate([la_hi, la_lo], axis=0).astype(BF16))
            b = jnp.where(fwd_c, sums[:C], sums[C:])
            tot = jnp.sum(la, axis=0, keepdims=True)
            decay = jnp.exp(tot)
            qh_sc[hh, rows, :] = (q2[hh] * jnp.exp(b) * scale).astype(BF16)
            k_hat = k2[hh] * jnp.exp(-b)
            kh_sc[hh, n] = jnp.concatenate(
                [jnp.where(fwd_c, k_hat, 0.0), jnp.where(fwd_c, 0.0, k_hat)], axis=0).astype(BF16)
            k_end = (k_hat * decay).astype(BF16)
            v_t = v_ref[rows, cols].astype(F32).T.astype(BF16)
            u_sc[hh, n] = _dot(v_t, k_end)
            d_sc[hh, n] = jnp.broadcast_to(decay, (8, LANES))
        return carry

    lax.fori_loop(0, n_chunks, prep, 0, unroll=unroll)

    def scan(i, states):
        j = n_chunks - 1 - i
        new = []
        for hh in range(2):
            st_sc[hh, i] = states[hh]
            upd = jnp.where(fwd_sq, u_sc[hh, i], u_sc[hh, j])
            dec = jnp.where(fwd_sq[:8], d_sc[hh, i], d_sc[hh, j])[0:1]
            new.append(states[hh] * dec + upd)
        return tuple(new)

    zero = jnp.zeros((LANES, LANES), F32)
    lax.fori_loop(0, n_chunks, scan, (zero, zero))

    def attend(n, carry):
        rows = pl.ds(pl.multiple_of(n * C, C), C)
        for hh in range(2):
            att = _dot_nt(qh_sc[hh, rows, :], kh_sc[hh, n])
            att_sc[hh, rows, :] = jnp.where(tri, att, 0.0).astype(BF16)
        return carry

    lax.fori_loop(0, n_chunks, attend, 0, unroll=unroll)

    def emit(n, carry):
        rows = pl.ds(pl.multiple_of(n * C, C), C)
        for hh in range(2):
            cols = head_cols(hh)
            v = v_ref[rows, cols]
            o = _dot(att_sc[hh, rows, :], jnp.concatenate([v, v], axis=0))
            state = jnp.where(fwd_sq, st_sc[hh, n], st_sc[hh, n_chunks - 1 - n]).astype(BF16)
            o = o + _dot_nt(qh_sc[hh, rows, :], state)
            ms = jnp.mean(o * o, axis=-1, keepdims=True)
            y = o * lax.rsqrt(ms + NORM_EPS) * gn_ref[...]
            out_ref[rows, cols] = (y * _silu(g_ref[rows, cols].astype(F32))).astype(BF16)
        return carry

    lax.fori_loop(0, n_chunks, emit, 0, unroll=unroll)


def _gla(proj, la, gn, layer, batch, seq, unroll=32):
    n_chunks = seq // GLA_CHUNK
    pair = 2 * LANES
    return pl.pallas_call(
        functools.partial(_gla_kernel, seq=seq, unroll=unroll),
        grid=(batch, GLA_HEADS // 2),
        in_specs=[
            pl.BlockSpec((seq, LANES), lambda b, p: (b, OFF_AQ // LANES + p)),
            pl.BlockSpec((seq, LANES), lambda b, p: (b, OFF_AK // LANES + p)),
            pl.BlockSpec((seq, pair), lambda b, p: (b, OFF_AV // pair + p)),
            pl.BlockSpec((seq, pair), lambda b, p: (b, OFF_AG // pair + p)),
            pl.BlockSpec((seq, pair), lambda b, p: (b, p)),
            pl.BlockSpec((None, 1, LANES), lambda b, p: (layer, 0, 0)),
        ],
        out_specs=pl.BlockSpec((seq, pair), lambda b, p: (b, p)),
        out_shape=jax.ShapeDtypeStruct((batch * seq, BRANCH_W), BF16),
        scratch_shapes=[
            pltpu.VMEM((2, n_chunks, LANES, LANES), F32),
            pltpu.VMEM((2, n_chunks, 8, LANES), F32),
            pltpu.VMEM((2, n_chunks, LANES, LANES), F32),
            pltpu.VMEM((2, seq, LANES), BF16),
            pltpu.VMEM((2, n_chunks, LANES, LANES), BF16),
            pltpu.VMEM((2, seq, LANES), BF16),
        ],
        compiler_params=pltpu.CompilerParams(
            dimension_semantics=("arbitrary", "arbitrary"), vmem_limit_bytes=VMEM_LIMIT),
        name="gla",
    )(proj, proj, proj, proj, la, gn)


def _diff_kernel(q_ref, k_ref, v_ref, g_ref, dl_ref, li_ref, gn_ref, out_ref,
                 q_sc, s0_sc, s1_sc, pm0_sc, pm1_sc, *, seq, tq):
    n_blk = seq // tq
    half0 = _lane_iota((tq, LANES)) < DIFF_D

    def stack_q(i, carry):
        q = q_ref[pl.ds(pl.multiple_of(i * tq, tq), tq), :].astype(F32)
        q_sc[i] = jnp.concatenate([jnp.where(half0, q, 0.0), jnp.where(half0, 0.0, q)], axis=0).astype(BF16)
        return carry

    lax.fori_loop(0, n_blk, stack_q, 0)

    dl = dl_ref[...]
    lam_init = li_ref[...]
    lam = (jnp.exp(jnp.sum(dl[0:1] * dl[1:2], axis=-1, keepdims=True))
           - jnp.exp(jnp.sum(dl[2:3] * dl[3:4], axis=-1, keepdims=True)) + lam_init)

    def scores(i, s_sc, pm_sc):
        s = _dot_nt(q_sc[i], k_ref[...])
        s_sc[...] = s
        pm_sc[...] = functools.reduce(jnp.maximum, [s[:, c:c + LANES] for c in range(0, seq, LANES)])

    def finish(i, s_sc, pm_sc):
        rows = pl.ds(i * tq, tq)
        m = jnp.max(pm_sc[...], axis=-1, keepdims=True)
        e = jnp.exp2(s_sc[...] - m)
        l = jnp.sum(e, axis=-1, keepdims=True)
        o2 = _dot(e.astype(BF16), v_ref[...]) * (1.0 / l)
        o = o2[:tq] - lam * o2[tq:]
        ms = jnp.mean(o * o, axis=-1, keepdims=True)
        y = o * lax.rsqrt(ms + NORM_EPS) * gn_ref[...] * (1.0 - lam_init)
        out_ref[rows, :] = (y * _silu(g_ref[rows, :].astype(F32))).astype(BF16)

    bufs = ((s0_sc, pm0_sc), (s1_sc, pm1_sc))
    scores(0, *bufs[0])
    for i in range(n_blk):
        if i + 1 < n_blk:
            scores(i + 1, *bufs[(i + 1) % 2])
        finish(i, *bufs[i % 2])


def _diff(proj, dl, lam_init, gn, layer, batch, seq, tq=256):
    blk = lambda off: pl.BlockSpec((seq, LANES), lambda b, h: (b, off // LANES + h))
    per_layer = lambda shape: pl.BlockSpec((None,) + shape, lambda b, h: (layer, 0, 0))
    return pl.pallas_call(
        functools.partial(_diff_kernel, seq=seq, tq=tq),
        grid=(batch, DIFF_HEADS),
        in_specs=[blk(OFF_BQ), blk(OFF_BK), blk(OFF_BV), blk(OFF_BG),
                  per_layer((4, DIFF_D)), per_layer((1, 1)), per_layer((1, LANES))],
        out_specs=pl.BlockSpec((seq, LANES), lambda b, h: (b, h)),
        out_shape=jax.ShapeDtypeStruct((batch * seq, BRANCH_W), BF16),
        scratch_shapes=[pltpu.VMEM((seq // tq, 2 * tq, LANES), BF16),
                        pltpu.VMEM((2 * tq, seq), F32), pltpu.VMEM((2 * tq, seq), F32),
                        pltpu.VMEM((2 * tq, LANES), F32), pltpu.VMEM((2 * tq, LANES), F32)],
        compiler_params=pltpu.CompilerParams(
            dimension_semantics=("arbitrary", "arbitrary"), vmem_limit_bytes=VMEM_LIMIT),
        name="diff",
    )(proj, proj, proj, proj, dl, lam_init, gn)


def _dilated_kernel(q_ref, k_ref, v_ref, g_ref, out_ref,
                    nat_sc, c4_sc, k4_sc, q16_sc, k16_sc, w1_sc, w4_sc, w16_sc,
                    s0_sc, s1_sc, s2_sc, e0_sc, e1_sc, e2_sc,
                    o_sc, m_sc, msw_sc, d_sc, bias_sc, *, seq, unroll):
    tq, win = DIL_TQ, 2 * DIL_TQ
    stage_rows = 2 * tq
    n_stage = seq // stage_rows
    len4, len16 = seq // 4, seq // 16
    head0 = _lane_iota((tq, LANES)) < DIL_HD
    head0_stage = _lane_iota((stage_rows, LANES)) < DIL_HD

    dcol = (lax.broadcasted_iota(jnp.int32, (tq, win), 1) - lax.broadcasted_iota(jnp.int32, (tq, win), 0))
    for idx in range(3):
        bias_sc[idx] = jnp.where(jnp.abs(dcol - idx * DIL_HALF) <= DIL_HALF, 0.0, NEG_INF)

    def stage(i, carry):
        rows = pl.ds(pl.multiple_of(i * stage_rows, stage_rows), stage_rows)
        nat_sc[0, rows, :] = q_ref[rows, :].astype(F32)
        nat_sc[1, rows, :] = k_ref[rows, :].astype(F32)
        nat_sc[2, rows, :] = v_ref[rows, :].astype(F32)
        return carry

    lax.fori_loop(0, n_stage, stage, 0)

    for t in range(3):
        for r in range(4):
            c4_sc[t, r * len4:(r + 1) * len4, :] = nat_sc[t, pl.ds(r, len4, stride=4), :]
    for r4 in range(4):
        for j in range(4):
            dst = slice((r4 + 4 * j) * len16, (r4 + 4 * j + 1) * len16)
            src = pl.ds(r4 * len4 + j, len16, stride=4)
            q16_sc[dst, :] = c4_sc[0, src, :]
            k16_sc[dst, :] = c4_sc[1, src, :].astype(BF16)
            w16_sc[0, dst, :] = c4_sc[2, src, :].astype(BF16)

    def value_weights(i, carry):
        rows = pl.ds(pl.multiple_of(i * stage_rows, stage_rows), stage_rows)
        k4_sc[rows, :] = c4_sc[1, rows, :].astype(BF16)
        for src, dst in ((v_ref, w1_sc), (c4_sc.at[2], w4_sc), (w16_sc.at[0], w16_sc)):
            v = src[rows, :].astype(F32)
            dst[1, rows, :] = jnp.where(head0_stage, 1.0, v).astype(BF16)
            dst[0, rows, :] = jnp.where(head0_stage, v, 1.0).astype(BF16)
        return carry

    lax.fori_loop(0, n_stage, value_weights, 0)

    def make_pattern(g, q_src, k_src, w_src, s_sc, e_sc, seq_len, w, out_rows):
        per_seq = seq_len // tq

        def where(t):
            r, i = t // per_seq, t % per_seq
            ks = jnp.clip(i * tq - tq // 2, 0, seq_len - w)
            which = jnp.where(i == 0, 0, jnp.where(i == per_seq - 1, 2, 1)) if per_seq > 1 else 0
            qrows = pl.ds(pl.multiple_of(r * seq_len + i * tq, tq), tq)
            krows = pl.ds(pl.multiple_of(r * seq_len + ks, tq // 2), w)
            return qrows, krows, which

        def scores(t):
            qrows, krows, which = where(t)
            q = q_src[qrows, :]
            q2 = jnp.concatenate([jnp.where(head0, q, 0.0), jnp.where(head0, 0.0, q)], axis=0)
            bias = bias_sc[which, :, :w]
            s = _dot_nt(q2.astype(BF16), k_src[krows, :])
            s_sc[t, :, :w] = s + jnp.concatenate([bias, bias], axis=0)

        def softmax(t):
            s = s_sc[t, :, :w]
            m = jnp.max(s, axis=-1, keepdims=True)
            e_sc[t, :, :w] = jnp.exp2(s - m).astype(BF16)
            m_sc[g, out_rows(t), :] = jnp.where(head0, m[:tq], m[tq:])
            msw_sc[g, out_rows(t), :] = jnp.where(head0, m[tq:], m[:tq])

        def values(t):
            _, krows, _ = where(t)
            out0 = _dot(e_sc[t, :tq, :w], w_src[0, krows, :])
            out1 = _dot(e_sc[t, tq:, :w], w_src[1, krows, :])
            rows = out_rows(t)
            o_sc[g, rows, :] = jnp.where(head0, out0, out1)
            d_sc[g, rows, :] = jnp.where(head0, out1, out0)

        return scores, softmax, values

    sc0, sm0, va0 = make_pattern(0, nat_sc.at[0], k_ref, w1_sc, s0_sc, e0_sc, seq, win,
                                 lambda t: pl.ds(pl.multiple_of(t * tq, tq), tq))
    sc1, sm1, va1 = make_pattern(1, c4_sc.at[0], k4_sc, w4_sc, s1_sc, e1_sc, len4, win,
                                 lambda t: pl.ds(pl.multiple_of(t * tq, tq), tq))
    sc2, sm2, va2 = make_pattern(2, q16_sc, k16_sc, w16_sc, s2_sc, e2_sc, len16, len16,
                                 lambda t: pl.ds((t % 4) * len4 + t // 4, len16, stride=4))

    for passes in ((sc0,), (sm0, sc1), (va0, sm1, sc2), (va1, sm2), (va2,)):
        def body(t, carry, passes=passes):
            for f in passes:
                f(t)
            return carry

        lax.fori_loop(0, seq // tq, body, 0, unroll=unroll)

    y_sc = nat_sc.at[0]
    per_class = len4 // stage_rows

    def combine(i, carry):
        r, m0 = i // per_class, (i % per_class) * stage_rows
        rows = pl.ds(pl.multiple_of(r * len4 + m0, stage_rows), stage_rows)
        nat_rows = pl.ds(r + 4 * m0, stage_rows, stride=4)
        m0, m1, m2 = m_sc[0, nat_rows, :], m_sc[1, rows, :], m_sc[2, rows, :]
        m = jnp.maximum(jnp.maximum(m0, m1), m2)
        w0, w1, w2 = jnp.exp2(m0 - m), jnp.exp2(m1 - m), jnp.exp2(m2 - m)
        o = w0 * o_sc[0, nat_rows, :] + w1 * o_sc[1, rows, :] + w2 * o_sc[2, rows, :]
        n0, n1, n2 = msw_sc[0, nat_rows, :], msw_sc[1, rows, :], msw_sc[2, rows, :]
        n = jnp.maximum(jnp.maximum(n0, n1), n2)
        den = (jnp.exp2(n0 - n) * d_sc[0, nat_rows, :] + jnp.exp2(n1 - n) * d_sc[1, rows, :]
               + jnp.exp2(n2 - n) * d_sc[2, rows, :])
        den = pltpu.roll(den, DIL_HD, 1)
        y_sc[nat_rows, :] = o * (1.0 / den)
        return carry

    lax.fori_loop(0, n_stage, combine, 0)

    def gate(i, carry):
        rows = pl.ds(pl.multiple_of(i * stage_rows, stage_rows), stage_rows)
        out_ref[rows, :] = (y_sc[rows, :] * _silu(g_ref[rows, :].astype(F32))).astype(BF16)
        return carry

    lax.fori_loop(0, n_stage, gate, 0)


def _dilated(proj, batch, seq, unroll=16):
    assert seq // 16 == DIL_TQ and all(w // (2 * d) == DIL_HALF for w, d in DIL_PATTERNS)
    blk = lambda off: pl.BlockSpec((seq, LANES), lambda b, p: (b, off // LANES + p))
    n_blocks = seq // DIL_TQ
    f32_rows = pltpu.VMEM((seq, LANES), F32)
    bf16_rows = pltpu.VMEM((seq, LANES), BF16)
    bf16_pair = pltpu.VMEM((2, seq, LANES), BF16)
    s_buf = pltpu.VMEM((n_blocks, 2 * DIL_TQ, 2 * DIL_TQ), F32)
    e_buf = pltpu.VMEM((n_blocks, 2 * DIL_TQ, 2 * DIL_TQ), BF16)
    return pl.pallas_call(
        functools.partial(_dilated_kernel, seq=seq, unroll=unroll),
        grid=(batch, DIL_HEADS // 2),
        in_specs=[blk(OFF_CQ), blk(OFF_CK), blk(OFF_CV), blk(OFF_CG)],
        out_specs=pl.BlockSpec((seq, LANES), lambda b, p: (b, p)),
        out_shape=jax.ShapeDtypeStruct((batch * seq, BRANCH_W), BF16),
        scratch_shapes=[
            pltpu.VMEM((3, seq, LANES), F32),
            pltpu.VMEM((3, seq, LANES), F32), bf16_rows, f32_rows, bf16_rows,
            bf16_pair, bf16_pair, bf16_pair,
            s_buf, s_buf, s_buf, e_buf, e_buf, e_buf,
            pltpu.VMEM((3, seq, LANES), F32), pltpu.VMEM((3, seq, LANES), F32),
            pltpu.VMEM((3, seq, LANES), F32), pltpu.VMEM((3, seq, LANES), F32),
            pltpu.VMEM((3, DIL_TQ, 2 * DIL_TQ), F32),
        ],
        compiler_params=pltpu.CompilerParams(
            dimension_semantics=("arbitrary", "arbitrary"), vmem_limit_bytes=VMEM_LIMIT),
        name="dilated",
    )(proj, proj, proj, proj)


def _merge_kernel(x_ref, ya_ref, yb_ref, yc_ref, m_ref, wb_ref, wo_ref, fg_ref, out_ref, wb_sc, wo_sc,
                  *, final):
    @pl.when(pl.program_id(0) == 0)
    def _():
        wb_sc[...] = wb_ref[...].astype(BF16)
        wo_sc[...] = wo_ref[...].astype(BF16)

    mixed = None
    for gi, y_ref in enumerate((ya_ref, yb_ref, yc_ref)):
        z = _dot(y_ref[...], wb_sc[gi])
        gate = 0.5 * jnp.tanh(0.5 * m_ref[:, gi * D_MODEL:(gi + 1) * D_MODEL].astype(F32)) + 0.5
        mixed = gate * z if mixed is None else mixed + gate * z
    x = x_ref[...] + _dot(mixed.astype(BF16), wo_sc[...])
    if final:
        ms = jnp.mean(x * x, axis=-1, keepdims=True)
        x = x * lax.rsqrt(ms + NORM_EPS) * fg_ref[...]
    out_ref[...] = x


def _merge(x2, ya, yb, yc, proj, wb, wo, final_g, layer, final, tm=1024):
    rows = x2.shape[0]
    yspec = pl.BlockSpec((tm, BRANCH_W), lambda i: (i, 0))
    return pl.pallas_call(
        functools.partial(_merge_kernel, final=final),
        grid=(rows // tm,),
        in_specs=[
            pl.BlockSpec((tm, D_MODEL), lambda i: (i, 0)),
            yspec, yspec, yspec,
            pl.BlockSpec((tm, MERGE_W), lambda i: (i, 0)),
            pl.BlockSpec((None, N_BRANCH, BRANCH_W, D_MODEL), lambda i: (layer, 0, 0, 0),
                         pipeline_mode=pl.Buffered(1)),
            pl.BlockSpec((None, D_MODEL, D_MODEL), lambda i: (layer, 0, 0), pipeline_mode=pl.Buffered(1)),
            pl.BlockSpec((1, D_MODEL), lambda i: (0, 0)),
        ],
        out_specs=pl.BlockSpec((tm, D_MODEL), lambda i: (i, 0)),
        out_shape=jax.ShapeDtypeStruct((rows, D_MODEL), F32),
        scratch_shapes=[pltpu.VMEM((N_BRANCH, BRANCH_W, D_MODEL), BF16), pltpu.VMEM((D_MODEL, D_MODEL), BF16)],
        compiler_params=pltpu.CompilerParams(
            dimension_semantics=("arbitrary",), vmem_limit_bytes=VMEM_LIMIT),
        name="merge",
    )(x2, ya, yb, yc, proj, wb, wo, final_g)


def _rope_tables(seq, dim):
    inv = 1.0 / (ROPE_THETA ** (jnp.arange(0, dim, 2, dtype=F32) / dim))
    ang = jnp.arange(seq, dtype=F32)[:, None] * inv[None, :]
    cos, sin = jnp.cos(ang), jnp.sin(ang)
    reps = LANES // dim
    cos_t = jnp.tile(jnp.concatenate([cos, cos], axis=-1), (1, reps))
    sin_t = jnp.tile(jnp.concatenate([-sin, sin], axis=-1), (1, reps))
    return cos_t, sin_t


def kernel(x, norm_g, w_in, gla_gate_up, gla_gate_b, gla_norm_g, diff_lambda, diff_norm_g, w_branch, w_out, final_norm_g):
    batch, seq, _ = x.shape
    depth = w_in.shape[0]
    cos, sin = _rope_tables(seq, DIFF_D)

    lo, hi = LOWRANK_OFF, LOWRANK_OFF + LOWRANK_W
    merge_off = IN_COLS - MERGE_W
    w_parts = (w_in[:, :, merge_off:].astype(BF16), w_in[:, :, :lo].astype(BF16),
               w_in[:, :, hi:merge_off].astype(BF16))
    w_lr = w_in[:, :, lo:hi].astype(BF16)
    w_low = jnp.concatenate([w_lr, w_lr, w_lr, jnp.zeros_like(w_lr)], axis=-1)
    gu = gla_gate_up.reshape(depth, 2, GLA_RANK, GLA_HEADS, GLA_DK)
    zeros = jnp.zeros_like(gu[:, 0])
    gu_f = jnp.concatenate([gu[:, 0], zeros], axis=-1).reshape(depth, GLA_RANK, GLA_HEADS * LANES)
    gu_b = jnp.concatenate([zeros, gu[:, 1]], axis=-1).reshape(depth, GLA_RANK, GLA_HEADS * LANES)
    gu_cat = jnp.concatenate([gu_f, gu_b], axis=1)
    gu_hi = gu_cat.astype(BF16)
    gu_lo = (gu_cat - gu_hi.astype(F32)).astype(BF16)
    gate_up = jnp.concatenate([gu_hi, gu_lo, gu_hi, jnp.zeros_like(gu_hi)], axis=1)
    gb = gla_gate_b.reshape(depth, 2, GLA_HEADS, GLA_DK)
    gate_b = jnp.concatenate([gb[:, 0], gb[:, 1]], axis=-1).reshape(depth, 1, GLA_HEADS * LANES)
    lam_init = jnp.asarray([0.8 - 0.6 * math.exp(-0.3 * layer) for layer in range(depth)],
                           F32).reshape(depth, 1, 1)
    norm_g3 = norm_g[:, None, :]
    gla_norm_g3 = gla_norm_g[:, None, :]
    diff_norm_g3 = diff_norm_g[:, None, :]

    x2 = x.reshape(batch * seq, D_MODEL)
    for layer in range(depth):
        proj, la = _in_proj(x2, norm_g3, w_parts, w_low, gate_up, gate_b, cos, sin, layer, seq)
        ya = _gla(proj, la, gla_norm_g3, layer, batch, seq)
        yb = _diff(proj, diff_lambda, lam_init, diff_norm_g3, layer, batch, seq)
        yc = _dilated(proj, batch, seq)
        x2 = _merge(x2, ya, yb, yc, proj, w_branch, w_out, final_norm_g[None], layer, final=layer == depth - 1)
    return x2.reshape(batch, seq, D_MODEL)
```

```python
import functools
import math

import jax
import jax.numpy as jnp
from jax import lax
from jax.experimental import pallas as pl
from jax.experimental.pallas import tpu as pltpu

F32 = jnp.float32
BF16 = jnp.bfloat16

LANES = 128

D_MODEL = 1024
ROPE_THETA = 10000.0
NORM_EPS = 1e-6
BRANCH_W = D_MODEL // 2
N_BRANCH = 3

GLA_HEADS = 4
GLA_DV = BRANCH_W // GLA_HEADS
GLA_DK = GLA_DV // 2
GLA_RANK = 16
GLA_TAU = 16.0
GLA_CHUNK = 64
GLA_QK_W = GLA_HEADS * GLA_DK

DIFF_HEADS = 4
DIFF_D = BRANCH_W // (2 * DIFF_HEADS)
DIFF_QK_W = DIFF_HEADS * 2 * DIFF_D

DIL_HEADS = 8
DIL_HD = BRANCH_W // DIL_HEADS
DIL_W = DIL_HEADS * DIL_HD
DIL_PATTERNS = ((128, 1), (512, 4), (2048, 16))
DIL_HALF = 64
DIL_TQ = 128
NEG_INF = -1e30

MERGE_W = N_BRANCH * D_MODEL
LOWRANK_OFF = GLA_QK_W * 2 + BRANCH_W * 2
LOWRANK_W = 2 * GLA_RANK
IN_COLS = LOWRANK_OFF + LOWRANK_W + 2 * (DIFF_QK_W * 2 + BRANCH_W * 2) + MERGE_W
MAIN_W = IN_COLS - LOWRANK_W

OFF_AQ = MERGE_W
OFF_AK = OFF_AQ + GLA_QK_W
OFF_AV = OFF_AK + GLA_QK_W
OFF_AG = OFF_AV + BRANCH_W
OFF_BQ = OFF_AG + BRANCH_W
OFF_BK = OFF_BQ + DIFF_QK_W
OFF_BV = OFF_BK + DIFF_QK_W
OFF_BG = OFF_BV + BRANCH_W
OFF_CQ = OFF_BG + BRANCH_W
OFF_CK = OFF_CQ + DIL_W
OFF_CV = OFF_CK + DIL_W
OFF_CG = OFF_CV + BRANCH_W

VMEM_LIMIT = 56 * 1024 * 1024


def _dot(a, b):
    return jnp.dot(a, b, preferred_element_type=F32)


def _dot_nt(a, b):
    return lax.dot_general(a, b, (((1,), (1,)), ((), ())), preferred_element_type=F32)


def _silu(x):
    return 0.5 * x * (1.0 + jnp.tanh(0.5 * x))


def _lane_iota(shape):
    return lax.broadcasted_iota(jnp.int32, shape, len(shape) - 1)


def _split_bf16(x):
    hi = x.astype(BF16).astype(F32)
    return hi, x - hi


def _rope(x, cos, sin_signed):
    lane = _lane_iota(x.shape)
    first_half = (lane % 64) < 32
    partner = jnp.where(first_half, pltpu.roll(x, 96, 1), pltpu.roll(x, 32, 1))
    return x * cos + partner * sin_signed


_PLAIN, _ROPE_Q, _ROPE_K = 0, 1, 2
_W_MERGE, _W_A, _W_BC = 0, 1, 2
_PROJ_SEGMENTS = (
    (_W_MERGE, 0, MERGE_W // 2, _PLAIN, 0), (_W_MERGE, MERGE_W // 2, MERGE_W // 2, _PLAIN, MERGE_W // 2),
    (_W_A, 0, LOWRANK_OFF, _PLAIN, OFF_AQ),
    (_W_BC, 0, DIFF_QK_W, _ROPE_Q, OFF_BQ), (_W_BC, OFF_BK - OFF_BQ, DIFF_QK_W, _ROPE_K, OFF_BK),
    (_W_BC, OFF_BV - OFF_BQ, 2 * BRANCH_W, _PLAIN, OFF_BV),
    (_W_BC, OFF_CQ - OFF_BQ, DIL_W, _ROPE_Q, OFF_CQ), (_W_BC, OFF_CK - OFF_BQ, DIL_W, _ROPE_K, OFF_CK),
    (_W_BC, OFF_CV - OFF_BQ, 2 * BRANCH_W, _PLAIN, OFF_CV),
)
LOG2E = 1.4426950408889634
QK_SCALE = DIFF_D ** -0.5 * LOG2E


def _in_proj_kernel(x_ref, g_ref, wm_ref, wa_ref, wbc_ref, wl_ref, gu_ref, gb_ref, cos_ref, sin_ref,
                    out_ref, la_ref):
    x = x_ref[...]
    ms = jnp.mean(x * x, axis=-1, keepdims=True)
    hb = (x * lax.rsqrt(ms + NORM_EPS) * g_ref[...]).astype(BF16)
    low = _dot(hb, wl_ref[...])
    low_hi, low_lo = _split_bf16(low)
    lhs = jnp.where(_lane_iota(low.shape) < 2 * LOWRANK_W, low_hi, low_lo).astype(BF16)
    z = _dot(lhs, gu_ref[...]) + gb_ref[...]
    log_sig = jnp.minimum(z, 0.0) - jnp.log1p(jnp.exp(-jnp.abs(z)))
    la_ref[...] = log_sig * (1.0 / GLA_TAU)
    w_refs = (wm_ref, wa_ref, wbc_ref)
    for which, src, width, kind, start in _PROJ_SEGMENTS:
        res = _dot(hb, w_refs[which][:, src:src + width])
        if kind != _PLAIN:
            cos, sin = cos_ref[...], sin_ref[...]
            tiles = [_rope(res[:, c:c + LANES], cos, sin) for c in range(0, width, LANES)]
            res = jnp.concatenate(tiles, axis=1)
            if kind == _ROPE_Q:
                res = res * QK_SCALE
        out_ref[:, start:start + width] = res.astype(BF16)


def _in_proj(x2, g, w_parts, w_low, gate_up, gate_b, cos, sin, layer, seq, tm=512):
    assert DIFF_D == DIL_HD and seq % tm == 0
    rows = x2.shape[0]
    resident = dict(pipeline_mode=pl.Buffered(1))
    pos_blocks = seq // tm
    return pl.pallas_call(
        _in_proj_kernel,
        grid=(rows // tm,),
        in_specs=[
            pl.BlockSpec((tm, D_MODEL), lambda i: (i, 0)),
            pl.BlockSpec((None, 1, D_MODEL), lambda i: (layer, 0, 0)),
            *[pl.BlockSpec((None,) + w.shape[1:], lambda i: (layer, 0, 0), **resident) for w in w_parts],
            pl.BlockSpec((None, D_MODEL, LANES), lambda i: (layer, 0, 0), **resident),
            pl.BlockSpec((None, LANES, 4 * LANES), lambda i: (layer, 0, 0), **resident),
            pl.BlockSpec((None, 1, 4 * LANES), lambda i: (layer, 0, 0)),
            pl.BlockSpec((tm, LANES), lambda i: (i % pos_blocks, 0)),
            pl.BlockSpec((tm, LANES), lambda i: (i % pos_blocks, 0)),
        ],
        out_specs=[
            pl.BlockSpec((tm, MAIN_W), lambda i: (i, 0)),
            pl.BlockSpec((tm, 4 * LANES), lambda i: (i, 0)),
        ],
        out_shape=[
            jax.ShapeDtypeStruct((rows, MAIN_W), BF16),
            jax.ShapeDtypeStruct((rows, 4 * LANES), F32),
        ],
        compiler_params=pltpu.CompilerParams(
            dimension_semantics=("arbitrary",), vmem_limit_bytes=VMEM_LIMIT),
        name="in_proj",
    )(x2, g, *w_parts, w_low, gate_up, gate_b, cos, sin)


def _gla_kernel(q_ref, k_ref, v_ref, g_ref, la_ref, gn_ref, out_ref,
                u_sc, d_sc, st_sc, qh_sc, kh_sc, att_sc, *, seq, unroll):
    C = GLA_CHUNK
    n_chunks = seq // C
    lane_c = _lane_iota((C, LANES))
    row_c = lax.broadcasted_iota(jnp.int32, (C, LANES), 0)
    fwd_c = lane_c < GLA_DK
    tri = jnp.where(fwd_c, row_c - lane_c, lane_c - GLA_DK - row_c) >= 0
    row_s = lax.broadcasted_iota(jnp.int32, (2 * C, LANES), 0)
    lane_s = _lane_iota((2 * C, LANES)) % C
    cum_op = (jnp.where(row_s < C, row_s - lane_s, lane_s - (row_s - C)) >= 0).astype(F32).astype(BF16)
    fwd_sq = _lane_iota((LANES, LANES)) < GLA_DK
    scale = GLA_DK ** -0.5

    def head_cols(hh):
        return slice(hh * LANES, (hh + 1) * LANES)

    def dup_heads(ref, rows):
        xx = ref[rows, :].astype(F32)
        rolled = pltpu.roll(xx, GLA_DK, 1)
        return jnp.where(fwd_c, xx, rolled), jnp.where(fwd_c, rolled, xx)

    def prep(n, carry):
        rows = pl.ds(pl.multiple_of(n * C, C), C)
        q2 = dup_heads(q_ref, rows)
        k2 = dup_heads(k_ref, rows)
        for hh in range(2):
            cols = head_cols(hh)
            la = la_ref[rows, cols]
            la_hi, la_lo = _split_bf16(la)
            sums = _dot(cum_op, jnp.concatenate([la_hi, la_lo], axis=0).astype(BF16))
            b = jnp.where(fwd_c, sums[:C], sums[C:])
            tot = jnp.sum(la, axis=0, keepdims=True)
            decay = jnp.exp(tot)
            qh_sc[hh, rows, :] = (q2[hh] * jnp.exp(b) * scale).astype(BF16)
            k_hat = k2[hh] * jnp.exp(-b)
            kh_sc[hh, n] = jnp.concatenate(
                [jnp.where(fwd_c, k_hat, 0.0), jnp.where(fwd_c, 0.0, k_hat)], axis=0).astype(BF16)
            k_end = (k_hat * decay).astype(BF16)
            u_sc[hh, n] = lax.dot_general(v_ref[rows, cols], k_end, (((0,), (0,)), ((), ())),
                                          preferred_element_type=F32)
            d_sc[hh, n] = jnp.broadcast_to(decay, (8, LANES))
        return carry

    lax.fori_loop(0, n_chunks, prep, 0, unroll=unroll)

    def scan(i, states):
        j = n_chunks - 1 - i
        new = []
        for hh in range(2):
            st_sc[hh, i] = states[hh]
            upd = jnp.where(fwd_sq, u_sc[hh, i], u_sc[hh, j])
            dec = jnp.where(fwd_sq[:8], d_sc[hh, i], d_sc[hh, j])[0:1]
            new.append(states[hh] * dec + upd)
        return tuple(new)

    zero = jnp.zeros((LANES, LANES), F32)
    lax.fori_loop(0, n_chunks, scan, (zero, zero), unroll=4)

    def attend(n, carry):
        rows = pl.ds(pl.multiple_of(n * C, C), C)
        for hh in range(2):
            att = _dot_nt(qh_sc[hh, rows, :], kh_sc[hh, n])
            att_sc[hh, rows, :] = jnp.where(tri, att, 0.0).astype(BF16)
        return carry

    lax.fori_loop(0, n_chunks, attend, 0, unroll=unroll)

    def emit(n, carry):
        rows = pl.ds(pl.multiple_of(n * C, C), C)
        for hh in range(2):
            cols = head_cols(hh)
            v = v_ref[rows, cols]
            o = _dot(att_sc[hh, rows, :], jnp.concatenate([v, v], axis=0))
            state = jnp.where(fwd_sq, st_sc[hh, n], st_sc[hh, n_chunks - 1 - n]).astype(BF16)
            o = o + _dot_nt(qh_sc[hh, rows, :], state)
            ms = jnp.mean(o * o, axis=-1, keepdims=True)
            y = o * lax.rsqrt(ms + NORM_EPS) * gn_ref[...]
            out_ref[rows, cols] = (y * _silu(g_ref[rows, cols].astype(F32))).astype(BF16)
        return carry

    lax.fori_loop(0, n_chunks, emit, 0, unroll=unroll)


def _gla(proj, la, gn, layer, batch, seq, unroll=32):
    n_chunks = seq // GLA_CHUNK
    pair = 2 * LANES
    return pl.pallas_call(
        functools.partial(_gla_kernel, seq=seq, unroll=unroll),
        grid=(batch, GLA_HEADS // 2),
        in_specs=[
            pl.BlockSpec((seq, LANES), lambda b, p: (b, OFF_AQ // LANES + p)),
            pl.BlockSpec((seq, LANES), lambda b, p: (b, OFF_AK // LANES + p)),
            pl.BlockSpec((seq, pair), lambda b, p: (b, OFF_AV // pair + p)),
            pl.BlockSpec((seq, pair), lambda b, p: (b, OFF_AG // pair + p)),
            pl.BlockSpec((seq, pair), lambda b, p: (b, p)),
            pl.BlockSpec((None, 1, LANES), lambda b, p: (layer, 0, 0)),
        ],
        out_specs=pl.BlockSpec((seq, pair), lambda b, p: (b, p)),
        out_shape=jax.ShapeDtypeStruct((batch * seq, BRANCH_W), BF16),
        scratch_shapes=[
            pltpu.VMEM((2, n_chunks, LANES, LANES), F32),
            pltpu.VMEM((2, n_chunks, 8, LANES), F32),
            pltpu.VMEM((2, n_chunks, LANES, LANES), F32),
            pltpu.VMEM((2, seq, LANES), BF16),
            pltpu.VMEM((2, n_chunks, LANES, LANES), BF16),
            pltpu.VMEM((2, seq, LANES), BF16),
        ],
        compiler_params=pltpu.CompilerParams(
            dimension_semantics=("arbitrary", "arbitrary"), vmem_limit_bytes=VMEM_LIMIT),
        name="gla",
    )(proj, proj, proj, proj, la, gn)


def _diff_kernel(q_ref, k_ref, v_ref, g_ref, dl_ref, li_ref, gn_ref, out_ref,
                 q_sc, s0_sc, s1_sc, pm0_sc, pm1_sc, *, seq, tq):
    n_blk = seq // tq
    half0 = _lane_iota((tq, LANES)) < DIFF_D

    def stack_q(i, carry):
        q = q_ref[pl.ds(pl.multiple_of(i * tq, tq), tq), :].astype(F32)
        q_sc[i] = jnp.concatenate([jnp.where(half0, q, 0.0), jnp.where(half0, 0.0, q)], axis=0).astype(BF16)
        return carry

    lax.fori_loop(0, n_blk, stack_q, 0)

    dl = dl_ref[...]
    lam_init = li_ref[...]
    lam = (jnp.exp(jnp.sum(dl[0:1] * dl[1:2], axis=-1, keepdims=True))
           - jnp.exp(jnp.sum(dl[2:3] * dl[3:4], axis=-1, keepdims=True)) + lam_init)

    def scores(i, s_sc, pm_sc):
        s = _dot_nt(q_sc[i], k_ref[...])
        s_sc[...] = s
        pm_sc[...] = functools.reduce(jnp.maximum, [s[:, c:c + LANES] for c in range(0, seq, LANES)])

    def finish(i, s_sc, pm_sc):
        rows = pl.ds(i * tq, tq)
        m = jnp.max(pm_sc[...], axis=-1, keepdims=True)
        e = jnp.exp2(s_sc[...] - m)
        l = jnp.sum(e, axis=-1, keepdims=True)
        o2 = _dot(e.astype(BF16), v_ref[...]) * (1.0 / l)
        o = o2[:tq] - lam * o2[tq:]
        ms = jnp.mean(o * o, axis=-1, keepdims=True)
        y = o * lax.rsqrt(ms + NORM_EPS) * gn_ref[...] * (1.0 - lam_init)
        out_ref[rows, :] = (y * _silu(g_ref[rows, :].astype(F32))).astype(BF16)

    bufs = ((s0_sc, pm0_sc), (s1_sc, pm1_sc))
    scores(0, *bufs[0])
    for i in range(n_blk):
        if i + 1 < n_blk:
            scores(i + 1, *bufs[(i + 1) % 2])
        finish(i, *bufs[i % 2])


def _diff(proj, dl, lam_init, gn, layer, batch, seq, tq=256):
    blk = lambda off: pl.BlockSpec((seq, LANES), lambda b, h: (b, off // LANES + h))
    per_layer = lambda shape: pl.BlockSpec((None,) + shape, lambda b, h: (layer, 0, 0))
    return pl.pallas_call(
        functools.partial(_diff_kernel, seq=seq, tq=tq),
        grid=(batch, DIFF_HEADS),
        in_specs=[blk(OFF_BQ), blk(OFF_BK), blk(OFF_BV), blk(OFF_BG),
                  per_layer((4, DIFF_D)), per_layer((1, 1)), per_layer((1, LANES))],
        out_specs=pl.BlockSpec((seq, LANES), lambda b, h: (b, h)),
        out_shape=jax.ShapeDtypeStruct((batch * seq, BRANCH_W), BF16),
        scratch_shapes=[pltpu.VMEM((seq // tq, 2 * tq, LANES), BF16),
                        pltpu.VMEM((2 * tq, seq), F32), pltpu.VMEM((2 * tq, seq), F32),
                        pltpu.VMEM((2 * tq, LANES), F32), pltpu.VMEM((2 * tq, LANES), F32)],
        compiler_params=pltpu.CompilerParams(
            dimension_semantics=("arbitrary", "arbitrary"), vmem_limit_bytes=VMEM_LIMIT),
        name="diff",
    )(proj, proj, proj, proj, dl, lam_init, gn)


def _dilated_kernel(q_ref, k_ref, v_ref, g_ref, out_ref,
                    nat_sc, c4_sc, k4_sc, q16_sc, k16_sc, w1_sc, w4_sc, w16_sc,
                    s0_sc, s1_sc, s2_sc, e0_sc, e1_sc, e2_sc,
                    o_sc, m_sc, msw_sc, d_sc, bias_sc, *, seq, unroll):
    tq, win = DIL_TQ, 2 * DIL_TQ
    stage_rows = 2 * tq
    n_stage = seq // stage_rows
    len4, len16 = seq // 4, seq // 16
    head0 = _lane_iota((tq, LANES)) < DIL_HD
    head0_stage = _lane_iota((stage_rows, LANES)) < DIL_HD

    dcol = (lax.broadcasted_iota(jnp.int32, (tq, win), 1) - lax.broadcasted_iota(jnp.int32, (tq, win), 0))
    for idx in range(3):
        bias_sc[idx] = jnp.where(jnp.abs(dcol - idx * DIL_HALF) <= DIL_HALF, 0.0, NEG_INF)

    def stage(i, carry):
        rows = pl.ds(pl.multiple_of(i * stage_rows, stage_rows), stage_rows)
        nat_sc[0, rows, :] = q_ref[rows, :].astype(F32)
        nat_sc[1, rows, :] = k_ref[rows, :].astype(F32)
        nat_sc[2, rows, :] = v_ref[rows, :].astype(F32)
        return carry

    lax.fori_loop(0, n_stage, stage, 0)

    for t in range(3):
        for r in range(4):
            c4_sc[t, r * len4:(r + 1) * len4, :] = nat_sc[t, pl.ds(r, len4, stride=4), :]
    for r4 in range(4):
        for j in range(4):
            dst = slice((r4 + 4 * j) * len16, (r4 + 4 * j + 1) * len16)
            src = pl.ds(r4 * len4 + j, len16, stride=4)
            q16_sc[dst, :] = c4_sc[0, src, :]
            k16_sc[dst, :] = c4_sc[1, src, :].astype(BF16)
            w16_sc[0, dst, :] = c4_sc[2, src, :].astype(BF16)

    def value_weights(i, carry):
        rows = pl.ds(pl.multiple_of(i * stage_rows, stage_rows), stage_rows)
        k4_sc[rows, :] = c4_sc[1, rows, :].astype(BF16)
        for src, dst in ((v_ref, w1_sc), (c4_sc.at[2], w4_sc), (w16_sc.at[0], w16_sc)):
            v = src[rows, :].astype(F32)
            dst[1, rows, :] = jnp.where(head0_stage, 1.0, v).astype(BF16)
            dst[0, rows, :] = jnp.where(head0_stage, v, 1.0).astype(BF16)
        return carry

    lax.fori_loop(0, n_stage, value_weights, 0)

    def make_pattern(g, q_src, k_src, w_src, s_sc, e_sc, seq_len, w, out_rows):
        per_seq = seq_len // tq

        def where(t):
            r, i = t // per_seq, t % per_seq
            ks = jnp.clip(i * tq - tq // 2, 0, seq_len - w)
            which = jnp.where(i == 0, 0, jnp.where(i == per_seq - 1, 2, 1)) if per_seq > 1 else 0
            qrows = pl.ds(pl.multiple_of(r * seq_len + i * tq, tq), tq)
            krows = pl.ds(pl.multiple_of(r * seq_len + ks, tq // 2), w)
            return qrows, krows, which

        def scores(t):
            qrows, krows, which = where(t)
            q = q_src[qrows, :]
            q2 = jnp.concatenate([jnp.where(head0, q, 0.0), jnp.where(head0, 0.0, q)], axis=0)
            bias = bias_sc[which, :, :w]
            s = _dot_nt(q2.astype(BF16), k_src[krows, :])
            s_sc[t, :, :w] = s + jnp.concatenate([bias, bias], axis=0)

        def softmax(t):
            s = s_sc[t, :, :w]
            m = jnp.max(s, axis=-1, keepdims=True)
            e_sc[t, :, :w] = jnp.exp2(s - m).astype(BF16)
            m_sc[g, out_rows(t), :] = jnp.where(head0, m[:tq], m[tq:])
            msw_sc[g, out_rows(t), :] = jnp.where(head0, m[tq:], m[:tq])

        def values(t):
            _, krows, _ = where(t)
            out0 = _dot(e_sc[t, :tq, :w], w_src[0, krows, :])
            out1 = _dot(e_sc[t, tq:, :w], w_src[1, krows, :])
            rows = out_rows(t)
            o_sc[g, rows, :] = jnp.where(head0, out0, out1)
            d_sc[g, rows, :] = jnp.where(head0, out1, out0)

        return scores, softmax, values

    sc0, sm0, va0 = make_pattern(0, nat_sc.at[0], k_ref, w1_sc, s0_sc, e0_sc, seq, win,
                                 lambda t: pl.ds(pl.multiple_of(t * tq, tq), tq))
    sc1, sm1, va1 = make_pattern(1, c4_sc.at[0], k4_sc, w4_sc, s1_sc, e1_sc, len4, win,
                                 lambda t: pl.ds(pl.multiple_of(t * tq, tq), tq))
    sc2, sm2, va2 = make_pattern(2, q16_sc, k16_sc, w16_sc, s2_sc, e2_sc, len16, len16,
                                 lambda t: pl.ds((t % 4) * len4 + t // 4, len16, stride=4))

    for passes in ((sc0,), (sm0, sc1), (va0, sm1, sc2), (va1, sm2), (va2,)):
        def body(t, carry, passes=passes):
            for f in passes:
                f(t)
            return carry

        lax.fori_loop(0, seq // tq, body, 0, unroll=unroll)

    y_sc = nat_sc.at[0]
    per_class = len4 // stage_rows

    def combine(i, carry):
        r, m0 = i // per_class, (i % per_class) * stage_rows
        rows = pl.ds(pl.multiple_of(r * len4 + m0, stage_rows), stage_rows)
        nat_rows = pl.ds(r + 4 * m0, stage_rows, stride=4)
        m0, m1, m2 = m_sc[0, nat_rows, :], m_sc[1, rows, :], m_sc[2, rows, :]
        m = jnp.maximum(jnp.maximum(m0, m1), m2)
        w0, w1, w2 = jnp.exp2(m0 - m), jnp.exp2(m1 - m), jnp.exp2(m2 - m)
        o = w0 * o_sc[0, nat_rows, :] + w1 * o_sc[1, rows, :] + w2 * o_sc[2, rows, :]
        n0, n1, n2 = msw_sc[0, nat_rows, :], msw_sc[1, rows, :], msw_sc[2, rows, :]
        n = jnp.maximum(jnp.maximum(n0, n1), n2)
        den = (jnp.exp2(n0 - n) * d_sc[0, nat_rows, :] + jnp.exp2(n1 - n) * d_sc[1, rows, :]
               + jnp.exp2(n2 - n) * d_sc[2, rows, :])
        den = pltpu.roll(den, DIL_HD, 1)
        y_sc[nat_rows, :] = o * (1.0 / den)
        return carry

    lax.fori_loop(0, n_stage, combine, 0)

    def gate(i, carry):
        rows = pl.ds(pl.multiple_of(i * stage_rows, stage_rows), stage_rows)
        out_ref[rows, :] = (y_sc[rows, :] * _silu(g_ref[rows, :].astype(F32))).astype(BF16)
        return carry

    lax.fori_loop(0, n_stage, gate, 0)


def _dilated(proj, batch, seq, unroll=16):
    assert seq // 16 == DIL_TQ and all(w // (2 * d) == DIL_HALF for w, d in DIL_PATTERNS)
    blk = lambda off: pl.BlockSpec((seq, LANES), lambda b, p: (b, off // LANES + p))
    n_blocks = seq // DIL_TQ
    f32_rows = pltpu.VMEM((seq, LANES), F32)
    bf16_rows = pltpu.VMEM((seq, LANES), BF16)
    bf16_pair = pltpu.VMEM((2, seq, LANES), BF16)
    s_buf = pltpu.VMEM((n_blocks, 2 * DIL_TQ, 2 * DIL_TQ), F32)
    e_buf = pltpu.VMEM((n_blocks, 2 * DIL_TQ, 2 * DIL_TQ), BF16)
    return pl.pallas_call(
        functools.partial(_dilated_kernel, seq=seq, unroll=unroll),
        grid=(batch, DIL_HEADS // 2),
        in_specs=[blk(OFF_CQ), blk(OFF_CK), blk(OFF_CV), blk(OFF_CG)],
        out_specs=pl.BlockSpec((seq, LANES), lambda b, p: (b, p)),
        out_shape=jax.ShapeDtypeStruct((batch * seq, BRANCH_W), BF16),
        scratch_shapes=[
            pltpu.VMEM((3, seq, LANES), F32),
            pltpu.VMEM((3, seq, LANES), F32), bf16_rows, f32_rows, bf16_rows,
            bf16_pair, bf16_pair, bf16_pair,
            s_buf, s_buf, s_buf, e_buf, e_buf, e_buf,
            pltpu.VMEM((3, seq, LANES), F32), pltpu.VMEM((3, seq, LANES), F32),
            pltpu.VMEM((3, seq, LANES), F32), pltpu.VMEM((3, seq, LANES), F32),
            pltpu.VMEM((3, DIL_TQ, 2 * DIL_TQ), F32),
        ],
        compiler_params=pltpu.CompilerParams(
            dimension_semantics=("arbitrary", "arbitrary"), vmem_limit_bytes=VMEM_LIMIT),
        name="dilated",
    )(proj, proj, proj, proj)


def _merge_kernel(x_ref, ya_ref, yb_ref, yc_ref, m_ref, wb_ref, wo_ref, fg_ref, out_ref, wb_sc, wo_sc,
                  *, final):
    @pl.when(pl.program_id(0) == 0)
    def _():
        wb_sc[...] = wb_ref[...].astype(BF16)
        wo_sc[...] = wo_ref[...].astype(BF16)

    mixed = None
    for gi, y_ref in enumerate((ya_ref, yb_ref, yc_ref)):
        z = _dot(y_ref[...], wb_sc[gi])
        gate = 0.5 * jnp.tanh(0.5 * m_ref[:, gi * D_MODEL:(gi + 1) * D_MODEL].astype(F32)) + 0.5
        mixed = gate * z if mixed is None else mixed + gate * z
    x = x_ref[...] + _dot(mixed.astype(BF16), wo_sc[...])
    if final:
        ms = jnp.mean(x * x, axis=-1, keepdims=True)
        x = x * lax.rsqrt(ms + NORM_EPS) * fg_ref[...]
    out_ref[...] = x


def _merge(x2, ya, yb, yc, proj, wb, wo, final_g, layer, final, tm=1024):
    rows = x2.shape[0]
    yspec = pl.BlockSpec((tm, BRANCH_W), lambda i: (i, 0))
    return pl.pallas_call(
        functools.partial(_merge_kernel, final=final),
        grid=(rows // tm,),
        in_specs=[
            pl.BlockSpec((tm, D_MODEL), lambda i: (i, 0)),
            yspec, yspec, yspec,
            pl.BlockSpec((tm, MERGE_W), lambda i: (i, 0)),
            pl.BlockSpec((None, N_BRANCH, BRANCH_W, D_MODEL), lambda i: (layer, 0, 0, 0),
                         pipeline_mode=pl.Buffered(1)),
            pl.BlockSpec((None, D_MODEL, D_MODEL), lambda i: (layer, 0, 0), pipeline_mode=pl.Buffered(1)),
            pl.BlockSpec((1, D_MODEL), lambda i: (0, 0)),
        ],
        out_specs=pl.BlockSpec((tm, D_MODEL), lambda i: (i, 0)),
        out_shape=jax.ShapeDtypeStruct((rows, D_MODEL), F32),
        scratch_shapes=[pltpu.VMEM((N_BRANCH, BRANCH_W, D_MODEL), BF16), pltpu.VMEM((D_MODEL, D_MODEL), BF16)],
        compiler_params=pltpu.CompilerParams(
            dimension_semantics=("arbitrary",), vmem_limit_bytes=VMEM_LIMIT),
        name="merge",
    )(x2, ya, yb, yc, proj, wb, wo, final_g)


def _rope_tables(seq, dim):
    inv = 1.0 / (ROPE_THETA ** (jnp.arange(0, dim, 2, dtype=F32) / dim))
    ang = jnp.arange(seq, dtype=F32)[:, None] * inv[None, :]
    cos, sin = jnp.cos(ang), jnp.sin(ang)
    reps = LANES // dim
    cos_t = jnp.tile(jnp.concatenate([cos, cos], axis=-1), (1, reps))
    sin_t = jnp.tile(jnp.concatenate([-sin, sin], axis=-1), (1, reps))
    return cos_t, sin_t


def kernel(x, norm_g, w_in, gla_gate_up, gla_gate_b, gla_norm_g, diff_lambda, diff_norm_g, w_branch, w_out, final_norm_g):
    batch, seq, _ = x.shape
    depth = w_in.shape[0]
    cos, sin = _rope_tables(seq, DIFF_D)

    lo, hi = LOWRANK_OFF, LOWRANK_OFF + LOWRANK_W
    merge_off = IN_COLS - MERGE_W
    w_parts = (w_in[:, :, merge_off:].astype(BF16), w_in[:, :, :lo].astype(BF16),
               w_in[:, :, hi:merge_off].astype(BF16))
    w_lr = w_in[:, :, lo:hi].astype(BF16)
    w_low = jnp.concatenate([w_lr, w_lr, w_lr, jnp.zeros_like(w_lr)], axis=-1)
    gu = gla_gate_up.reshape(depth, 2, GLA_RANK, GLA_HEADS, GLA_DK)
    zeros = jnp.zeros_like(gu[:, 0])
    gu_f = jnp.concatenate([gu[:, 0], zeros], axis=-1).reshape(depth, GLA_RANK, GLA_HEADS * LANES)
    gu_b = jnp.concatenate([zeros, gu[:, 1]], axis=-1).reshape(depth, GLA_RANK, GLA_HEADS * LANES)
    gu_cat = jnp.concatenate([gu_f, gu_b], axis=1)
    gu_hi = gu_cat.astype(BF16)
    gu_lo = (gu_cat - gu_hi.astype(F32)).astype(BF16)
    gate_up = jnp.concatenate([gu_hi, gu_lo, gu_hi, jnp.zeros_like(gu_hi)], axis=1)
    gb = gla_gate_b.reshape(depth, 2, GLA_HEADS, GLA_DK)
    gate_b = jnp.concatenate([gb[:, 0], gb[:, 1]], axis=-1).reshape(depth, 1, GLA_HEADS * LANES)
    lam_init = jnp.asarray([0.8 - 0.6 * math.exp(-0.3 * layer) for layer in range(depth)],
                           F32).reshape(depth, 1, 1)
    norm_g3 = norm_g[:, None, :]
    gla_norm_g3 = gla_norm_g[:, None, :]
    diff_norm_g3 = diff_norm_g[:, None, :]

    x2 = x.reshape(batch * seq, D_MODEL)
    for layer in range(depth):
        proj, la = _in_proj(x2, norm_g3, w_parts, w_low, gate_up, gate_b, cos, sin, layer, seq)
        ya = _gla(proj, la, gla_norm_g3, layer, batch, seq)
        yb = _diff(proj, diff_lambda, lam_init, diff_norm_g3, layer, batch, seq)
        yc = _dilated(proj, batch, seq)
        x2 = _merge(x2, ya, yb, yc, proj, w_branch, w_out, final_norm_g[None], layer, final=layer == depth - 1)
    return x2.reshape(batch, seq, D_MODEL)
```

```python
import functools
import math

import jax
import jax.numpy as jnp
from jax import lax
from jax.experimental import pallas as pl
from jax.experimental.pallas import tpu as pltpu

F32 = jnp.float32
BF16 = jnp.bfloat16

LANES = 128

D_MODEL = 1024
ROPE_THETA = 10000.0
NORM_EPS = 1e-6
BRANCH_W = D_MODEL // 2
N_BRANCH = 3

GLA_HEADS = 4
GLA_DV = BRANCH_W // GLA_HEADS
GLA_DK = GLA_DV // 2
GLA_RANK = 16
GLA_TAU = 16.0
GLA_CHUNK = 64
GLA_QK_W = GLA_HEADS * GLA_DK

DIFF_HEADS = 4
DIFF_D = BRANCH_W // (2 * DIFF_HEADS)
DIFF_QK_W = DIFF_HEADS * 2 * DIFF_D

DIL_HEADS = 8
DIL_HD = BRANCH_W // DIL_HEADS
DIL_W = DIL_HEADS * DIL_HD
DIL_PATTERNS = ((128, 1), (512, 4), (2048, 16))
DIL_HALF = 64
DIL_TQ = 128
NEG_INF = -1e30

MERGE_W = N_BRANCH * D_MODEL
LOWRANK_OFF = GLA_QK_W * 2 + BRANCH_W * 2
LOWRANK_W = 2 * GLA_RANK
IN_COLS = LOWRANK_OFF + LOWRANK_W + 2 * (DIFF_QK_W * 2 + BRANCH_W * 2) + MERGE_W
MAIN_W = IN_COLS - LOWRANK_W

OFF_AQ = MERGE_W
OFF_AK = OFF_AQ + GLA_QK_W
OFF_AV = OFF_AK + GLA_QK_W
OFF_AG = OFF_AV + BRANCH_W
OFF_BQ = OFF_AG + BRANCH_W
OFF_BK = OFF_BQ + DIFF_QK_W
OFF_BV = OFF_BK + DIFF_QK_W
OFF_BG = OFF_BV + BRANCH_W
OFF_CQ = OFF_BG + BRANCH_W
OFF_CK = OFF_CQ + DIL_W
OFF_CV = OFF_CK + DIL_W
OFF_CG = OFF_CV + BRANCH_W

VMEM_LIMIT = 56 * 1024 * 1024


def _dot(a, b):
    return jnp.dot(a, b, preferred_element_type=F32)


def _dot_nt(a, b):
    return lax.dot_general(a, b, (((1,), (1,)), ((), ())), preferred_element_type=F32)


def _silu(x):
    return 0.5 * x * (1.0 + jnp.tanh(0.5 * x))


def _lane_iota(shape):
    return lax.broadcasted_iota(jnp.int32, shape, len(shape) - 1)


def _split_bf16(x):
    hi = x.astype(BF16).astype(F32)
    return hi, x - hi


def _rope(x, cos, sin_signed):
    lane = _lane_iota(x.shape)
    first_half = (lane % 64) < 32
    partner = jnp.where(first_half, pltpu.roll(x, 96, 1), pltpu.roll(x, 32, 1))
    return x * cos + partner * sin_signed


_PLAIN, _ROPE_Q, _ROPE_K = 0, 1, 2
_W_MERGE, _W_A, _W_BC = 0, 1, 2
_PROJ_SEGMENTS = (
    (_W_MERGE, 0, MERGE_W // 2, _PLAIN, 0), (_W_MERGE, MERGE_W // 2, MERGE_W // 2, _PLAIN, MERGE_W // 2),
    (_W_A, 0, LOWRANK_OFF, _PLAIN, OFF_AQ),
    (_W_BC, 0, DIFF_QK_W, _ROPE_Q, OFF_BQ), (_W_BC, OFF_BK - OFF_BQ, DIFF_QK_W, _ROPE_K, OFF_BK),
    (_W_BC, OFF_BV - OFF_BQ, 2 * BRANCH_W, _PLAIN, OFF_BV),
    (_W_BC, OFF_CQ - OFF_BQ, DIL_W, _ROPE_Q, OFF_CQ), (_W_BC, OFF_CK - OFF_BQ, DIL_W, _ROPE_K, OFF_CK),
    (_W_BC, OFF_CV - OFF_BQ, 2 * BRANCH_W, _PLAIN, OFF_CV),
)
LOG2E = 1.4426950408889634
QK_SCALE = DIFF_D ** -0.5 * LOG2E


def _in_proj_kernel(x_ref, g_ref, wm_ref, wa_ref, wbc_ref, wl_ref, gu_ref, gb_ref, cos_ref, sin_ref,
                    out_ref, la_ref):
    x = x_ref[...]
    ms = jnp.mean(x * x, axis=-1, keepdims=True)
    hb = (x * lax.rsqrt(ms + NORM_EPS) * g_ref[...]).astype(BF16)
    low = _dot(hb, wl_ref[...])
    low_hi, low_lo = _split_bf16(low)
    lhs = jnp.where(_lane_iota(low.shape) < 2 * LOWRANK_W, low_hi, low_lo).astype(BF16)
    z = _dot(lhs, gu_ref[...]) + gb_ref[...]
    log_sig = jnp.minimum(z, 0.0) - jnp.log1p(jnp.exp(-jnp.abs(z)))
    la_ref[...] = log_sig * (1.0 / GLA_TAU)
    w_refs = (wm_ref, wa_ref, wbc_ref)
    for which, src, width, kind, start in _PROJ_SEGMENTS:
        res = _dot(hb, w_refs[which][:, src:src + width])
        if kind != _PLAIN:
            cos, sin = cos_ref[...], sin_ref[...]
            tiles = [_rope(res[:, c:c + LANES], cos, sin) for c in range(0, width, LANES)]
            res = jnp.concatenate(tiles, axis=1)
            if kind == _ROPE_Q:
                res = res * QK_SCALE
        out_ref[:, start:start + width] = res.astype(BF16)


def _in_proj(x2, g, w_parts, w_low, gate_up, gate_b, cos, sin, layer, seq, tm=512):
    assert DIFF_D == DIL_HD and seq % tm == 0
    rows = x2.shape[0]
    resident = dict(pipeline_mode=pl.Buffered(1))
    pos_blocks = seq // tm
    return pl.pallas_call(
        _in_proj_kernel,
        grid=(rows // tm,),
        in_specs=[
            pl.BlockSpec((tm, D_MODEL), lambda i: (i, 0)),
            pl.BlockSpec((None, 1, D_MODEL), lambda i: (layer, 0, 0)),
            *[pl.BlockSpec((None,) + w.shape[1:], lambda i: (layer, 0, 0), **resident) for w in w_parts],
            pl.BlockSpec((None, D_MODEL, LANES), lambda i: (layer, 0, 0), **resident),
            pl.BlockSpec((None, LANES, 4 * LANES), lambda i: (layer, 0, 0), **resident),
            pl.BlockSpec((None, 1, 4 * LANES), lambda i: (layer, 0, 0)),
            pl.BlockSpec((tm, LANES), lambda i: (i % pos_blocks, 0)),
            pl.BlockSpec((tm, LANES), lambda i: (i % pos_blocks, 0)),
        ],
        out_specs=[
            pl.BlockSpec((tm, MAIN_W), lambda i: (i, 0)),
            pl.BlockSpec((tm, 4 * LANES), lambda i: (i, 0)),
        ],
        out_shape=[
            jax.ShapeDtypeStruct((rows, MAIN_W), BF16),
            jax.ShapeDtypeStruct((rows, 4 * LANES), F32),
        ],
        compiler_params=pltpu.CompilerParams(
            dimension_semantics=("arbitrary",), vmem_limit_bytes=VMEM_LIMIT),
        name="in_proj",
    )(x2, g, *w_parts, w_low, gate_up, gate_b, cos, sin)


def _gla_kernel(q_ref, k_ref, v_ref, g_ref, la_ref, gn_ref, out_ref,
                u_sc, d_sc, st_sc, qh_sc, kh_sc, att_sc, *, seq, unroll):
    C = GLA_CHUNK
    n_chunks = seq // C
    lane_c = _lane_iota((C, LANES))
    row_c = lax.broadcasted_iota(jnp.int32, (C, LANES), 0)
    fwd_c = lane_c < GLA_DK
    tri = jnp.where(fwd_c, row_c - lane_c, lane_c - GLA_DK - row_c) >= 0
    row_s = lax.broadcasted_iota(jnp.int32, (2 * C, LANES), 0)
    lane_s = _lane_iota((2 * C, LANES)) % C
    cum_op = (jnp.where(row_s < C, row_s - lane_s, lane_s - (row_s - C)) >= 0).astype(F32).astype(BF16)
    fwd_sq = _lane_iota((LANES, LANES)) < GLA_DK
    scale = GLA_DK ** -0.5

    def head_cols(hh):
        return slice(hh * LANES, (hh + 1) * LANES)

    def dup_heads(ref, rows):
        xx = ref[rows, :].astype(F32)
        rolled = pltpu.roll(xx, GLA_DK, 1)
        return jnp.where(fwd_c, xx, rolled), jnp.where(fwd_c, rolled, xx)

    def prep(n, carry):
        rows = pl.ds(pl.multiple_of(n * C, C), C)
        q2 = dup_heads(q_ref, rows)
        k2 = dup_heads(k_ref, rows)
        for hh in range(2):
            cols = head_cols(hh)
            la = la_ref[rows, cols]
            la_hi, la_lo = _split_bf16(la)
            sums = _dot(cum_op, jnp.concatenate([la_hi, la_lo], axis=0).astype(BF16))
            b = jnp.where(fwd_c, sums[:C], sums[C:])
            tot = jnp.sum(la, axis=0, keepdims=True)
            decay = jnp.exp(tot)
            qh_sc[hh, rows, :] = (q2[hh] * jnp.exp(b) * scale).astype(BF16)
            k_hat = k2[hh] * jnp.exp(-b)
            kh_sc[hh, n] = jnp.concatenate(
                [jnp.where(fwd_c, k_hat, 0.0), jnp.where(fwd_c, 0.0, k_hat)], axis=0).astype(BF16)
            k_end = (k_hat * decay).astype(BF16)
            u_sc[hh, n] = lax.dot_general(v_ref[rows, cols], k_end, (((0,), (0,)), ((), ())),
                                          preferred_element_type=F32)
            d_sc[hh, n] = jnp.broadcast_to(decay, (8, LANES))
        return carry

    lax.fori_loop(0, n_chunks, prep, 0, unroll=unroll)

    def scan(i, states):
        j = n_chunks - 1 - i
        new = []
        for hh in range(2):
            st_sc[hh, i] = states[hh]
            upd = jnp.where(fwd_sq, u_sc[hh, i], u_sc[hh, j])
            dec = jnp.where(fwd_sq[:8], d_sc[hh, i], d_sc[hh, j])[0:1]
            new.append(states[hh] * dec + upd)
        return tuple(new)

    zero = jnp.zeros((LANES, LANES), F32)
    lax.fori_loop(0, n_chunks, scan, (zero, zero), unroll=4)

    def attend(n, carry):
        rows = pl.ds(pl.multiple_of(n * C, C), C)
        for hh in range(2):
            att = _dot_nt(qh_sc[hh, rows, :], kh_sc[hh, n])
            att_sc[hh, rows, :] = jnp.where(tri, att, 0.0).astype(BF16)
        return carry

    lax.fori_loop(0, n_chunks, attend, 0, unroll=unroll)

    def emit(n, carry):
        rows = pl.ds(pl.multiple_of(n * C, C), C)
        for hh in range(2):
            cols = head_cols(hh)
            v = v_ref[rows, cols]
            o = _dot(att_sc[hh, rows, :], jnp.concatenate([v, v], axis=0))
            state = jnp.where(fwd_sq, st_sc[hh, n], st_sc[hh, n_chunks - 1 - n]).astype(BF16)
            o = o + _dot_nt(qh_sc[hh, rows, :], state)
            ms = jnp.mean(o * o, axis=-1, keepdims=True)
            y = o * lax.rsqrt(ms + NORM_EPS) * gn_ref[...]
            out_ref[rows, cols] = (y * _silu(g_ref[rows, cols].astype(F32))).astype(BF16)
        return carry

    lax.fori_loop(0, n_chunks, emit, 0, unroll=unroll)


def _gla(proj, la, gn, layer, batch, seq, unroll=32):
    n_chunks = seq // GLA_CHUNK
    pair = 2 * LANES
    return pl.pallas_call(
        functools.partial(_gla_kernel, seq=seq, unroll=unroll),
        grid=(batch, GLA_HEADS // 2),
        in_specs=[
            pl.BlockSpec((seq, LANES), lambda b, p: (b, OFF_AQ // LANES + p)),
            pl.BlockSpec((seq, LANES), lambda b, p: (b, OFF_AK // LANES + p)),
            pl.BlockSpec((seq, pair), lambda b, p: (b, OFF_AV // pair + p)),
            pl.BlockSpec((seq, pair), lambda b, p: (b, OFF_AG // pair + p)),
            pl.BlockSpec((seq, pair), lambda b, p: (b, p)),
            pl.BlockSpec((None, 1, LANES), lambda b, p: (layer, 0, 0)),
        ],
        out_specs=pl.BlockSpec((seq, pair), lambda b, p: (b, p)),
        out_shape=jax.ShapeDtypeStruct((batch * seq, BRANCH_W), BF16),
        scratch_shapes=[
            pltpu.VMEM((2, n_chunks, LANES, LANES), F32),
            pltpu.VMEM((2, n_chunks, 8, LANES), F32),
            pltpu.VMEM((2, n_chunks, LANES, LANES), F32),
            pltpu.VMEM((2, seq, LANES), BF16),
            pltpu.VMEM((2, n_chunks, LANES, LANES), BF16),
            pltpu.VMEM((2, seq, LANES), BF16),
        ],
        compiler_params=pltpu.CompilerParams(
            dimension_semantics=("arbitrary", "arbitrary"), vmem_limit_bytes=VMEM_LIMIT),
        name="gla",
    )(proj, proj, proj, proj, la, gn)


def _diff_kernel(q_ref, k_ref, v_ref, g_ref, dl_ref, li_ref, gn_ref, out_ref,
                 q_sc, s0_sc, s1_sc, pm0_sc, pm1_sc, *, seq, tq):
    n_blk = seq // tq
    half0 = _lane_iota((tq, LANES)) < DIFF_D

    def stack_q(i, carry):
        q = q_ref[pl.ds(pl.multiple_of(i * tq, tq), tq), :].astype(F32)
        q_sc[i] = jnp.concatenate([jnp.where(half0, q, 0.0), jnp.where(half0, 0.0, q)], axis=0).astype(BF16)
        return carry

    lax.fori_loop(0, n_blk, stack_q, 0)

    dl = dl_ref[...]
    lam_init = li_ref[...]
    lam = (jnp.exp(jnp.sum(dl[0:1] * dl[1:2], axis=-1, keepdims=True))
           - jnp.exp(jnp.sum(dl[2:3] * dl[3:4], axis=-1, keepdims=True)) + lam_init)

    def scores(i, s_sc, pm_sc):
        s = _dot_nt(q_sc[i], k_ref[...])
        s_sc[...] = s
        pm_sc[...] = functools.reduce(jnp.maximum, [s[:, c:c + LANES] for c in range(0, seq, LANES)])

    def finish(i, s_sc, pm_sc):
        rows = pl.ds(i * tq, tq)
        m = jnp.max(pm_sc[...], axis=-1, keepdims=True)
        e = jnp.exp2(s_sc[...] - m)
        l = jnp.sum(e, axis=-1, keepdims=True)
        o2 = _dot(e.astype(BF16), v_ref[...]) * (1.0 / l)
        o = o2[:tq] - lam * o2[tq:]
        ms = jnp.mean(o * o, axis=-1, keepdims=True)
        y = o * lax.rsqrt(ms + NORM_EPS) * gn_ref[...] * (1.0 - lam_init)
        out_ref[rows, :] = (y * _silu(g_ref[rows, :].astype(F32))).astype(BF16)

    bufs = ((s0_sc, pm0_sc), (s1_sc, pm1_sc))
    scores(0, *bufs[0])
    for i in range(n_blk):
        if i + 1 < n_blk:
            scores(i + 1, *bufs[(i + 1) % 2])
        finish(i, *bufs[i % 2])


def _diff(proj, dl, lam_init, gn, layer, batch, seq, tq=256):
    blk = lambda off: pl.BlockSpec((seq, LANES), lambda b, h: (b, off // LANES + h))
    per_layer = lambda shape: pl.BlockSpec((None,) + shape, lambda b, h: (layer, 0, 0))
    return pl.pallas_call(
        functools.partial(_diff_kernel, seq=seq, tq=tq),
        grid=(batch, DIFF_HEADS),
        in_specs=[blk(OFF_BQ), blk(OFF_BK), blk(OFF_BV), blk(OFF_BG),
                  per_layer((4, DIFF_D)), per_layer((1, 1)), per_layer((1, LANES))],
        out_specs=pl.BlockSpec((seq, LANES), lambda b, h: (b, h)),
        out_shape=jax.ShapeDtypeStruct((batch * seq, BRANCH_W), BF16),
        scratch_shapes=[pltpu.VMEM((seq // tq, 2 * tq, LANES), BF16),
                        pltpu.VMEM((2 * tq, seq), F32), pltpu.VMEM((2 * tq, seq), F32),
                        pltpu.VMEM((2 * tq, LANES), F32), pltpu.VMEM((2 * tq, LANES), F32)],
        compiler_params=pltpu.CompilerParams(
            dimension_semantics=("arbitrary", "arbitrary"), vmem_limit_bytes=VMEM_LIMIT),
        name="diff",
    )(proj, proj, proj, proj, dl, lam_init, gn)


def _dilated_kernel(q_ref, k_ref, v_ref, g_ref, out_ref,
                    nat_sc, c4_sc, k4_sc, q16_sc, k16_sc, w1_sc, w4_sc, w16_sc,
                    s0_sc, s1_sc, s2_sc, e0_sc, e1_sc, e2_sc,
                    o_sc, m_sc, msw_sc, d_sc, bias_sc, *, seq, unroll):
    tq, win = DIL_TQ, 2 * DIL_TQ
    stage_rows = 2 * tq
    n_stage = seq // stage_rows
    len4, len16 = seq // 4, seq // 16
    head0 = _lane_iota((tq, LANES)) < DIL_HD
    head0_stage = _lane_iota((stage_rows, LANES)) < DIL_HD

    dcol = (lax.broadcasted_iota(jnp.int32, (tq, win), 1) - lax.broadcasted_iota(jnp.int32, (tq, win), 0))
    for idx in range(3):
        bias_sc[idx] = jnp.where(jnp.abs(dcol - idx * DIL_HALF) <= DIL_HALF, 0.0, NEG_INF)

    def stage(i, carry):
        rows = pl.ds(pl.multiple_of(i * stage_rows, stage_rows), stage_rows)
        nat_sc[0, rows, :] = q_ref[rows, :].astype(F32)
        nat_sc[1, rows, :] = k_ref[rows, :].astype(F32)
        nat_sc[2, rows, :] = v_ref[rows, :].astype(F32)
        return carry

    lax.fori_loop(0, n_stage, stage, 0, unroll=2)

    for t in range(3):
        for r in range(4):
            c4_sc[t, r * len4:(r + 1) * len4, :] = nat_sc[t, pl.ds(r, len4, stride=4), :]
    for r4 in range(4):
        for j in range(4):
            dst = slice((r4 + 4 * j) * len16, (r4 + 4 * j + 1) * len16)
            src = pl.ds(r4 * len4 + j, len16, stride=4)
            q16_sc[dst, :] = c4_sc[0, src, :]
            k16_sc[dst, :] = c4_sc[1, src, :].astype(BF16)
            w16_sc[0, dst, :] = c4_sc[2, src, :].astype(BF16)

    def value_weights(i, carry):
        rows = pl.ds(pl.multiple_of(i * stage_rows, stage_rows), stage_rows)
        k4_sc[rows, :] = c4_sc[1, rows, :].astype(BF16)
        for src, dst in ((v_ref, w1_sc), (c4_sc.at[2], w4_sc), (w16_sc.at[0], w16_sc)):
            v = src[rows, :].astype(F32)
            dst[1, rows, :] = jnp.where(head0_stage, 1.0, v).astype(BF16)
            dst[0, rows, :] = jnp.where(head0_stage, v, 1.0).astype(BF16)
        return carry

    lax.fori_loop(0, n_stage, value_weights, 0, unroll=2)

    def make_pattern(g, q_src, k_src, w_src, s_sc, e_sc, seq_len, w, out_rows):
        per_seq = seq_len // tq

        def where(t):
            r, i = t // per_seq, t % per_seq
            ks = jnp.clip(i * tq - tq // 2, 0, seq_len - w)
            which = jnp.where(i == 0, 0, jnp.where(i == per_seq - 1, 2, 1)) if per_seq > 1 else 0
            qrows = pl.ds(pl.multiple_of(r * seq_len + i * tq, tq), tq)
            krows = pl.ds(pl.multiple_of(r * seq_len + ks, tq // 2), w)
            return qrows, krows, which

        def scores(t):
            qrows, krows, which = where(t)
            q = q_src[qrows, :]
            q2 = jnp.concatenate([jnp.where(head0, q, 0.0), jnp.where(head0, 0.0, q)], axis=0)
            bias = bias_sc[which, :, :w]
            s = _dot_nt(q2.astype(BF16), k_src[krows, :])
            s_sc[t, :, :w] = s + jnp.concatenate([bias, bias], axis=0)

        def softmax(t):
            s = s_sc[t, :, :w]
            m = jnp.max(s, axis=-1, keepdims=True)
            e_sc[t, :, :w] = jnp.exp2(s - m).astype(BF16)
            m_sc[g, out_rows(t), :] = jnp.where(head0, m[:tq], m[tq:])
            msw_sc[g, out_rows(t), :] = jnp.where(head0, m[tq:], m[:tq])

        def values(t):
            _, krows, _ = where(t)
            out0 = _dot(e_sc[t, :tq, :w], w_src[0, krows, :])
            out1 = _dot(e_sc[t, tq:, :w], w_src[1, krows, :])
            rows = out_rows(t)
            o_sc[g, rows, :] = jnp.where(head0, out0, out1)
            d_sc[g, rows, :] = jnp.where(head0, out1, out0)

        return scores, softmax, values

    sc0, sm0, va0 = make_pattern(0, nat_sc.at[0], k_ref, w1_sc, s0_sc, e0_sc, seq, win,
                                 lambda t: pl.ds(pl.multiple_of(t * tq, tq), tq))
    sc1, sm1, va1 = make_pattern(1, c4_sc.at[0], k4_sc, w4_sc, s1_sc, e1_sc, len4, win,
                                 lambda t: pl.ds(pl.multiple_of(t * tq, tq), tq))
    sc2, sm2, va2 = make_pattern(2, q16_sc, k16_sc, w16_sc, s2_sc, e2_sc, len16, len16,
                                 lambda t: pl.ds((t % 4) * len4 + t // 4, len16, stride=4))

    for passes in ((sc0,), (sm0, sc1), (va0, sm1, sc2), (va1, sm2), (va2,)):
        def body(t, carry, passes=passes):
            for f in passes:
                f(t)
            return carry

        lax.fori_loop(0, seq // tq, body, 0, unroll=unroll)

    y_sc = nat_sc.at[0]
    per_class = len4 // stage_rows

    def combine(i, carry):
        r, m0 = i // per_class, (i % per_class) * stage_rows
        rows = pl.ds(pl.multiple_of(r * len4 + m0, stage_rows), stage_rows)
        nat_rows = pl.ds(r + 4 * m0, stage_rows, stride=4)
        m0, m1, m2 = m_sc[0, nat_rows, :], m_sc[1, rows, :], m_sc[2, rows, :]
        m = jnp.maximum(jnp.maximum(m0, m1), m2)
        w0, w1, w2 = jnp.exp2(m0 - m), jnp.exp2(m1 - m), jnp.exp2(m2 - m)
        o = w0 * o_sc[0, nat_rows, :] + w1 * o_sc[1, rows, :] + w2 * o_sc[2, rows, :]
        n0, n1, n2 = msw_sc[0, nat_rows, :], msw_sc[1, rows, :], msw_sc[2, rows, :]
        n = jnp.maximum(jnp.maximum(n0, n1), n2)
        den = (jnp.exp2(n0 - n) * d_sc[0, nat_rows, :] + jnp.exp2(n1 - n) * d_sc[1, rows, :]
               + jnp.exp2(n2 - n) * d_sc[2, rows, :])
        den = pltpu.roll(den, DIL_HD, 1)
        y_sc[nat_rows, :] = o * (1.0 / den)
        return carry

    lax.fori_loop(0, n_stage, combine, 0, unroll=2)

    def gate(i, carry):
        rows = pl.ds(pl.multiple_of(i * stage_rows, stage_rows), stage_rows)
        out_ref[rows, :] = (y_sc[rows, :] * _silu(g_ref[rows, :].astype(F32))).astype(BF16)
        return carry

    lax.fori_loop(0, n_stage, gate, 0, unroll=2)


def _dilated(proj, batch, seq, unroll=16):
    assert seq // 16 == DIL_TQ and all(w // (2 * d) == DIL_HALF for w, d in DIL_PATTERNS)
    blk = lambda off: pl.BlockSpec((seq, LANES), lambda b, p: (b, off // LANES + p))
    n_blocks = seq // DIL_TQ
    f32_rows = pltpu.VMEM((seq, LANES), F32)
    bf16_rows = pltpu.VMEM((seq, LANES), BF16)
    bf16_pair = pltpu.VMEM((2, seq, LANES), BF16)
    s_buf = pltpu.VMEM((n_blocks, 2 * DIL_TQ, 2 * DIL_TQ), F32)
    e_buf = pltpu.VMEM((n_blocks, 2 * DIL_TQ, 2 * DIL_TQ), BF16)
    return pl.pallas_call(
        functools.partial(_dilated_kernel, seq=seq, unroll=unroll),
        grid=(batch, DIL_HEADS // 2),
        in_specs=[blk(OFF_CQ), blk(OFF_CK), blk(OFF_CV), blk(OFF_CG)],
        out_specs=pl.BlockSpec((seq, LANES), lambda b, p: (b, p)),
        out_shape=jax.ShapeDtypeStruct((batch * seq, BRANCH_W), BF16),
        scratch_shapes=[
            pltpu.VMEM((3, seq, LANES), F32),
            pltpu.VMEM((3, seq, LANES), F32), bf16_rows, f32_rows, bf16_rows,
            bf16_pair, bf16_pair, bf16_pair,
            s_buf, s_buf, s_buf, e_buf, e_buf, e_buf,
            pltpu.VMEM((3, seq, LANES), F32), pltpu.VMEM((3, seq, LANES), F32),
            pltpu.VMEM((3, seq, LANES), F32), pltpu.VMEM((3, seq, LANES), F32),
            pltpu.VMEM((3, DIL_TQ, 2 * DIL_TQ), F32),
        ],
        compiler_params=pltpu.CompilerParams(
            dimension_semantics=("arbitrary", "arbitrary"), vmem_limit_bytes=VMEM_LIMIT),
        name="dilated",
    )(proj, proj, proj, proj)


def _merge_kernel(x_ref, ya_ref, yb_ref, yc_ref, m_ref, wb_ref, wo_ref, fg_ref, out_ref, wb_sc, wo_sc,
                  *, final):
    @pl.when(pl.program_id(0) == 0)
    def _():
        wb_sc[...] = wb_ref[...].astype(BF16)
        wo_sc[...] = wo_ref[...].astype(BF16)

    mixed = None
    for gi, y_ref in enumerate((ya_ref, yb_ref, yc_ref)):
        z = _dot(y_ref[...], wb_sc[gi])
        gate = 0.5 * jnp.tanh(0.5 * m_ref[:, gi * D_MODEL:(gi + 1) * D_MODEL].astype(F32)) + 0.5
        mixed = gate * z if mixed is None else mixed + gate * z
    x = x_ref[...] + _dot(mixed.astype(BF16), wo_sc[...])
    if final:
        ms = jnp.mean(x * x, axis=-1, keepdims=True)
        x = x * lax.rsqrt(ms + NORM_EPS) * fg_ref[...]
    out_ref[...] = x


def _merge(x2, ya, yb, yc, proj, wb, wo, final_g, layer, final, tm=1024):
    rows = x2.shape[0]
    yspec = pl.BlockSpec((tm, BRANCH_W), lambda i: (i, 0))
    return pl.pallas_call(
        functools.partial(_merge_kernel, final=final),
        grid=(rows // tm,),
        in_specs=[
            pl.BlockSpec((tm, D_MODEL), lambda i: (i, 0)),
            yspec, yspec, yspec,
            pl.BlockSpec((tm, MERGE_W), lambda i: (i, 0)),
            pl.BlockSpec((None, N_BRANCH, BRANCH_W, D_MODEL), lambda i: (layer, 0, 0, 0),
                         pipeline_mode=pl.Buffered(1)),
            pl.BlockSpec((None, D_MODEL, D_MODEL), lambda i: (layer, 0, 0), pipeline_mode=pl.Buffered(1)),
            pl.BlockSpec((1, D_MODEL), lambda i: (0, 0)),
        ],
        out_specs=pl.BlockSpec((tm, D_MODEL), lambda i: (i, 0)),
        out_shape=jax.ShapeDtypeStruct((rows, D_MODEL), F32),
        scratch_shapes=[pltpu.VMEM((N_BRANCH, BRANCH_W, D_MODEL), BF16), pltpu.VMEM((D_MODEL, D_MODEL), BF16)],
        compiler_params=pltpu.CompilerParams(
            dimension_semantics=("arbitrary",), vmem_limit_bytes=VMEM_LIMIT),
        name="merge",
    )(x2, ya, yb, yc, proj, wb, wo, final_g)


def _rope_tables(seq, dim):
    inv = 1.0 / (ROPE_THETA ** (jnp.arange(0, dim, 2, dtype=F32) / dim))
    ang = jnp.arange(seq, dtype=F32)[:, None] * inv[None, :]
    cos, sin = jnp.cos(ang), jnp.sin(ang)
    reps = LANES // dim
    cos_t = jnp.tile(jnp.concatenate([cos, cos], axis=-1), (1, reps))
    sin_t = jnp.tile(jnp.concatenate([-sin, sin], axis=-1), (1, reps))
    return cos_t, sin_t


def kernel(x, norm_g, w_in, gla_gate_up, gla_gate_b, gla_norm_g, diff_lambda, diff_norm_g, w_branch, w_out, final_norm_g):
    batch, seq, _ = x.shape
    depth = w_in.shape[0]
    cos, sin = _rope_tables(seq, DIFF_D)

    lo, hi = LOWRANK_OFF, LOWRANK_OFF + LOWRANK_W
    merge_off = IN_COLS - MERGE_W
    w_parts = (w_in[:, :, merge_off:].astype(BF16), w_in[:, :, :lo].astype(BF16),
               w_in[:, :, hi:merge_off].astype(BF16))
    w_lr = w_in[:, :, lo:hi].astype(BF16)
    w_low = jnp.concatenate([w_lr, w_lr, w_lr, jnp.zeros_like(w_lr)], axis=-1)
    gu = gla_gate_up.reshape(depth, 2, GLA_RANK, GLA_HEADS, GLA_DK)
    zeros = jnp.zeros_like(gu[:, 0])
    gu_f = jnp.concatenate([gu[:, 0], zeros], axis=-1).reshape(depth, GLA_RANK, GLA_HEADS * LANES)
    gu_b = jnp.concatenate([zeros, gu[:, 1]], axis=-1).reshape(depth, GLA_RANK, GLA_HEADS * LANES)
    gu_cat = jnp.concatenate([gu_f, gu_b], axis=1)
    gu_hi = gu_cat.astype(BF16)
    gu_lo = (gu_cat - gu_hi.astype(F32)).astype(BF16)
    gate_up = jnp.concatenate([gu_hi, gu_lo, gu_hi, jnp.zeros_like(gu_hi)], axis=1)
    gb = gla_gate_b.reshape(depth, 2, GLA_HEADS, GLA_DK)
    gate_b = jnp.concatenate([gb[:, 0], gb[:, 1]], axis=-1).reshape(depth, 1, GLA_HEADS * LANES)
    lam_init = jnp.asarray([0.8 - 0.6 * math.exp(-0.3 * layer) for layer in range(depth)],
                           F32).reshape(depth, 1, 1)
    norm_g3 = norm_g[:, None, :]
    gla_norm_g3 = gla_norm_g[:, None, :]
    diff_norm_g3 = diff_norm_g[:, None, :]

    x2 = x.reshape(batch * seq, D_MODEL)
    for layer in range(depth):
        proj, la = _in_proj(x2, norm_g3, w_parts, w_low, gate_up, gate_b, cos, sin, layer, seq)
        ya = _gla(proj, la, gla_norm_g3, layer, batch, seq)
        yb = _diff(proj, diff_lambda, lam_init, diff_norm_g3, layer, batch, seq)
        yc = _dilated(proj, batch, seq)
        x2 = _merge(x2, ya, yb, yc, proj, w_branch, w_out, final_norm_g[None], layer, final=layer == depth - 1)
    return x2.reshape(batch, seq, D_MODEL)
```

```python
import functools
import math

import jax
import jax.numpy as jnp
from jax import lax
from jax.experimental import pallas as pl
from jax.experimental.pallas import tpu as pltpu

F32 = jnp.float32
BF16 = jnp.bfloat16

LANES = 128

D_MODEL = 1024
ROPE_THETA = 10000.0
NORM_EPS = 1e-6
BRANCH_W = D_MODEL // 2
N_BRANCH = 3

GLA_HEADS = 4
GLA_DV = BRANCH_W // GLA_HEADS
GLA_DK = GLA_DV // 2
GLA_RANK = 16
GLA_TAU = 16.0
GLA_CHUNK = 64
GLA_QK_W = GLA_HEADS * GLA_DK

DIFF_HEADS = 4
DIFF_D = BRANCH_W // (2 * DIFF_HEADS)
DIFF_QK_W = DIFF_HEADS * 2 * DIFF_D

DIL_HEADS = 8
DIL_HD = BRANCH_W // DIL_HEADS
DIL_W = DIL_HEADS * DIL_HD
DIL_PATTERNS = ((128, 1), (512, 4), (2048, 16))
DIL_HALF = 64
DIL_TQ = 128
NEG_INF = -1e30

MERGE_W = N_BRANCH * D_MODEL
LOWRANK_OFF = GLA_QK_W * 2 + BRANCH_W * 2
LOWRANK_W = 2 * GLA_RANK
IN_COLS = LOWRANK_OFF + LOWRANK_W + 2 * (DIFF_QK_W * 2 + BRANCH_W * 2) + MERGE_W
MAIN_W = IN_COLS - LOWRANK_W

OFF_AQ = MERGE_W
OFF_AK = OFF_AQ + GLA_QK_W
OFF_AV = OFF_AK + GLA_QK_W
OFF_AG = OFF_AV + BRANCH_W
OFF_BQ = OFF_AG + BRANCH_W
OFF_BK = OFF_BQ + DIFF_QK_W
OFF_BV = OFF_BK + DIFF_QK_W
OFF_BG = OFF_BV + BRANCH_W
OFF_CQ = OFF_BG + BRANCH_W
OFF_CK = OFF_CQ + DIL_W
OFF_CV = OFF_CK + DIL_W
OFF_CG = OFF_CV + BRANCH_W

VMEM_LIMIT = 56 * 1024 * 1024


def _dot(a, b):
    return jnp.dot(a, b, preferred_element_type=F32)


def _dot_nt(a, b):
    return lax.dot_general(a, b, (((1,), (1,)), ((), ())), preferred_element_type=F32)


def _silu(x):
    return 0.5 * x * (1.0 + jnp.tanh(0.5 * x))


def _lane_iota(shape):
    return lax.broadcasted_iota(jnp.int32, shape, len(shape) - 1)


def _split_bf16(x):
    hi = x.astype(BF16).astype(F32)
    return hi, x - hi


def _rope(x, cos, sin_signed):
    lane = _lane_iota(x.shape)
    first_half = (lane % 64) < 32
    partner = jnp.where(first_half, pltpu.roll(x, 96, 1), pltpu.roll(x, 32, 1))
    return x * cos + partner * sin_signed


_PLAIN, _ROPE_Q, _ROPE_K = 0, 1, 2
_W_MERGE, _W_A, _W_BC = 0, 1, 2
_PROJ_SEGMENTS = (
    (_W_MERGE, 0, MERGE_W // 2, _PLAIN, 0), (_W_MERGE, MERGE_W // 2, MERGE_W // 2, _PLAIN, MERGE_W // 2),
    (_W_A, 0, LOWRANK_OFF, _PLAIN, OFF_AQ),
    (_W_BC, 0, DIFF_QK_W, _ROPE_Q, OFF_BQ), (_W_BC, OFF_BK - OFF_BQ, DIFF_QK_W, _ROPE_K, OFF_BK),
    (_W_BC, OFF_BV - OFF_BQ, 2 * BRANCH_W, _PLAIN, OFF_BV),
    (_W_BC, OFF_CQ - OFF_BQ, DIL_W, _ROPE_Q, OFF_CQ), (_W_BC, OFF_CK - OFF_BQ, DIL_W, _ROPE_K, OFF_CK),
    (_W_BC, OFF_CV - OFF_BQ, 2 * BRANCH_W, _PLAIN, OFF_CV),
)
LOG2E = 1.4426950408889634
QK_SCALE = DIFF_D ** -0.5 * LOG2E


def _in_proj_kernel(x_ref, g_ref, wm_ref, wa_ref, wbc_ref, wl_ref, gu_ref, gb_ref, cos_ref, sin_ref,
                    out_ref, la_ref):
    x = x_ref[...]
    ms = jnp.mean(x * x, axis=-1, keepdims=True)
    hb = (x * lax.rsqrt(ms + NORM_EPS) * g_ref[...]).astype(BF16)
    low = _dot(hb, wl_ref[...])
    low_hi, low_lo = _split_bf16(low)
    lhs = jnp.where(_lane_iota(low.shape) < 2 * LOWRANK_W, low_hi, low_lo).astype(BF16)
    z = _dot(lhs, gu_ref[...]) + gb_ref[...]
    log_sig = jnp.minimum(z, 0.0) - jnp.log1p(jnp.exp(-jnp.abs(z)))
    la_ref[...] = log_sig * (1.0 / GLA_TAU)
    w_refs = (wm_ref, wa_ref, wbc_ref)
    for which, src, width, kind, start in _PROJ_SEGMENTS:
        res = _dot(hb, w_refs[which][:, src:src + width])
        if kind != _PLAIN:
            cos, sin = cos_ref[...], sin_ref[...]
            tiles = [_rope(res[:, c:c + LANES], cos, sin) for c in range(0, width, LANES)]
            res = jnp.concatenate(tiles, axis=1)
            if kind == _ROPE_Q:
                res = res * QK_SCALE
        out_ref[:, start:start + width] = res.astype(BF16)


def _in_proj(x2, g, w_parts, w_low, gate_up, gate_b, cos, sin, layer, seq, tm=512):
    assert DIFF_D == DIL_HD and seq % tm == 0
    rows = x2.shape[0]
    resident = dict(pipeline_mode=pl.Buffered(1))
    pos_blocks = seq // tm
    return pl.pallas_call(
        _in_proj_kernel,
        grid=(rows // tm,),
        in_specs=[
            pl.BlockSpec((tm, D_MODEL), lambda i: (i, 0)),
            pl.BlockSpec((None, 1, D_MODEL), lambda i: (layer, 0, 0)),
            *[pl.BlockSpec((None,) + w.shape[1:], lambda i: (layer, 0, 0), **resident) for w in w_parts],
            pl.BlockSpec((None, D_MODEL, LANES), lambda i: (layer, 0, 0), **resident),
            pl.BlockSpec((None, LANES, 4 * LANES), lambda i: (layer, 0, 0), **resident),
            pl.BlockSpec((None, 1, 4 * LANES), lambda i: (layer, 0, 0)),
            pl.BlockSpec((tm, LANES), lambda i: (i % pos_blocks, 0)),
            pl.BlockSpec((tm, LANES), lambda i: (i % pos_blocks, 0)),
        ],
        out_specs=[
            pl.BlockSpec((tm, MAIN_W), lambda i: (i, 0)),
            pl.BlockSpec((tm, 4 * LANES), lambda i: (i, 0)),
        ],
        out_shape=[
            jax.ShapeDtypeStruct((rows, MAIN_W), BF16),
            jax.ShapeDtypeStruct((rows, 4 * LANES), F32),
        ],
        compiler_params=pltpu.CompilerParams(
            dimension_semantics=("arbitrary",), vmem_limit_bytes=VMEM_LIMIT),
        name="in_proj",
    )(x2, g, *w_parts, w_low, gate_up, gate_b, cos, sin)


def _gla_kernel(q_ref, k_ref, v_ref, g_ref, la_ref, gn_ref, out_ref,
                u_sc, d_sc, st_sc, qh_sc, kh_sc, att_sc, *, seq, unroll):
    C = GLA_CHUNK
    n_chunks = seq // C
    lane_c = _lane_iota((C, LANES))
    row_c = lax.broadcasted_iota(jnp.int32, (C, LANES), 0)
    fwd_c = lane_c < GLA_DK
    tri = jnp.where(fwd_c, row_c - lane_c, lane_c - GLA_DK - row_c) >= 0
    row_s = lax.broadcasted_iota(jnp.int32, (2 * C, LANES), 0)
    lane_s = _lane_iota((2 * C, LANES)) % C
    cum_op = (jnp.where(row_s < C, row_s - lane_s, lane_s - (row_s - C)) >= 0).astype(F32).astype(BF16)
    fwd_sq = _lane_iota((LANES, LANES)) < GLA_DK
    scale = GLA_DK ** -0.5

    def head_cols(hh):
        return slice(hh * LANES, (hh + 1) * LANES)

    def dup_heads(ref, rows):
        xx = ref[rows, :].astype(F32)
        rolled = pltpu.roll(xx, GLA_DK, 1)
        return jnp.where(fwd_c, xx, rolled), jnp.where(fwd_c, rolled, xx)

    def prep(n, carry):
        rows = pl.ds(pl.multiple_of(n * C, C), C)
        q2 = dup_heads(q_ref, rows)
        k2 = dup_heads(k_ref, rows)
        for hh in range(2):
            cols = head_cols(hh)
            la = la_ref[rows, cols]
            la_hi, la_lo = _split_bf16(la)
            sums = _dot(cum_op, jnp.concatenate([la_hi, la_lo], axis=0).astype(BF16))
            b = jnp.where(fwd_c, sums[:C], sums[C:])
            tot = jnp.sum(la, axis=0, keepdims=True)
            decay = jnp.exp(tot)
            qh_sc[hh, rows, :] = (q2[hh] * jnp.exp(b) * scale).astype(BF16)
            k_hat = k2[hh] * jnp.exp(-b)
            kh_sc[hh, n] = jnp.concatenate(
                [jnp.where(fwd_c, k_hat, 0.0), jnp.where(fwd_c, 0.0, k_hat)], axis=0).astype(BF16)
            k_end = (k_hat * decay).astype(BF16)
            u_sc[hh, n] = lax.dot_general(v_ref[rows, cols], k_end, (((0,), (0,)), ((), ())),
                                          preferred_element_type=F32)
            d_sc[hh, n] = jnp.broadcast_to(decay, (8, LANES))
        return carry

    lax.fori_loop(0, n_chunks, prep, 0, unroll=unroll)

    def scan(i, states):
        j = n_chunks - 1 - i
        new = []
        for hh in range(2):
            st_sc[hh, i] = states[hh]
            upd = jnp.where(fwd_sq, u_sc[hh, i], u_sc[hh, j])
            dec = jnp.where(fwd_sq[:8], d_sc[hh, i], d_sc[hh, j])[0:1]
            new.append(states[hh] * dec + upd)
        return tuple(new)

    zero = jnp.zeros((LANES, LANES), F32)
    lax.fori_loop(0, n_chunks, scan, (zero, zero), unroll=4)

    def attend(n, carry):
        rows = pl.ds(pl.multiple_of(n * C, C), C)
        for hh in range(2):
            att = _dot_nt(qh_sc[hh, rows, :], kh_sc[hh, n])
            att_sc[hh, rows, :] = jnp.where(tri, att, 0.0).astype(BF16)
        return carry

    lax.fori_loop(0, n_chunks, attend, 0, unroll=unroll)

    def emit(n, carry):
        rows = pl.ds(pl.multiple_of(n * C, C), C)
        for hh in range(2):
            cols = head_cols(hh)
            v = v_ref[rows, cols]
            o = _dot(att_sc[hh, rows, :], jnp.concatenate([v, v], axis=0))
            state = jnp.where(fwd_sq, st_sc[hh, n], st_sc[hh, n_chunks - 1 - n]).astype(BF16)
            o = o + _dot_nt(qh_sc[hh, rows, :], state)
            ms = jnp.mean(o * o, axis=-1, keepdims=True)
            y = o * lax.rsqrt(ms + NORM_EPS) * gn_ref[...]
            out_ref[rows, cols] = (y * _silu(g_ref[rows, cols].astype(F32))).astype(BF16)
        return carry

    lax.fori_loop(0, n_chunks, emit, 0, unroll=unroll)


def _gla(proj, la, gn, layer, batch, seq, unroll=32):
    n_chunks = seq // GLA_CHUNK
    pair = 2 * LANES
    return pl.pallas_call(
        functools.partial(_gla_kernel, seq=seq, unroll=unroll),
        grid=(batch, GLA_HEADS // 2),
        in_specs=[
            pl.BlockSpec((seq, LANES), lambda b, p: (b, OFF_AQ // LANES + p)),
            pl.BlockSpec((seq, LANES), lambda b, p: (b, OFF_AK // LANES + p)),
            pl.BlockSpec((seq, pair), lambda b, p: (b, OFF_AV // pair + p)),
            pl.BlockSpec((seq, pair), lambda b, p: (b, OFF_AG // pair + p)),
            pl.BlockSpec((seq, pair), lambda b, p: (b, p)),
            pl.BlockSpec((None, 1, LANES), lambda b, p: (layer, 0, 0)),
        ],
        out_specs=pl.BlockSpec((seq, pair), lambda b, p: (b, p)),
        out_shape=jax.ShapeDtypeStruct((batch * seq, BRANCH_W), BF16),
        scratch_shapes=[
            pltpu.VMEM((2, n_chunks, LANES, LANES), F32),
            pltpu.VMEM((2, n_chunks, 8, LANES), F32),
            pltpu.VMEM((2, n_chunks, LANES, LANES), F32),
            pltpu.VMEM((2, seq, LANES), BF16),
            pltpu.VMEM((2, n_chunks, LANES, LANES), BF16),
            pltpu.VMEM((2, seq, LANES), BF16),
        ],
        compiler_params=pltpu.CompilerParams(
            dimension_semantics=("arbitrary", "arbitrary"), vmem_limit_bytes=VMEM_LIMIT),
        name="gla",
    )(proj, proj, proj, proj, la, gn)


def _diff_kernel(q_ref, k_ref, v_ref, g_ref, dl_ref, li_ref, gn_ref, out_ref,
                 q_sc, s0_sc, s1_sc, pm0_sc, pm1_sc, *, seq, tq):
    n_blk = seq // tq
    half0 = _lane_iota((tq, LANES)) < DIFF_D

    def stack_q(i, carry):
        q = q_ref[pl.ds(pl.multiple_of(i * tq, tq), tq), :].astype(F32)
        q_sc[i] = jnp.concatenate([jnp.where(half0, q, 0.0), jnp.where(half0, 0.0, q)], axis=0).astype(BF16)
        return carry

    lax.fori_loop(0, n_blk, stack_q, 0)

    dl = dl_ref[...]
    lam_init = li_ref[...]
    lam = (jnp.exp(jnp.sum(dl[0:1] * dl[1:2], axis=-1, keepdims=True))
           - jnp.exp(jnp.sum(dl[2:3] * dl[3:4], axis=-1, keepdims=True)) + lam_init)

    def scores(i, s_sc, pm_sc):
        s = _dot_nt(q_sc[i], k_ref[...])
        s_sc[...] = s
        pm_sc[...] = functools.reduce(jnp.maximum, [s[:, c:c + LANES] for c in range(0, seq, LANES)])

    def finish(i, s_sc, pm_sc):
        rows = pl.ds(i * tq, tq)
        m = jnp.max(pm_sc[...], axis=-1, keepdims=True)
        e = jnp.exp2(s_sc[...] - m)
        l = jnp.sum(e, axis=-1, keepdims=True)
        o2 = _dot(e.astype(BF16), v_ref[...]) * (1.0 / l)
        o = o2[:tq] - lam * o2[tq:]
        ms = jnp.mean(o * o, axis=-1, keepdims=True)
        y = o * lax.rsqrt(ms + NORM_EPS) * gn_ref[...] * (1.0 - lam_init)
        out_ref[rows, :] = (y * _silu(g_ref[rows, :].astype(F32))).astype(BF16)

    bufs = ((s0_sc, pm0_sc), (s1_sc, pm1_sc))
    scores(0, *bufs[0])
    for i in range(n_blk):
        if i + 1 < n_blk:
            scores(i + 1, *bufs[(i + 1) % 2])
        finish(i, *bufs[i % 2])


def _diff(proj, dl, lam_init, gn, layer, batch, seq, tq=256):
    blk = lambda off: pl.BlockSpec((seq, LANES), lambda b, h: (b, off // LANES + h))
    per_layer = lambda shape: pl.BlockSpec((None,) + shape, lambda b, h: (layer, 0, 0))
    return pl.pallas_call(
        functools.partial(_diff_kernel, seq=seq, tq=tq),
        grid=(batch, DIFF_HEADS),
        in_specs=[blk(OFF_BQ), blk(OFF_BK), blk(OFF_BV), blk(OFF_BG),
                  per_layer((4, DIFF_D)), per_layer((1, 1)), per_layer((1, LANES))],
        out_specs=pl.BlockSpec((seq, LANES), lambda b, h: (b, h)),
        out_shape=jax.ShapeDtypeStruct((batch * seq, BRANCH_W), BF16),
        scratch_shapes=[pltpu.VMEM((seq // tq, 2 * tq, LANES), BF16),
                        pltpu.VMEM((2 * tq, seq), F32), pltpu.VMEM((2 * tq, seq), F32),
                        pltpu.VMEM((2 * tq, LANES), F32), pltpu.VMEM((2 * tq, LANES), F32)],
        compiler_params=pltpu.CompilerParams(
            dimension_semantics=("arbitrary", "arbitrary"), vmem_limit_bytes=VMEM_LIMIT),
        name="diff",
    )(proj, proj, proj, proj, dl, lam_init, gn)


def _dilated_kernel(q_ref, k_ref, v_ref, g_ref, out_ref,
                    nat_sc, c4_sc, k4_sc, q16_sc, k16_sc, w1_sc, w4_sc, w16_sc,
                    s0_sc, s1_sc, s2_sc, e0_sc, e1_sc, e2_sc,
                    o_sc, m_sc, msw_sc, d_sc, bias_sc, *, seq, unroll):
    tq, win = DIL_TQ, 2 * DIL_TQ
    stage_rows = 2 * tq
    n_stage = seq // stage_rows
    len4, len16 = seq // 4, seq // 16
    head0 = _lane_iota((tq, LANES)) < DIL_HD
    head0_stage = _lane_iota((stage_rows, LANES)) < DIL_HD

    dcol = (lax.broadcasted_iota(jnp.int32, (tq, win), 1) - lax.broadcasted_iota(jnp.int32, (tq, win), 0))
    for idx in range(3):
        bias_sc[idx] = jnp.where(jnp.abs(dcol - idx * DIL_HALF) <= DIL_HALF, 0.0, NEG_INF)

    def stage(i, carry):
        rows = pl.ds(pl.multiple_of(i * stage_rows, stage_rows), stage_rows)
        nat_sc[0, rows, :] = q_ref[rows, :].astype(F32)
        nat_sc[1, rows, :] = k_ref[rows, :].astype(F32)
        nat_sc[2, rows, :] = v_ref[rows, :].astype(F32)
        return carry

    lax.fori_loop(0, n_stage, stage, 0, unroll=True)

    for t in range(3):
        for r in range(4):
            c4_sc[t, r * len4:(r + 1) * len4, :] = nat_sc[t, pl.ds(r, len4, stride=4), :]
    for r4 in range(4):
        for j in range(4):
            dst = slice((r4 + 4 * j) * len16, (r4 + 4 * j + 1) * len16)
            src = pl.ds(r4 * len4 + j, len16, stride=4)
            q16_sc[dst, :] = c4_sc[0, src, :]
            k16_sc[dst, :] = c4_sc[1, src, :].astype(BF16)
            w16_sc[0, dst, :] = c4_sc[2, src, :].astype(BF16)

    def value_weights(i, carry):
        rows = pl.ds(pl.multiple_of(i * stage_rows, stage_rows), stage_rows)
        k4_sc[rows, :] = c4_sc[1, rows, :].astype(BF16)
        for src, dst in ((v_ref, w1_sc), (c4_sc.at[2], w4_sc), (w16_sc.at[0], w16_sc)):
            v = src[rows, :].astype(F32)
            dst[1, rows, :] = jnp.where(head0_stage, 1.0, v).astype(BF16)
            dst[0, rows, :] = jnp.where(head0_stage, v, 1.0).astype(BF16)
        return carry

    lax.fori_loop(0, n_stage, value_weights, 0, unroll=True)

    def make_pattern(g, q_src, k_src, w_src, s_sc, e_sc, seq_len, w, out_rows):
        per_seq = seq_len // tq

        def where(t):
            r, i = t // per_seq, t % per_seq
            ks = jnp.clip(i * tq - tq // 2, 0, seq_len - w)
            which = jnp.where(i == 0, 0, jnp.where(i == per_seq - 1, 2, 1)) if per_seq > 1 else 0
            qrows = pl.ds(pl.multiple_of(r * seq_len + i * tq, tq), tq)
            krows = pl.ds(pl.multiple_of(r * seq_len + ks, tq // 2), w)
            return qrows, krows, which

        def scores(t):
            qrows, krows, which = where(t)
            q = q_src[qrows, :]
            q2 = jnp.concatenate([jnp.where(head0, q, 0.0), jnp.where(head0, 0.0, q)], axis=0)
            bias = bias_sc[which, :, :w]
            s = _dot_nt(q2.astype(BF16), k_src[krows, :])
            s_sc[t, :, :w] = s + jnp.concatenate([bias, bias], axis=0)

        def softmax(t):
            s = s_sc[t, :, :w]
            m = jnp.max(s, axis=-1, keepdims=True)
            e_sc[t, :, :w] = jnp.exp2(s - m).astype(BF16)
            m_sc[g, out_rows(t), :] = jnp.where(head0, m[:tq], m[tq:])
            msw_sc[g, out_rows(t), :] = jnp.where(head0, m[tq:], m[:tq])

        def values(t):
            _, krows, _ = where(t)
            out0 = _dot(e_sc[t, :tq, :w], w_src[0, krows, :])
            out1 = _dot(e_sc[t, tq:, :w], w_src[1, krows, :])
            rows = out_rows(t)
            o_sc[g, rows, :] = jnp.where(head0, out0, out1)
            d_sc[g, rows, :] = jnp.where(head0, out1, out0)

        return scores, softmax, values

    sc0, sm0, va0 = make_pattern(0, nat_sc.at[0], k_ref, w1_sc, s0_sc, e0_sc, seq, win,
                                 lambda t: pl.ds(pl.multiple_of(t * tq, tq), tq))
    sc1, sm1, va1 = make_pattern(1, c4_sc.at[0], k4_sc, w4_sc, s1_sc, e1_sc, len4, win,
                                 lambda t: pl.ds(pl.multiple_of(t * tq, tq), tq))
    sc2, sm2, va2 = make_pattern(2, q16_sc, k16_sc, w16_sc, s2_sc, e2_sc, len16, len16,
                                 lambda t: pl.ds((t % 4) * len4 + t // 4, len16, stride=4))

    for passes in ((sc0,), (sm0, sc1), (va0, sm1, sc2), (va1, sm2), (va2,)):
        def body(t, carry, passes=passes):
            for f in passes:
                f(t)
            return carry

        lax.fori_loop(0, seq // tq, body, 0, unroll=unroll)

    y_sc = nat_sc.at[0]
    per_class = len4 // stage_rows

    def combine(i, carry):
        r, m0 = i // per_class, (i % per_class) * stage_rows
        rows = pl.ds(pl.multiple_of(r * len4 + m0, stage_rows), stage_rows)
        nat_rows = pl.ds(r + 4 * m0, stage_rows, stride=4)
        m0, m1, m2 = m_sc[0, nat_rows, :], m_sc[1, rows, :], m_sc[2, rows, :]
        m = jnp.maximum(jnp.maximum(m0, m1), m2)
        w0, w1, w2 = jnp.exp2(m0 - m), jnp.exp2(m1 - m), jnp.exp2(m2 - m)
        o = w0 * o_sc[0, nat_rows, :] + w1 * o_sc[1, rows, :] + w2 * o_sc[2, rows, :]
        n0, n1, n2 = msw_sc[0, nat_rows, :], msw_sc[1, rows, :], msw_sc[2, rows, :]
        n = jnp.maximum(jnp.maximum(n0, n1), n2)
        den = (jnp.exp2(n0 - n) * d_sc[0, nat_rows, :] + jnp.exp2(n1 - n) * d_sc[1, rows, :]
               + jnp.exp2(n2 - n) * d_sc[2, rows, :])
        den = pltpu.roll(den, DIL_HD, 1)
        y_sc[nat_rows, :] = o * (1.0 / den)
        return carry

    lax.fori_loop(0, n_stage, combine, 0, unroll=True)

    def gate(i, carry):
        rows = pl.ds(pl.multiple_of(i * stage_rows, stage_rows), stage_rows)
        out_ref[rows, :] = (y_sc[rows, :] * _silu(g_ref[rows, :].astype(F32))).astype(BF16)
        return carry

    lax.fori_loop(0, n_stage, gate, 0, unroll=True)


def _dilated(proj, batch, seq, unroll=16):
    assert seq // 16 == DIL_TQ and all(w // (2 * d) == DIL_HALF for w, d in DIL_PATTERNS)
    blk = lambda off: pl.BlockSpec((seq, LANES), lambda b, p: (b, off // LANES + p))
    n_blocks = seq // DIL_TQ
    f32_rows = pltpu.VMEM((seq, LANES), F32)
    bf16_rows = pltpu.VMEM((seq, LANES), BF16)
    bf16_pair = pltpu.VMEM((2, seq, LANES), BF16)
    s_buf = pltpu.VMEM((n_blocks, 2 * DIL_TQ, 2 * DIL_TQ), F32)
    e_buf = pltpu.VMEM((n_blocks, 2 * DIL_TQ, 2 * DIL_TQ), BF16)
    return pl.pallas_call(
        functools.partial(_dilated_kernel, seq=seq, unroll=unroll),
        grid=(batch, DIL_HEADS // 2),
        in_specs=[blk(OFF_CQ), blk(OFF_CK), blk(OFF_CV), blk(OFF_CG)],
        out_specs=pl.BlockSpec((seq, LANES), lambda b, p: (b, p)),
        out_shape=jax.ShapeDtypeStruct((batch * seq, BRANCH_W), BF16),
        scratch_shapes=[
            pltpu.VMEM((3, seq, LANES), F32),
            pltpu.VMEM((3, seq, LANES), F32), bf16_rows, f32_rows, bf16_rows,
            bf16_pair, bf16_pair, bf16_pair,
            s_buf, s_buf, s_buf, e_buf, e_buf, e_buf,
            pltpu.VMEM((3, seq, LANES), F32), pltpu.VMEM((3, seq, LANES), F32),
            pltpu.VMEM((3, seq, LANES), F32), pltpu.VMEM((3, seq, LANES), F32),
            pltpu.VMEM((3, DIL_TQ, 2 * DIL_TQ), F32),
        ],
        compiler_params=pltpu.CompilerParams(
            dimension_semantics=("arbitrary", "arbitrary"), vmem_limit_bytes=VMEM_LIMIT),
        name="dilated",
    )(proj, proj, proj, proj)


def _merge_kernel(x_ref, ya_ref, yb_ref, yc_ref, m_ref, wb_ref, wo_ref, fg_ref, out_ref, wb_sc, wo_sc,
                  *, final):
    @pl.when(pl.program_id(0) == 0)
    def _():
        wb_sc[...] = wb_ref[...].astype(BF16)
        wo_sc[...] = wo_ref[...].astype(BF16)

    mixed = None
    for gi, y_ref in enumerate((ya_ref, yb_ref, yc_ref)):
        z = _dot(y_ref[...], wb_sc[gi])
        gate = 0.5 * jnp.tanh(0.5 * m_ref[:, gi * D_MODEL:(gi + 1) * D_MODEL].astype(F32)) + 0.5
        mixed = gate * z if mixed is None else mixed + gate * z
    x = x_ref[...] + _dot(mixed.astype(BF16), wo_sc[...])
    if final:
        ms = jnp.mean(x * x, axis=-1, keepdims=True)
        x = x * lax.rsqrt(ms + NORM_EPS) * fg_ref[...]
    out_ref[...] = x


def _merge(x2, ya, yb, yc, proj, wb, wo, final_g, layer, final, tm=1024):
    rows = x2.shape[0]
    yspec = pl.BlockSpec((tm, BRANCH_W), lambda i: (i, 0))
    return pl.pallas_call(
        functools.partial(_merge_kernel, final=final),
        grid=(rows // tm,),
        in_specs=[
            pl.BlockSpec((tm, D_MODEL), lambda i: (i, 0)),
            yspec, yspec, yspec,
            pl.BlockSpec((tm, MERGE_W), lambda i: (i, 0)),
            pl.BlockSpec((None, N_BRANCH, BRANCH_W, D_MODEL), lambda i: (layer, 0, 0, 0),
                         pipeline_mode=pl.Buffered(1)),
            pl.BlockSpec((None, D_MODEL, D_MODEL), lambda i: (layer, 0, 0), pipeline_mode=pl.Buffered(1)),
            pl.BlockSpec((1, D_MODEL), lambda i: (0, 0)),
        ],
        out_specs=pl.BlockSpec((tm, D_MODEL), lambda i: (i, 0)),
        out_shape=jax.ShapeDtypeStruct((rows, D_MODEL), F32),
        scratch_shapes=[pltpu.VMEM((N_BRANCH, BRANCH_W, D_MODEL), BF16), pltpu.VMEM((D_MODEL, D_MODEL), BF16)],
        compiler_params=pltpu.CompilerParams(
            dimension_semantics=("arbitrary",), vmem_limit_bytes=VMEM_LIMIT),
        name="merge",
    )(x2, ya, yb, yc, proj, wb, wo, final_g)


def _rope_tables(seq, dim):
    inv = 1.0 / (ROPE_THETA ** (jnp.arange(0, dim, 2, dtype=F32) / dim))
    ang = jnp.arange(seq, dtype=F32)[:, None] * inv[None, :]
    cos, sin = jnp.cos(ang), jnp.sin(ang)
    reps = LANES // dim
    cos_t = jnp.tile(jnp.concatenate([cos, cos], axis=-1), (1, reps))
    sin_t = jnp.tile(jnp.concatenate([-sin, sin], axis=-1), (1, reps))
    return cos_t, sin_t


def kernel(x, norm_g, w_in, gla_gate_up, gla_gate_b, gla_norm_g, diff_lambda, diff_norm_g, w_branch, w_out, final_norm_g):
    batch, seq, _ = x.shape
    depth = w_in.shape[0]
    cos, sin = _rope_tables(seq, DIFF_D)

    lo, hi = LOWRANK_OFF, LOWRANK_OFF + LOWRANK_W
    merge_off = IN_COLS - MERGE_W
    w_parts = (w_in[:, :, merge_off:].astype(BF16), w_in[:, :, :lo].astype(BF16),
               w_in[:, :, hi:merge_off].astype(BF16))
    w_lr = w_in[:, :, lo:hi].astype(BF16)
    w_low = jnp.concatenate([w_lr, w_lr, w_lr, jnp.zeros_like(w_lr)], axis=-1)
    gu = gla_gate_up.reshape(depth, 2, GLA_RANK, GLA_HEADS, GLA_DK)
    zeros = jnp.zeros_like(gu[:, 0])
    gu_f = jnp.concatenate([gu[:, 0], zeros], axis=-1).reshape(depth, GLA_RANK, GLA_HEADS * LANES)
    gu_b = jnp.concatenate([zeros, gu[:, 1]], axis=-1).reshape(depth, GLA_RANK, GLA_HEADS * LANES)
    gu_cat = jnp.concatenate([gu_f, gu_b], axis=1)
    gu_hi = gu_cat.astype(BF16)
    gu_lo = (gu_cat - gu_hi.astype(F32)).astype(BF16)
    gate_up = jnp.concatenate([gu_hi, gu_lo, gu_hi, jnp.zeros_like(gu_hi)], axis=1)
    gb = gla_gate_b.reshape(depth, 2, GLA_HEADS, GLA_DK)
    gate_b = jnp.concatenate([gb[:, 0], gb[:, 1]], axis=-1).reshape(depth, 1, GLA_HEADS * LANES)
    lam_init = jnp.asarray([0.8 - 0.6 * math.exp(-0.3 * layer) for layer in range(depth)],
                           F32).reshape(depth, 1, 1)
    norm_g3 = norm_g[:, None, :]
    gla_norm_g3 = gla_norm_g[:, None, :]
    diff_norm_g3 = diff_norm_g[:, None, :]

    x2 = x.reshape(batch * seq, D_MODEL)
    for layer in range(depth):
        proj, la = _in_proj(x2, norm_g3, w_parts, w_low, gate_up, gate_b, cos, sin, layer, seq)
        ya = _gla(proj, la, gla_norm_g3, layer, batch, seq)
        yb = _diff(proj, diff_lambda, lam_init, diff_norm_g3, layer, batch, seq)
        yc = _dilated(proj, batch, seq)
        x2 = _merge(x2, ya, yb, yc, proj, w_branch, w_out, final_norm_g[None], layer, final=layer == depth - 1)
    return x2.reshape(batch, seq, D_MODEL)
```
